```python
import math
import jax, jax.numpy as jnp
from jax import lax
import numpy as np

D_MODEL = 2048
BATCH = 8
SEQ = 2048
DEPTH = 2

CHUNK = 64
Q_BLOCK = 128
N_MIXERS = 2
N_GLA = (DEPTH + 1) // 2
N_MLA = DEPTH // 2
DEEPNORM_ALPHA = float((2 * DEPTH) ** 0.25)
DEEPNORM_BETA = float((8 * DEPTH) ** -0.25)
LN_EPS = 1e-5
RMS_EPS = 1e-6

GLA_HEADS = 4
GLA_DK = D_MODEL // 2 // GLA_HEADS
GLA_DV = D_MODEL // GLA_HEADS
GLA_HK = GLA_HEADS * GLA_DK
GLA_HV = GLA_HEADS * GLA_DV
GLA_GATE_RANK = 16
GLA_TAU = 16.0
GLA_IN = 2 * GLA_HK + 2 * GLA_HV + GLA_GATE_RANK

MLA_HEADS = 16
MLA_Q_RANK = D_MODEL // 4
MLA_KV_RANK = D_MODEL // 4
MLA_NOPE = 128
MLA_ROPE = 64
MLA_V = 128
MLA_QK = MLA_NOPE + MLA_ROPE
MLA_HV = MLA_HEADS * MLA_V
MLA_IN = MLA_Q_RANK + MLA_KV_RANK + MLA_ROPE + MLA_HV
ROPE_THETA = 10000.0

kernel_name = "hybrid_gla_mla_deepnorm_stream"


def layer_norm(x, g, b):
    xf = x.astype(jnp.float32)
    mu = jnp.mean(xf, axis=-1, keepdims=True)
    xc = xf - mu
    var = jnp.mean(xc * xc, axis=-1, keepdims=True)
    return xc * lax.rsqrt(var + LN_EPS) * g.astype(jnp.float32) + b.astype(jnp.float32)


def rms_norm(x, g):
    xf = x.astype(jnp.float32)
    return xf * lax.rsqrt(jnp.mean(xf * xf, axis=-1, keepdims=True) + RMS_EPS) * g.astype(jnp.float32)


def apply_rope(t, positions):
    half = t.shape[-1] // 2
    inv_freq = ROPE_THETA ** (-jnp.arange(half, dtype=jnp.float32) / half)
    ang = positions.astype(jnp.float32)[..., None] * inv_freq
    cos = jnp.cos(ang)[:, :, None, :]
    sin = jnp.sin(ang)[:, :, None, :]
    t1 = t[..., :half].astype(jnp.float32)
    t2 = t[..., half:].astype(jnp.float32)
    return jnp.concatenate([t1 * cos - t2 * sin, t1 * sin + t2 * cos], axis=-1).astype(t.dtype)


def gla_mixer(x, w_in, w_a2, b_a, g_out, w_out):
    B, S, _ = x.shape
    nc = S // CHUNK
    h = x @ w_in
    q, k, v, z, a_lr = jnp.split(h, [GLA_HK, 2 * GLA_HK, 2 * GLA_HK + GLA_HV, 2 * GLA_HK + 2 * GLA_HV], axis=-1)
    log_a = jax.nn.log_sigmoid((a_lr @ w_a2 + b_a).astype(jnp.float32)) / GLA_TAU

    def to_chunks(t, d):
        return t.reshape(B, nc, CHUNK, GLA_HEADS, d).transpose(1, 0, 3, 2, 4)

    q_c = to_chunks(q * (GLA_DK ** -0.5), GLA_DK)
    k_c = to_chunks(k, GLA_DK)
    v_c = to_chunks(v, GLA_DV)
    la = to_chunks(log_a, GLA_DK)
    cum = jnp.cumsum(la, axis=3)
    total = cum[:, :, :, -1]
    k_dec = k_c * jnp.exp(total[:, :, :, None, :] - cum)
    chunk_decay = jnp.exp(total)

    def step(state, inp):
        q_i, k_i, v_i, a_i = inp
        state = a_i[..., None] * state + jnp.einsum('bhcd,bhce->bhde', k_i, v_i)
        o_i = jnp.einsum('bhcd,bhde->bhce', q_i, state)
        return state, o_i

    s0 = jnp.zeros((B, GLA_HEADS, GLA_DK, GLA_DV), jnp.float32)
    _, o = lax.scan(step, s0, (q_c, k_dec, v_c, chunk_decay))
    o = o.transpose(1, 0, 3, 2, 4).reshape(B, S, GLA_HEADS, GLA_DV)
    o = rms_norm(o, g_out).reshape(B, S, GLA_HV)
    o = o * jax.nn.silu(z.astype(jnp.float32))
    return o @ w_out


def mla_mixer(x, positions, w_in, g_q, w_uq, g_kv, w_ukv, w_out):
    B, S, _ = x.shape
    h = x @ w_in
    c_q, c_kv, k_rope, z = jnp.split(h, [MLA_Q_RANK, MLA_Q_RANK + MLA_KV_RANK, MLA_Q_RANK + MLA_KV_RANK + MLA_ROPE], axis=-1)
    c_q = rms_norm(c_q, g_q).astype(x.dtype)
    c_kv = rms_norm(c_kv, g_kv).astype(x.dtype)
    q = (c_q @ w_uq).reshape(B, S, MLA_HEADS, MLA_QK)
    kv = (c_kv @ w_ukv).reshape(B, S, MLA_HEADS, MLA_NOPE + MLA_V)
    q_nope, q_rope = q[..., :MLA_NOPE], q[..., MLA_NOPE:]
    k_nope, v = kv[..., :MLA_NOPE], kv[..., MLA_NOPE:]
    q_rope = apply_rope(q_rope, positions)
    k_rope = apply_rope(k_rope[:, :, None, :], positions)
    q_full = jnp.concatenate([q_nope, q_rope], axis=-1) * (MLA_QK ** -0.5)
    k_full = jnp.concatenate([k_nope, jnp.broadcast_to(k_rope, (B, S, MLA_HEADS, MLA_ROPE))], axis=-1)

    nb = S // Q_BLOCK
    q_blocks = q_full.reshape(B, nb, Q_BLOCK, MLA_HEADS, MLA_QK).transpose(1, 0, 2, 3, 4)
    key_chunk = jnp.arange(S) // CHUNK

    def attend(args):
        q_blk, blk = args
        q_chunk = (blk * Q_BLOCK + jnp.arange(Q_BLOCK)) // CHUNK
        s = jnp.einsum('bqhd,bkhd->bhqk', q_blk, k_full).astype(jnp.float32)
        mask = key_chunk[None, :] <= q_chunk[:, None]
        s = jnp.where(mask[None, None], s, -jnp.inf)
        p = jax.nn.softmax(s, axis=-1)
        return jnp.einsum('bhqk,bkhd->bqhd', p.astype(v.dtype), v)

    o = lax.map(attend, (q_blocks, jnp.arange(nb)))
    o = o.transpose(1, 0, 2, 3, 4).reshape(B, S, MLA_HV)
    o = o * jax.nn.silu(z.astype(jnp.float32))
    return o @ w_out


def setup_inputs(seed: int = 0) -> dict:
    key = jax.random.key(seed)
    ks = jax.random.split(key, 16)
    f32 = jnp.float32
    nrm = lambda k, shape, scale: jax.random.normal(k, shape, f32) * scale
    x = jax.random.normal(ks[0], (BATCH, SEQ, D_MODEL), f32)
    offset = jax.random.randint(ks[1], (BATCH, 1), 0, 64, dtype=jnp.int32) * CHUNK
    positions = (jnp.arange(SEQ, dtype=jnp.int32)[None, :] + offset).astype(jnp.int32)
    return {
        "x": x,
        "positions": positions,
        "gla_w_in": nrm(ks[2], (N_GLA, D_MODEL, GLA_IN), D_MODEL ** -0.5),
        "gla_w_a2": nrm(ks[3], (N_GLA, GLA_GATE_RANK, GLA_HK), GLA_GATE_RANK ** -0.5),
        "gla_b_a": nrm(ks[4], (N_GLA, GLA_HK), 0.1),
        "gla_g_out": 1.0 + nrm(ks[5], (N_GLA, GLA_HEADS, GLA_DV), 0.02),
        "gla_w_out": nrm(ks[6], (N_GLA, GLA_HV, D_MODEL), GLA_HV ** -0.5 * DEEPNORM_BETA),
        "mla_w_in": nrm(ks[7], (N_MLA, D_MODEL, MLA_IN), D_MODEL ** -0.5),
        "mla_g_q": 1.0 + nrm(ks[8], (N_MLA, MLA_Q_RANK), 0.02),
        "mla_w_uq": nrm(ks[9], (N_MLA, MLA_Q_RANK, MLA_HEADS * MLA_QK), MLA_Q_RANK ** -0.5),
        "mla_g_kv": 1.0 + nrm(ks[10], (N_MLA, MLA_KV_RANK), 0.02),
        "mla_w_ukv": nrm(ks[11], (N_MLA, MLA_KV_RANK, MLA_HEADS * (MLA_NOPE + MLA_V)), MLA_KV_RANK ** -0.5),
        "mla_w_out": nrm(ks[12], (N_MLA, MLA_HV, D_MODEL), MLA_HV ** -0.5 * DEEPNORM_BETA),
        "ln_g": 1.0 + nrm(ks[13], (DEPTH, D_MODEL), 0.02),
        "ln_b": nrm(ks[14], (DEPTH, D_MODEL), 0.02),
    }


def reference(x, positions, gla_w_in, gla_w_a2, gla_b_a, gla_g_out, gla_w_out,
              mla_w_in, mla_g_q, mla_w_uq, mla_g_kv, mla_w_ukv, mla_w_out, ln_g, ln_b):
    for i in range(DEPTH):
        j = i // N_MIXERS
        if i % N_MIXERS == 0:
            y = gla_mixer(x, gla_w_in[j], gla_w_a2[j], gla_b_a[j], gla_g_out[j], gla_w_out[j])
        else:
            y = mla_mixer(x, positions, mla_w_in[j], mla_g_q[j], mla_w_uq[j], mla_g_kv[j],
                          mla_w_ukv[j], mla_w_out[j])
        x = layer_norm(DEEPNORM_ALPHA * x + y, ln_g[i], ln_b[i]).astype(x.dtype)
    return x
```

```python
import functools
import math

import jax
import jax.numpy as jnp
from jax import lax
from jax.experimental import pallas as pl
from jax.experimental.pallas import tpu as pltpu

BF16 = jnp.bfloat16
F32 = jnp.float32

D_MODEL = 2048
DEPTH = 2
CHUNK = 64
DEEPNORM_ALPHA = float((2 * DEPTH) ** 0.25)
LN_EPS = 1e-5
RMS_EPS = 1e-6

GLA_HEADS = 4
GLA_DK = 256
GLA_DV = 512
GLA_HK = GLA_HEADS * GLA_DK
GLA_HV = GLA_HEADS * GLA_DV
GLA_GATE_RANK = 16
GLA_TAU = 16.0

MLA_HEADS = 16
MLA_Q_RANK = 512
MLA_KV_RANK = 512
MLA_NOPE = 128
MLA_ROPE = 64
MLA_V = 128
MLA_QK = MLA_NOPE + MLA_ROPE
MLA_HV = MLA_HEADS * MLA_V
ROPE_THETA = 10000.0

LANES = 128
SUBLANES = 8
MXU_DIM = 256
VMEM_LIMIT = 56 * 1024 * 1024

PROJ_BM = 256
OUT_BM = 512
GLA_TC = 512
ATT_T = 256
ROPE_TAB_R = 2048
QHEAD = 2 * LANES


def _params(*sem):
    return pltpu.CompilerParams(dimension_semantics=sem, vmem_limit_bytes=VMEM_LIMIT)


def _resident(shape):
    zeros = (0,) * len(shape)
    return pl.BlockSpec(shape, lambda *_: zeros, pipeline_mode=pl.Buffered(1))


def _dot(a, b):
    return jnp.dot(a, b, preferred_element_type=F32)


def _dot_nt(a, b):
    return lax.dot_general(a, b, (((1,), (1,)), ((), ())), preferred_element_type=F32)


def _silu(z):
    return z * (1.0 / (1.0 + jnp.exp(-z)))


def _rms_rows(x, g):
    ms = jnp.mean(x * x, axis=-1, keepdims=True)
    return x * lax.rsqrt(ms + RMS_EPS) * g


def _rope_tab_kernel(pos_ref, freq_ref, tabT_ref, tab_ref):
    pos = pos_ref[...].astype(F32)
    ang = freq_ref[...] * pos
    c = jnp.cos(ang)
    s = jnp.sin(ang)
    r = lax.broadcasted_iota(jnp.int32, ang.shape, 0)
    t = jnp.where(r < 2 * 32, c, jnp.where(r < 3 * 32, -s, s))
    tabT_ref[...] = t
    tab_ref[...] = t.T


def _rope_tables(positions):
    n = positions.size
    half = MLA_ROPE // 2
    inv_freq = ROPE_THETA ** (-jnp.arange(half, dtype=F32) / half)
    freq_col = jnp.tile(inv_freq, LANES // half).reshape(LANES, 1)
    pos_row = positions.reshape(1, n)
    return pl.pallas_call(
        _rope_tab_kernel,
        grid=(n // ROPE_TAB_R,),
        in_specs=[pl.BlockSpec((1, ROPE_TAB_R), lambda i: (0, i)),
                  pl.BlockSpec((LANES, 1), lambda i: (0, 0))],
        out_specs=[pl.BlockSpec((LANES, ROPE_TAB_R), lambda i: (0, i)),
                   pl.BlockSpec((ROPE_TAB_R, LANES), lambda i: (i, 0))],
        out_shape=[jax.ShapeDtypeStruct((LANES, n), F32),
                   jax.ShapeDtypeStruct((n, LANES), F32)],
        compiler_params=_params("arbitrary"),
        name="rope_tables",
    )(pos_row, freq_col)


def _gla_in_kernel(x_ref, wqkz_ref, wvT_ref, wa_ref, q_ref, k_ref, z_ref, vT_ref, a_ref):
    xb = x_ref[...].astype(BF16)
    q_ref[...] = _dot(xb, wqkz_ref[:, :GLA_HK]).astype(BF16)
    k_ref[...] = _dot(xb, wqkz_ref[:, GLA_HK:2 * GLA_HK])
    z_ref[...] = _dot(xb, wqkz_ref[:, 2 * GLA_HK:]).astype(BF16)
    vT_ref[...] = _dot_nt(wvT_ref[...], xb).astype(BF16)
    a_ref[...] = _dot(xb, wa_ref[...])


def _gla_in_proj(x2, w_in):
    n = x2.shape[0]
    bm = PROJ_BM
    wq = w_in[:, :GLA_HK]
    wk = w_in[:, GLA_HK:2 * GLA_HK]
    wv = w_in[:, 2 * GLA_HK:2 * GLA_HK + GLA_HV]
    wz = w_in[:, 2 * GLA_HK + GLA_HV:2 * GLA_HK + 2 * GLA_HV]
    wa = w_in[:, 2 * GLA_HK + 2 * GLA_HV:]
    wqkz = jnp.concatenate([wq, wk, wz], axis=1).astype(BF16)
    wvT = wv.T.astype(BF16)
    wa_p = jnp.pad(wa, ((0, 0), (0, LANES - GLA_GATE_RANK))).astype(BF16)
    nqkz = wqkz.shape[1]
    return pl.pallas_call(
        _gla_in_kernel,
        grid=(n // bm,),
        in_specs=[pl.BlockSpec((bm, D_MODEL), lambda i: (i, 0)),
                  _resident((D_MODEL, nqkz)),
                  _resident((GLA_HV, D_MODEL)),
                  _resident((D_MODEL, LANES))],
        out_specs=[pl.BlockSpec((bm, GLA_HK), lambda i: (i, 0)),
                   pl.BlockSpec((bm, GLA_HK), lambda i: (i, 0)),
                   pl.BlockSpec((bm, GLA_HV), lambda i: (i, 0)),
                   pl.BlockSpec((GLA_HV, bm), lambda i: (0, i)),
                   pl.BlockSpec((bm, LANES), lambda i: (i, 0))],
        out_shape=[jax.ShapeDtypeStruct((n, GLA_HK), BF16),
                   jax.ShapeDtypeStruct((n, GLA_HK), F32),
                   jax.ShapeDtypeStruct((n, GLA_HV), BF16),
                   jax.ShapeDtypeStruct((GLA_HV, n), BF16),
                   jax.ShapeDtypeStruct((n, LANES), F32)],
        compiler_params=_params("arbitrary"),
        name="gla_in_proj",
    )(x2, wqkz, wvT, wa_p)


def _chunk_cumsum(la):
    t, n = la.shape
    x = la.reshape(t // SUBLANES, SUBLANES, n)
    row = lax.broadcasted_iota(jnp.int32, x.shape, 1)
    s = 1
    while s < SUBLANES:
        x = x + jnp.where(row >= s, pltpu.roll(x, s, axis=1), 0.0)
        s *= 2
    tiles_per_chunk = CHUNK // SUBLANES
    out = []
    carry = None
    for i in range(t // SUBLANES):
        tile = x[i]
        if i % tiles_per_chunk != 0:
            tile = tile + carry
        carry = tile[SUBLANES - 1:SUBLANES, :]
        out.append(tile)
    return jnp.concatenate(out, axis=0)


def _gla_rec_kernel(q_ref, k_ref, vT_ref, z_ref, a_ref, wa2_ref, ba_ref, g_ref,
                    o_ref, st_ref):
    @pl.when(pl.program_id(1) == 0)
    def _():
        st_ref[...] = jnp.zeros_like(st_ref)

    pre = _dot(a_ref[...].astype(BF16), wa2_ref[...]) + ba_ref[...]
    la = (jnp.minimum(pre, 0.0) - jnp.log1p(jnp.exp(-jnp.abs(pre)))) * (1.0 / GLA_TAU)
    cum = _chunk_cumsum(la)
    qscale = GLA_DK ** -0.5
    pair = 2 * CHUNK
    for c in range(GLA_TC // CHUNK):
        r0 = c * CHUNK
        cum_c = cum[r0:r0 + CHUNK, :]
        tot = cum_c[CHUNK - 1:CHUNK, :]
        kd = (k_ref[r0:r0 + CHUNK, :] * jnp.exp(tot - cum_c)).astype(BF16)
        dec = jnp.exp(tot)
        zeros = jnp.zeros_like(kd)
        kd_pair = jnp.concatenate([kd, zeros] if c % 2 == 0 else [zeros, kd], axis=0)
        p0 = (c // 2) * pair
        q_c = q_ref[r0:r0 + CHUNK, :]
        for h in range(GLA_HEADS):
            ks = slice(h * GLA_DK, (h + 1) * GLA_DK)
            vs = slice(h * GLA_DV, (h + 1) * GLA_DV)
            kv = _dot(vT_ref[vs, p0:p0 + pair], kd_pair[:, ks])
            s_new = st_ref[h] * dec[:, ks] + kv
            st_ref[h] = s_new
            o = _dot_nt(q_c[:, ks], s_new.astype(BF16)) * qscale
            o = _rms_rows(o, g_ref[:, vs])
            o = o * _silu(z_ref[r0:r0 + CHUNK, vs].astype(F32))
            o_ref[r0:r0 + CHUNK, vs] = o.astype(BF16)


def _gla_recurrence(q, k, vT, z, a_lr, w_a2, b_a, g_out, batch, seq):
    n = q.shape[0]
    tc = GLA_TC
    steps = seq // tc
    wa2_p = jnp.pad(w_a2, ((0, LANES - GLA_GATE_RANK), (0, 0))).astype(BF16)
    row = lambda b, t: (b * steps + t, 0)
    return pl.pallas_call(
        _gla_rec_kernel,
        grid=(batch, steps),
        in_specs=[pl.BlockSpec((tc, GLA_HK), row),
                  pl.BlockSpec((tc, GLA_HK), row),
                  pl.BlockSpec((GLA_HV, tc), lambda b, t: (0, b * steps + t)),
                  pl.BlockSpec((tc, GLA_HV), row),
                  pl.BlockSpec((tc, LANES), row),
                  _resident((LANES, GLA_HK)),
                  _resident((1, GLA_HK)),
                  _resident((1, GLA_HV))],
        out_specs=pl.BlockSpec((tc, GLA_HV), row),
        out_shape=jax.ShapeDtypeStruct((n, GLA_HV), BF16),
        scratch_shapes=[pltpu.VMEM((GLA_HEADS, GLA_DV, GLA_DK), F32)],
        compiler_params=_params("arbitrary", "arbitrary"),
        name="gla_recurrence",
    )(q, k, vT, z, a_lr, wa2_p, b_a.reshape(1, GLA_HK), g_out.reshape(1, GLA_HV))


def _out_ln_kernel(o_ref, x_ref, w_ref, g_ref, b_ref, y_ref):
    r = DEEPNORM_ALPHA * x_ref[...] + _dot(o_ref[...], w_ref[...])
    mu = jnp.mean(r, axis=-1, keepdims=True)
    rc = r - mu
    var = jnp.mean(rc * rc, axis=-1, keepdims=True)
    y_ref[...] = rc * lax.rsqrt(var + LN_EPS) * g_ref[...] + b_ref[...]


def _out_proj_ln(o, x2, w_out, g, b):
    n = x2.shape[0]
    bm = OUT_BM
    return pl.pallas_call(
        _out_ln_kernel,
        grid=(n // bm,),
        in_specs=[pl.BlockSpec((bm, o.shape[1]), lambda i: (i, 0)),
                  pl.BlockSpec((bm, D_MODEL), lambda i: (i, 0)),
                  _resident((o.shape[1], D_MODEL)),
                  _resident((1, D_MODEL)),
                  _resident((1, D_MODEL))],
        out_specs=pl.BlockSpec((bm, D_MODEL), lambda i: (i, 0)),
        out_shape=jax.ShapeDtypeStruct((n, D_MODEL), F32),
        compiler_params=_params("arbitrary"),
        name="out_proj_ln",
    )(o, x2, w_out.astype(BF16), g.reshape(1, D_MODEL), b.reshape(1, D_MODEL))


def _mla_in_kernel(x_ref, wc_ref, wz_ref, wr_ref, gq_ref, gkv_ref, tab_ref,
                   cq_ref, ckv_ref, z_ref, kr_ref):
    xb = x_ref[...].astype(BF16)
    c = _dot(xb, wc_ref[...])
    cq_ref[...] = _rms_rows(c[:, :MLA_Q_RANK], gq_ref[...]).astype(BF16)
    ckv_ref[...] = _rms_rows(c[:, MLA_Q_RANK:], gkv_ref[...]).astype(BF16)
    z_ref[...] = _dot(xb, wz_ref[...]).astype(BF16)
    a = _dot(xb, wr_ref[...]) * tab_ref[...]
    kr_ref[...] = (a + pltpu.roll(a, LANES // 2, axis=1)).astype(BF16)


def _mla_in_proj(x2, w_in, g_q, g_kv, tab):
    n = x2.shape[0]
    bm = PROJ_BM
    half = MLA_ROPE // 2
    nc = MLA_Q_RANK + MLA_KV_RANK
    wc = w_in[:, :nc].astype(BF16)
    wr = w_in[:, nc:nc + MLA_ROPE]
    wz = w_in[:, nc + MLA_ROPE:].astype(BF16)
    t1, t2 = wr[:, :half], wr[:, half:]
    wr4 = jnp.concatenate([t1, t2, t2, t1], axis=1).astype(BF16)
    row = lambda i: (i, 0)
    return pl.pallas_call(
        _mla_in_kernel,
        grid=(n // bm,),
        in_specs=[pl.BlockSpec((bm, D_MODEL), row),
                  _resident((D_MODEL, nc)),
                  _resident((D_MODEL, MLA_HV)),
                  _resident((D_MODEL, LANES)),
                  _resident((1, MLA_Q_RANK)),
                  _resident((1, MLA_KV_RANK)),
                  pl.BlockSpec((bm, LANES), row)],
        out_specs=[pl.BlockSpec((bm, MLA_Q_RANK), row),
                   pl.BlockSpec((bm, MLA_KV_RANK), row),
                   pl.BlockSpec((bm, MLA_HV), row),
                   pl.BlockSpec((bm, LANES), row)],
        out_shape=[jax.ShapeDtypeStruct((n, MLA_Q_RANK), BF16),
                   jax.ShapeDtypeStruct((n, MLA_KV_RANK), BF16),
                   jax.ShapeDtypeStruct((n, MLA_HV), BF16),
                   jax.ShapeDtypeStruct((n, LANES), BF16)],
        compiler_params=_params("arbitrary"),
        name="mla_in_proj",
    )(x2, wc, wz, wr4, g_q.reshape(1, -1), g_kv.reshape(1, -1), tab)


def _mla_up_kernel(cq_ref, ckv_ref, wqT_ref, wk_ref, wvT_ref, tabT_ref,
                   qT_ref, k_ref, vT_ref):
    cq = cq_ref[...]
    ckv = ckv_ref[...]
    scale = MLA_QK ** -0.5
    tab = tabT_ref[...] * scale
    group = 4 * QHEAD
    for g0 in range(0, MLA_HEADS * QHEAD, group):
        acc = _dot_nt(wqT_ref[g0:g0 + group, :], cq)
        for h0 in range(0, group, QHEAD):
            r0 = g0 + h0
            qT_ref[r0:r0 + LANES, :] = (acc[h0:h0 + LANES] * scale).astype(BF16)
            qT_ref[r0 + LANES:r0 + QHEAD, :] = (acc[h0 + LANES:h0 + QHEAD] * tab).astype(BF16)
    k_ref[...] = _dot(ckv, wk_ref[...]).astype(BF16)
    vT_ref[...] = _dot_nt(wvT_ref[...], ckv).astype(BF16)


def _mla_up_proj(cq, ckv, w_uq, w_ukv, tabT):
    n = cq.shape[0]
    bm = OUT_BM
    half = MLA_ROPE // 2
    wq = w_uq.reshape(MLA_Q_RANK, MLA_HEADS, MLA_QK)
    nope = wq[:, :, :MLA_NOPE]
    t1 = wq[:, :, MLA_NOPE:MLA_NOPE + half]
    t2 = wq[:, :, MLA_NOPE + half:]
    wq4 = jnp.concatenate([nope, t1, t2, t2, t1], axis=2)
    wqT = wq4.reshape(MLA_Q_RANK, MLA_HEADS * QHEAD).T.astype(BF16)
    wkv = w_ukv.reshape(MLA_KV_RANK, MLA_HEADS, MLA_NOPE + MLA_V)
    wk = wkv[:, :, :MLA_NOPE].reshape(MLA_KV_RANK, MLA_HEADS * MLA_NOPE).astype(BF16)
    wvT = wkv[:, :, MLA_NOPE:].reshape(MLA_KV_RANK, MLA_HV).T.astype(BF16)
    row = lambda i: (i, 0)
    col = lambda i: (0, i)
    return pl.pallas_call(
        _mla_up_kernel,
        grid=(n // bm,),
        in_specs=[pl.BlockSpec((bm, MLA_Q_RANK), row),
                  pl.BlockSpec((bm, MLA_KV_RANK), row),
                  _resident((MLA_HEADS * QHEAD, MLA_Q_RANK)),
                  _resident((MLA_KV_RANK, MLA_HEADS * MLA_NOPE)),
                  _resident((MLA_HV, MLA_KV_RANK)),
                  pl.BlockSpec((LANES, bm), col)],
        out_specs=[pl.BlockSpec((MLA_HEADS * QHEAD, bm), col),
                   pl.BlockSpec((bm, MLA_HEADS * MLA_NOPE), row),
                   pl.BlockSpec((MLA_HV, bm), col)],
        out_shape=[jax.ShapeDtypeStruct((MLA_HEADS * QHEAD, n), BF16),
                   jax.ShapeDtypeStruct((n, MLA_HEADS * MLA_NOPE), BF16),
                   jax.ShapeDtypeStruct((MLA_HV, n), BF16)],
        compiler_params=_params("arbitrary"),
        name="mla_up_proj",
    )(cq, ckv, wqT, wk, wvT, tabT)


def _mla_attn_kernel(qT_ref, k_ref, kr_ref, vT_ref, z_ref, o_ref):
    t = ATT_T
    nt = qT_ref.shape[1] // t
    kk = lax.broadcasted_iota(jnp.int32, (t, t), 0) // CHUNK
    qq = lax.broadcasted_iota(jnp.int32, (t, t), 1) // CHUNK
    visible = kk <= qq
    for i in range(nt):
        qT = qT_ref[:, i * t:(i + 1) * t]
        m = l = acc = None
        for j in range(i + 1):
            ks = slice(j * t, (j + 1) * t)
            kcat = jnp.concatenate([k_ref[ks, :], kr_ref[ks, :]], axis=1)
            s = _dot(kcat, qT)
            if j == i:
                s = jnp.where(visible, s, -jnp.inf)
            m_blk = jnp.max(s, axis=0, keepdims=True)
            if j == 0:
                m = m_blk
                p = jnp.exp(s - m)
                l = jnp.sum(p, axis=0, keepdims=True)
                acc = _dot(vT_ref[:, ks], p.astype(BF16))
            else:
                m_new = jnp.maximum(m, m_blk)
                alpha = jnp.exp(m - m_new)
                p = jnp.exp(s - m_new)
                l = alpha * l + jnp.sum(p, axis=0, keepdims=True)
                acc = alpha * acc + _dot(vT_ref[:, ks], p.astype(BF16))
                m = m_new
        o = (acc * (1.0 / l)).T
        rows = slice(i * t, (i + 1) * t)
        o_ref[rows, :] = (o * _silu(z_ref[rows, :].astype(F32))).astype(BF16)


def _mla_attention(qT, k, kr, vT, z, batch, seq):
    n = k.shape[0]
    return pl.pallas_call(
        _mla_attn_kernel,
        grid=(batch, MLA_HEADS),
        in_specs=[pl.BlockSpec((QHEAD, seq), lambda b, h: (h, b)),
                  pl.BlockSpec((seq, MLA_NOPE), lambda b, h: (b, h)),
                  pl.BlockSpec((seq, LANES), lambda b, h: (b, 0)),
                  pl.BlockSpec((MLA_V, seq), lambda b, h: (h, b)),
                  pl.BlockSpec((seq, MLA_V), lambda b, h: (b, h))],
        out_specs=pl.BlockSpec((seq, MLA_V), lambda b, h: (b, h)),
        out_shape=jax.ShapeDtypeStruct((n, MLA_HV), BF16),
        compiler_params=_params("arbitrary", "arbitrary"),
        name="mla_attention",
    )(qT, k, kr, vT, z)


def kernel(x, positions, gla_w_in, gla_w_a2, gla_b_a, gla_g_out, gla_w_out,
           mla_w_in, mla_g_q, mla_w_uq, mla_g_kv, mla_w_ukv, mla_w_out, ln_g, ln_b):
    batch, seq, d = x.shape
    assert d == D_MODEL and seq % GLA_TC == 0 and seq % ATT_T == 0
    x2 = x.reshape(batch * seq, d)

    q, k, z, vT, a_lr = _gla_in_proj(x2, gla_w_in[0])
    o = _gla_recurrence(q, k, vT, z, a_lr, gla_w_a2[0], gla_b_a[0], gla_g_out[0], batch, seq)
    x2 = _out_proj_ln(o, x2, gla_w_out[0], ln_g[0], ln_b[0])

    tabT, tab = _rope_tables(positions)
    cq, ckv, z, kr = _mla_in_proj(x2, mla_w_in[0], mla_g_q[0], mla_g_kv[0], tab)
    qT, kn, vT = _mla_up_proj(cq, ckv, mla_w_uq[0], mla_w_ukv[0], tabT)
    o = _mla_attention(qT, kn, kr, vT, z, batch, seq)
    x2 = _out_proj_ln(o, x2, mla_w_out[0], ln_g[1], ln_b[1])
    return x2.reshape(batch, seq, d)
```

```python
import functools
import math

import jax
import jax.numpy as jnp
from jax import lax
from jax.experimental import pallas as pl
from jax.experimental.pallas import tpu as pltpu

BF16 = jnp.bfloat16
F32 = jnp.float32

D_MODEL = 2048
DEPTH = 2
CHUNK = 64
DEEPNORM_ALPHA = float((2 * DEPTH) ** 0.25)
LN_EPS = 1e-5
RMS_EPS = 1e-6

GLA_HEADS = 4
GLA_DK = 256
GLA_DV = 512
GLA_HK = GLA_HEADS * GLA_DK
GLA_HV = GLA_HEADS * GLA_DV
GLA_GATE_RANK = 16
GLA_TAU = 16.0

MLA_HEADS = 16
MLA_Q_RANK = 512
MLA_KV_RANK = 512
MLA_NOPE = 128
MLA_ROPE = 64
MLA_V = 128
MLA_QK = MLA_NOPE + MLA_ROPE
MLA_HV = MLA_HEADS * MLA_V
ROPE_THETA = 10000.0

LANES = 128
SUBLANES = 8
BF16_ROWS = 16
MXU_DIM = 256
VMEM_LIMIT = 56 * 1024 * 1024

PROJ_BM = 256
OUT_BM = 512
GLA_TC = 512
ATT_T = 256
ATT_HB = 4
ROPE_TAB_R = 2048
QHEAD = 2 * LANES


def _params(*sem):
    return pltpu.CompilerParams(dimension_semantics=sem, vmem_limit_bytes=VMEM_LIMIT)


def _resident(shape):
    zeros = (0,) * len(shape)
    return pl.BlockSpec(shape, lambda *_: zeros, pipeline_mode=pl.Buffered(1))


def _dot(a, b):
    return jnp.dot(a, b, preferred_element_type=F32)


def _dot_nt(a, b):
    return lax.dot_general(a, b, (((1,), (1,)), ((), ())), preferred_element_type=F32)


def _silu(z):
    return z * (1.0 / (1.0 + jnp.exp(-z)))


def _rms_rows(x, g):
    ms = jnp.mean(x * x, axis=-1, keepdims=True)
    return x * lax.rsqrt(ms + RMS_EPS) * g


def _rope_tab_kernel(pos_ref, freq_ref, tabT_ref, tab_ref):
    pos = pos_ref[...].astype(F32)
    ang = freq_ref[...] * pos
    c = jnp.cos(ang)
    s = jnp.sin(ang)
    r = lax.broadcasted_iota(jnp.int32, ang.shape, 0)
    t = jnp.where(r < 2 * 32, c, jnp.where(r < 3 * 32, -s, s))
    tabT_ref[...] = t
    tab_ref[...] = t.T


def _rope_tables(positions):
    n = positions.size
    half = MLA_ROPE // 2
    inv_freq = ROPE_THETA ** (-jnp.arange(half, dtype=F32) / half)
    freq_col = jnp.tile(inv_freq, LANES // half).reshape(LANES, 1)
    pos_row = positions.reshape(1, n)
    return pl.pallas_call(
        _rope_tab_kernel,
        grid=(n // ROPE_TAB_R,),
        in_specs=[pl.BlockSpec((1, ROPE_TAB_R), lambda i: (0, i)),
                  pl.BlockSpec((LANES, 1), lambda i: (0, 0))],
        out_specs=[pl.BlockSpec((LANES, ROPE_TAB_R), lambda i: (0, i)),
                   pl.BlockSpec((ROPE_TAB_R, LANES), lambda i: (i, 0))],
        out_shape=[jax.ShapeDtypeStruct((LANES, n), F32),
                   jax.ShapeDtypeStruct((n, LANES), F32)],
        compiler_params=_params("arbitrary"),
        name="rope_tables",
    )(pos_row, freq_col)


def _gla_in_kernel(x_ref, wqkz_ref, wvT_ref, wa_ref, q_ref, k_ref, z_ref, vT_ref, a_ref):
    xb = x_ref[...].astype(BF16)
    q_ref[...] = _dot(xb, wqkz_ref[:, :GLA_HK]).astype(BF16)
    k_ref[...] = _dot(xb, wqkz_ref[:, GLA_HK:2 * GLA_HK])
    z_ref[...] = _dot(xb, wqkz_ref[:, 2 * GLA_HK:]).astype(BF16)
    vT_ref[...] = _dot_nt(wvT_ref[...], xb).astype(BF16)
    a_ref[...] = _dot(xb, wa_ref[...])


def _gla_in_proj(x2, w_in):
    n = x2.shape[0]
    bm = PROJ_BM
    wq = w_in[:, :GLA_HK]
    wk = w_in[:, GLA_HK:2 * GLA_HK]
    wv = w_in[:, 2 * GLA_HK:2 * GLA_HK + GLA_HV]
    wz = w_in[:, 2 * GLA_HK + GLA_HV:2 * GLA_HK + 2 * GLA_HV]
    wa = w_in[:, 2 * GLA_HK + 2 * GLA_HV:]
    wqkz = jnp.concatenate([wq, wk, wz], axis=1).astype(BF16)
    wvT = wv.T.astype(BF16)
    wa_p = jnp.pad(wa, ((0, 0), (0, LANES - GLA_GATE_RANK))).astype(BF16)
    nqkz = wqkz.shape[1]
    return pl.pallas_call(
        _gla_in_kernel,
        grid=(n // bm,),
        in_specs=[pl.BlockSpec((bm, D_MODEL), lambda i: (i, 0)),
                  _resident((D_MODEL, nqkz)),
                  _resident((GLA_HV, D_MODEL)),
                  _resident((D_MODEL, LANES))],
        out_specs=[pl.BlockSpec((bm, GLA_HK), lambda i: (i, 0)),
                   pl.BlockSpec((bm, GLA_HK), lambda i: (i, 0)),
                   pl.BlockSpec((bm, GLA_HV), lambda i: (i, 0)),
                   pl.BlockSpec((GLA_HV, bm), lambda i: (0, i)),
                   pl.BlockSpec((bm, LANES), lambda i: (i, 0))],
        out_shape=[jax.ShapeDtypeStruct((n, GLA_HK), BF16),
                   jax.ShapeDtypeStruct((n, GLA_HK), F32),
                   jax.ShapeDtypeStruct((n, GLA_HV), BF16),
                   jax.ShapeDtypeStruct((GLA_HV, n), BF16),
                   jax.ShapeDtypeStruct((n, LANES), F32)],
        compiler_params=_params("arbitrary"),
        name="gla_in_proj",
    )(x2, wqkz, wvT, wa_p)


def _chunk_cumsum(la):
    t, n = la.shape
    x = la.reshape(t // SUBLANES, SUBLANES, n)
    row = lax.broadcasted_iota(jnp.int32, x.shape, 1)
    s = 1
    while s < SUBLANES:
        x = x + jnp.where(row >= s, pltpu.roll(x, s, axis=1), 0.0)
        s *= 2
    tiles_per_chunk = CHUNK // SUBLANES
    out = []
    carry = None
    for i in range(t // SUBLANES):
        tile = x[i]
        if i % tiles_per_chunk != 0:
            tile = tile + carry
        carry = tile[SUBLANES - 1:SUBLANES, :]
        out.append(tile)
    return jnp.concatenate(out, axis=0)


def _gla_rec_kernel(q_ref, k_ref, vT_ref, z_ref, a_ref, wa2_ref, ba_ref, g_ref,
                    o_ref, st_ref):
    @pl.when(pl.program_id(1) == 0)
    def _():
        st_ref[...] = jnp.zeros_like(st_ref)

    pre = _dot(a_ref[...].astype(BF16), wa2_ref[...]) + ba_ref[...]
    la = (jnp.minimum(pre, 0.0) - jnp.log1p(jnp.exp(-jnp.abs(pre)))) * (1.0 / GLA_TAU)
    cum = _chunk_cumsum(la)
    qscale = GLA_DK ** -0.5
    pair = 2 * CHUNK
    for c in range(GLA_TC // CHUNK):
        r0 = c * CHUNK
        cum_c = cum[r0:r0 + CHUNK, :]
        tot = cum_c[CHUNK - 1:CHUNK, :]
        kd = (k_ref[r0:r0 + CHUNK, :] * jnp.exp(tot - cum_c)).astype(BF16)
        dec = jnp.exp(tot)
        zeros = jnp.zeros_like(kd)
        kd_pair = jnp.concatenate([kd, zeros] if c % 2 == 0 else [zeros, kd], axis=0)
        p0 = (c // 2) * pair
        q_c = q_ref[r0:r0 + CHUNK, :]
        for h in range(GLA_HEADS):
            ks = slice(h * GLA_DK, (h + 1) * GLA_DK)
            vs = slice(h * GLA_DV, (h + 1) * GLA_DV)
            kv = _dot(vT_ref[vs, p0:p0 + pair], kd_pair[:, ks])
            s_new = st_ref[h] * dec[:, ks] + kv
            st_ref[h] = s_new
            o = _dot_nt(q_c[:, ks], s_new.astype(BF16)) * qscale
            o = _rms_rows(o, g_ref[:, vs])
            o = o * _silu(z_ref[r0:r0 + CHUNK, vs].astype(F32))
            o_ref[r0:r0 + CHUNK, vs] = o.astype(BF16)


def _gla_recurrence(q, k, vT, z, a_lr, w_a2, b_a, g_out, batch, seq):
    n = q.shape[0]
    tc = GLA_TC
    steps = seq // tc
    wa2_p = jnp.pad(w_a2, ((0, LANES - GLA_GATE_RANK), (0, 0))).astype(BF16)
    row = lambda b, t: (b * steps + t, 0)
    return pl.pallas_call(
        _gla_rec_kernel,
        grid=(batch, steps),
        in_specs=[pl.BlockSpec((tc, GLA_HK), row),
                  pl.BlockSpec((tc, GLA_HK), row),
                  pl.BlockSpec((GLA_HV, tc), lambda b, t: (0, b * steps + t)),
                  pl.BlockSpec((tc, GLA_HV), row),
                  pl.BlockSpec((tc, LANES), row),
                  _resident((LANES, GLA_HK)),
                  _resident((1, GLA_HK)),
                  _resident((1, GLA_HV))],
        out_specs=pl.BlockSpec((tc, GLA_HV), row),
        out_shape=jax.ShapeDtypeStruct((n, GLA_HV), BF16),
        scratch_shapes=[pltpu.VMEM((GLA_HEADS, GLA_DV, GLA_DK), F32)],
        compiler_params=_params("arbitrary", "arbitrary"),
        name="gla_recurrence",
    )(q, k, vT, z, a_lr, wa2_p, b_a.reshape(1, GLA_HK), g_out.reshape(1, GLA_HV))


def _out_ln_kernel(o_ref, x_ref, w_ref, g_ref, b_ref, y_ref):
    r = DEEPNORM_ALPHA * x_ref[...] + _dot(o_ref[...], w_ref[...])
    mu = jnp.mean(r, axis=-1, keepdims=True)
    rc = r - mu
    var = jnp.mean(rc * rc, axis=-1, keepdims=True)
    y_ref[...] = rc * lax.rsqrt(var + LN_EPS) * g_ref[...] + b_ref[...]


def _out_proj_ln(o, x2, w_out, g, b):
    n = x2.shape[0]
    bm = OUT_BM
    return pl.pallas_call(
        _out_ln_kernel,
        grid=(n // bm,),
        in_specs=[pl.BlockSpec((bm, o.shape[1]), lambda i: (i, 0)),
                  pl.BlockSpec((bm, D_MODEL), lambda i: (i, 0)),
                  _resident((o.shape[1], D_MODEL)),
                  _resident((1, D_MODEL)),
                  _resident((1, D_MODEL))],
        out_specs=pl.BlockSpec((bm, D_MODEL), lambda i: (i, 0)),
        out_shape=jax.ShapeDtypeStruct((n, D_MODEL), F32),
        compiler_params=_params("arbitrary"),
        name="out_proj_ln",
    )(o, x2, w_out.astype(BF16), g.reshape(1, D_MODEL), b.reshape(1, D_MODEL))


def _mla_in_kernel(x_ref, wc_ref, wz_ref, wr_ref, gq_ref, gkv_ref, tab_ref,
                   cq_ref, ckv_ref, z_ref, kr_ref):
    xb = x_ref[...].astype(BF16)
    c = _dot(xb, wc_ref[...])
    cq_ref[...] = _rms_rows(c[:, :MLA_Q_RANK], gq_ref[...]).astype(BF16)
    ckv_ref[...] = _rms_rows(c[:, MLA_Q_RANK:], gkv_ref[...]).astype(BF16)
    z_ref[...] = _dot(xb, wz_ref[...]).astype(BF16)
    a = _dot(xb, wr_ref[...]) * tab_ref[...]
    kr_ref[...] = (a + pltpu.roll(a, LANES // 2, axis=1)).astype(BF16)


def _mla_in_proj(x2, w_in, g_q, g_kv, tab):
    n = x2.shape[0]
    bm = PROJ_BM
    half = MLA_ROPE // 2
    nc = MLA_Q_RANK + MLA_KV_RANK
    wc = w_in[:, :nc].astype(BF16)
    wr = w_in[:, nc:nc + MLA_ROPE]
    wz = w_in[:, nc + MLA_ROPE:].astype(BF16)
    t1, t2 = wr[:, :half], wr[:, half:]
    wr4 = jnp.concatenate([t1, t2, t2, t1], axis=1).astype(BF16)
    row = lambda i: (i, 0)
    return pl.pallas_call(
        _mla_in_kernel,
        grid=(n // bm,),
        in_specs=[pl.BlockSpec((bm, D_MODEL), row),
                  _resident((D_MODEL, nc)),
                  _resident((D_MODEL, MLA_HV)),
                  _resident((D_MODEL, LANES)),
                  _resident((1, MLA_Q_RANK)),
                  _resident((1, MLA_KV_RANK)),
                  pl.BlockSpec((bm, LANES), row)],
        out_specs=[pl.BlockSpec((bm, MLA_Q_RANK), row),
                   pl.BlockSpec((bm, MLA_KV_RANK), row),
                   pl.BlockSpec((bm, MLA_HV), row),
                   pl.BlockSpec((bm, LANES), row)],
        out_shape=[jax.ShapeDtypeStruct((n, MLA_Q_RANK), BF16),
                   jax.ShapeDtypeStruct((n, MLA_KV_RANK), BF16),
                   jax.ShapeDtypeStruct((n, MLA_HV), BF16),
                   jax.ShapeDtypeStruct((n, LANES), BF16)],
        compiler_params=_params("arbitrary"),
        name="mla_in_proj",
    )(x2, wc, wz, wr4, g_q.reshape(1, -1), g_kv.reshape(1, -1), tab)


def _mla_up_kernel(cq_ref, ckv_ref, wqT_ref, wk_ref, wvT_ref, tabT_ref,
                   qT_ref, k_ref, vT_ref):
    cq = cq_ref[...]
    ckv = ckv_ref[...]
    scale = MLA_QK ** -0.5 * math.log2(math.e)
    tab = tabT_ref[...] * scale
    group = 4 * QHEAD
    for g0 in range(0, MLA_HEADS * QHEAD, group):
        acc = _dot_nt(wqT_ref[g0:g0 + group, :], cq)
        for h0 in range(0, group, QHEAD):
            r0 = g0 + h0
            qT_ref[r0:r0 + LANES, :] = (acc[h0:h0 + LANES] * scale).astype(BF16)
            qT_ref[r0 + LANES:r0 + QHEAD, :] = (acc[h0 + LANES:h0 + QHEAD] * tab).astype(BF16)
    k_ref[...] = _dot(ckv, wk_ref[...]).astype(BF16)
    vT_ref[...] = _dot_nt(wvT_ref[...], ckv).astype(BF16)


def _mla_up_proj(cq, ckv, w_uq, w_ukv, tabT):
    n = cq.shape[0]
    bm = OUT_BM
    half = MLA_ROPE // 2
    wq = w_uq.reshape(MLA_Q_RANK, MLA_HEADS, MLA_QK)
    nope = wq[:, :, :MLA_NOPE]
    t1 = wq[:, :, MLA_NOPE:MLA_NOPE + half]
    t2 = wq[:, :, MLA_NOPE + half:]
    wq4 = jnp.concatenate([nope, t1, t2, t2, t1], axis=2)
    wqT = wq4.reshape(MLA_Q_RANK, MLA_HEADS * QHEAD).T.astype(BF16)
    wkv = w_ukv.reshape(MLA_KV_RANK, MLA_HEADS, MLA_NOPE + MLA_V)
    wk = wkv[:, :, :MLA_NOPE].reshape(MLA_KV_RANK, MLA_HEADS * MLA_NOPE).astype(BF16)
    wvT = wkv[:, :, MLA_NOPE:].reshape(MLA_KV_RANK, MLA_HV).T.astype(BF16)
    row = lambda i: (i, 0)
    col = lambda i: (0, i)
    return pl.pallas_call(
        _mla_up_kernel,
        grid=(n // bm,),
        in_specs=[pl.BlockSpec((bm, MLA_Q_RANK), row),
                  pl.BlockSpec((bm, MLA_KV_RANK), row),
                  _resident((MLA_HEADS * QHEAD, MLA_Q_RANK)),
                  _resident((MLA_KV_RANK, MLA_HEADS * MLA_NOPE)),
                  _resident((MLA_HV, MLA_KV_RANK)),
                  pl.BlockSpec((LANES, bm), col)],
        out_specs=[pl.BlockSpec((MLA_HEADS * QHEAD, bm), col),
                   pl.BlockSpec((bm, MLA_HEADS * MLA_NOPE), row),
                   pl.BlockSpec((MLA_HV, bm), col)],
        out_shape=[jax.ShapeDtypeStruct((MLA_HEADS * QHEAD, n), BF16),
                   jax.ShapeDtypeStruct((n, MLA_HEADS * MLA_NOPE), BF16),
                   jax.ShapeDtypeStruct((MLA_HV, n), BF16)],
        compiler_params=_params("arbitrary"),
        name="mla_up_proj",
    )(cq, ckv, wqT, wk, wvT, tabT)


def _mla_attn_kernel(qT_ref, k_ref, kr_ref, vT_ref, z_ref, o_ref, s_ref):
    t = ATT_T
    nt = qT_ref.shape[1] // t
    kk = lax.broadcasted_iota(jnp.int32, (t, t), 0) // CHUNK
    qq = lax.broadcasted_iota(jnp.int32, (t, t), 1) // CHUNK
    visible = kk <= qq
    ones = jnp.ones((BF16_ROWS, t), BF16)
    blocks = [(i, j) for i in range(nt) for j in range(i + 1)]

    def head_steps(h, slot):
        m = [None] * nt
        acc = [None] * nt

        def score(idx, i, j):
            ks = slice(j * t, (j + 1) * t)
            kcat = jnp.concatenate([k_ref[ks, h * MLA_NOPE:(h + 1) * MLA_NOPE],
                                    kr_ref[ks, :]], axis=1)
            s = _dot(kcat, qT_ref[h * QHEAD:(h + 1) * QHEAD, i * t:(i + 1) * t])
            if j == i:
                s = jnp.where(visible, s, -jnp.inf)
            m_blk = jnp.max(s, axis=0, keepdims=True)
            m[i] = m_blk if m[i] is None else jnp.maximum(m[i], m_blk)
            s_ref[slot, idx] = s

        def value(idx, i, j):
            p = jnp.exp2(s_ref[slot, idx] - m[i]).astype(BF16)
            v_aug = jnp.concatenate([vT_ref[h * MLA_V:(h + 1) * MLA_V, j * t:(j + 1) * t],
                                     ones], axis=0)
            pv = _dot(v_aug, p)
            acc[i] = pv if acc[i] is None else acc[i] + pv
            if j == i:
                a = acc[i]
                o = (a[:MLA_V] * (1.0 / a[MLA_V:MLA_V + 1])).T
                rows = slice(i * t, (i + 1) * t)
                cols = slice(h * MLA_V, (h + 1) * MLA_V)
                o_ref[rows, cols] = (o * _silu(z_ref[rows, cols].astype(F32))).astype(BF16)

        return ([functools.partial(score, idx, i, j) for idx, (i, j) in enumerate(blocks)],
                [functools.partial(value, idx, i, j) for idx, (i, j) in enumerate(blocks)])

    steps = [head_steps(h, h % 2) for h in range(ATT_HB)]
    for f in steps[0][0]:
        f()
    for h in range(ATT_HB):
        nxt = steps[h + 1][0] if h + 1 < ATT_HB else []
        for n, f in enumerate(steps[h][1]):
            if n < len(nxt):
                nxt[n]()
            f()


def _mla_attention(qT, k, kr, vT, z, batch, seq):
    n = k.shape[0]
    hb = ATT_HB
    nt = seq // ATT_T
    n_blocks = nt * (nt + 1) // 2
    return pl.pallas_call(
        _mla_attn_kernel,
        grid=(batch, MLA_HEADS // hb),
        in_specs=[pl.BlockSpec((hb * QHEAD, seq), lambda b, h: (h, b)),
                  pl.BlockSpec((seq, hb * MLA_NOPE), lambda b, h: (b, h)),
                  pl.BlockSpec((seq, LANES), lambda b, h: (b, 0)),
                  pl.BlockSpec((hb * MLA_V, seq), lambda b, h: (h, b)),
                  pl.BlockSpec((seq, hb * MLA_V), lambda b, h: (b, h))],
        out_specs=pl.BlockSpec((seq, hb * MLA_V), lambda b, h: (b, h)),
        out_shape=jax.ShapeDtypeStruct((n, MLA_HV), BF16),
        scratch_shapes=[pltpu.VMEM((2, n_blocks, ATT_T, ATT_T), F32)],
        compiler_params=_params("arbitrary", "arbitrary"),
        name="mla_attention",
    )(qT, k, kr, vT, z)


def kernel(x, positions, gla_w_in, gla_w_a2, gla_b_a, gla_g_out, gla_w_out,
           mla_w_in, mla_g_q, mla_w_uq, mla_g_kv, mla_w_ukv, mla_w_out, ln_g, ln_b):
    batch, seq, d = x.shape
    assert d == D_MODEL and seq % GLA_TC == 0 and seq % ATT_T == 0
    x2 = x.reshape(batch * seq, d)

    q, k, z, vT, a_lr = _gla_in_proj(x2, gla_w_in[0])
    o = _gla_recurrence(q, k, vT, z, a_lr, gla_w_a2[0], gla_b_a[0], gla_g_out[0], batch, seq)
    x2 = _out_proj_ln(o, x2, gla_w_out[0], ln_g[0], ln_b[0])

    tabT, tab = _rope_tables(positions)
    cq, ckv, z, kr = _mla_in_proj(x2, mla_w_in[0], mla_g_q[0], mla_g_kv[0], tab)
    qT, kn, vT = _mla_up_proj(cq, ckv, mla_w_uq[0], mla_w_ukv[0], tabT)
    o = _mla_attention(qT, kn, kr, vT, z, batch, seq)
    x2 = _out_proj_ln(o, x2, mla_w_out[0], ln_g[1], ln_b[1])
    return x2.reshape(batch, seq, d)
```

```python
import functools
import math

import jax
import jax.numpy as jnp
from jax import lax
from jax.experimental import pallas as pl
from jax.experimental.pallas import tpu as pltpu

BF16 = jnp.bfloat16
F32 = jnp.float32

D_MODEL = 2048
DEPTH = 2
CHUNK = 64
DEEPNORM_ALPHA = float((2 * DEPTH) ** 0.25)
LN_EPS = 1e-5
RMS_EPS = 1e-6

GLA_HEADS = 4
GLA_DK = 256
GLA_DV = 512
GLA_HK = GLA_HEADS * GLA_DK
GLA_HV = GLA_HEADS * GLA_DV
GLA_GATE_RANK = 16
GLA_TAU = 16.0

MLA_HEADS = 16
MLA_Q_RANK = 512
MLA_KV_RANK = 512
MLA_NOPE = 128
MLA_ROPE = 64
MLA_V = 128
MLA_QK = MLA_NOPE + MLA_ROPE
MLA_HV = MLA_HEADS * MLA_V
ROPE_THETA = 10000.0

LANES = 128
SUBLANES = 8
BF16_ROWS = 16
MXU_DIM = 256
VMEM_LIMIT = 56 * 1024 * 1024

PROJ_BM = 256
OUT_BM = 512
GLA_TC = 512
ATT_T = 256
ATT_HB = 4
ROPE_TAB_R = 2048
QHEAD = 2 * LANES


def _params(*sem):
    return pltpu.CompilerParams(dimension_semantics=sem, vmem_limit_bytes=VMEM_LIMIT)


def _resident(shape):
    zeros = (0,) * len(shape)
    return pl.BlockSpec(shape, lambda *_: zeros, pipeline_mode=pl.Buffered(1))


def _dot(a, b):
    return jnp.dot(a, b, preferred_element_type=F32)


def _dot_nt(a, b):
    return lax.dot_general(a, b, (((1,), (1,)), ((), ())), preferred_element_type=F32)


def _silu(z):
    return z * (1.0 / (1.0 + jnp.exp(-z)))


def _rms_rows(x, g):
    ms = jnp.mean(x * x, axis=-1, keepdims=True)
    return x * lax.rsqrt(ms + RMS_EPS) * g


def _rope_tab_kernel(pos_ref, freq_ref, tabT_ref, tab_ref):
    pos = pos_ref[...].astype(F32)
    ang = freq_ref[...] * pos
    c = jnp.cos(ang)
    s = jnp.sin(ang)
    r = lax.broadcasted_iota(jnp.int32, ang.shape, 0)
    t = jnp.where(r < 2 * 32, c, jnp.where(r < 3 * 32, -s, s))
    tabT_ref[...] = t
    tab_ref[...] = t.T


def _rope_tables(positions):
    n = positions.size
    half = MLA_ROPE // 2
    inv_freq = ROPE_THETA ** (-jnp.arange(half, dtype=F32) / half)
    freq_col = jnp.tile(inv_freq, LANES // half).reshape(LANES, 1)
    pos_row = positions.reshape(1, n)
    return pl.pallas_call(
        _rope_tab_kernel,
        grid=(n // ROPE_TAB_R,),
        in_specs=[pl.BlockSpec((1, ROPE_TAB_R), lambda i: (0, i)),
                  pl.BlockSpec((LANES, 1), lambda i: (0, 0))],
        out_specs=[pl.BlockSpec((LANES, ROPE_TAB_R), lambda i: (0, i)),
                   pl.BlockSpec((ROPE_TAB_R, LANES), lambda i: (i, 0))],
        out_shape=[jax.ShapeDtypeStruct((LANES, n), F32),
                   jax.ShapeDtypeStruct((n, LANES), F32)],
        compiler_params=_params("arbitrary"),
        name="rope_tables",
    )(pos_row, freq_col)


def _chunk_cumsum(la):
    t, n = la.shape
    x = la.reshape(t // SUBLANES, SUBLANES, n)
    row = lax.broadcasted_iota(jnp.int32, x.shape, 1)
    s = 1
    while s < SUBLANES:
        x = x + jnp.where(row >= s, pltpu.roll(x, s, axis=1), 0.0)
        s *= 2
    tiles_per_chunk = CHUNK // SUBLANES
    out = []
    carry = None
    for i in range(t // SUBLANES):
        tile = x[i]
        if i % tiles_per_chunk != 0:
            tile = tile + carry
        carry = tile[SUBLANES - 1:SUBLANES, :]
        out.append(tile)
    return jnp.concatenate(out, axis=0)


def _gla_in_kernel(x_ref, wqkz_ref, wvT_ref, wa_ref, wa2_ref, ba_ref, g_ref,
                   q_ref, kdm_ref, sz_ref, vT_ref, dec_ref):
    bm = x_ref.shape[0]
    xb = x_ref[...].astype(BF16)
    q_ref[...] = _dot(xb, wqkz_ref[:, :GLA_HK]).astype(BF16)
    z = _dot(xb, wqkz_ref[:, 2 * GLA_HK:])
    sz_ref[...] = (_silu(z) * g_ref[...]).astype(BF16)
    vT_ref[...] = _dot_nt(wvT_ref[...], xb).astype(BF16)

    a = _dot(xb, wa_ref[...]).astype(BF16)
    pre = _dot(a, wa2_ref[...]) + ba_ref[...]
    la = (jnp.minimum(pre, 0.0) - jnp.log(1.0 + jnp.exp(-jnp.abs(pre)))) * (1.0 / GLA_TAU)
    cum = _chunk_cumsum(la)
    chunks = bm // CHUNK
    tots = [cum[(c + 1) * CHUNK - 1:(c + 1) * CHUNK, :] for c in range(chunks)]
    dec_ref[0] = jnp.exp(jnp.concatenate(tots, axis=0))
    tot_rows = jnp.concatenate([jnp.broadcast_to(t, (CHUNK, GLA_HK)) for t in tots], axis=0)
    kd = (_dot(xb, wqkz_ref[:, GLA_HK:2 * GLA_HK]) * jnp.exp(tot_rows - cum)).astype(BF16)

    zeros = jnp.zeros((CHUNK, GLA_DK), BF16)
    for c in range(chunks):
        rows = slice(c * CHUNK, (c + 1) * CHUNK)
        for h in range(GLA_HEADS):
            c0 = h * 2 * GLA_DK
            mine = slice(c0 + (c % 2) * GLA_DK, c0 + (c % 2 + 1) * GLA_DK)
            other = slice(c0 + (1 - c % 2) * GLA_DK, c0 + (2 - c % 2) * GLA_DK)
            kdm_ref[rows, mine] = kd[rows, h * GLA_DK:(h + 1) * GLA_DK]
            kdm_ref[rows, other] = zeros


def _gla_in_proj(x2, w_in, w_a2, b_a, g_out):
    n = x2.shape[0]
    bm = PROJ_BM
    wq = w_in[:, :GLA_HK]
    wk = w_in[:, GLA_HK:2 * GLA_HK]
    wv = w_in[:, 2 * GLA_HK:2 * GLA_HK + GLA_HV]
    wz = w_in[:, 2 * GLA_HK + GLA_HV:2 * GLA_HK + 2 * GLA_HV]
    wa = w_in[:, 2 * GLA_HK + 2 * GLA_HV:]
    wqkz = jnp.concatenate([wq, wk, wz], axis=1).astype(BF16)
    wvT = wv.T.astype(BF16)
    wa_p = jnp.pad(wa, ((0, 0), (0, LANES - GLA_GATE_RANK))).astype(BF16)
    wa2_p = jnp.pad(w_a2, ((0, LANES - GLA_GATE_RANK), (0, 0))).astype(BF16)
    nqkz = wqkz.shape[1]
    row = lambda i: (i, 0)
    return pl.pallas_call(
        _gla_in_kernel,
        grid=(n // bm,),
        in_specs=[pl.BlockSpec((bm, D_MODEL), row),
                  _resident((D_MODEL, nqkz)),
                  _resident((GLA_HV, D_MODEL)),
                  _resident((D_MODEL, LANES)),
                  _resident((LANES, GLA_HK)),
                  _resident((1, GLA_HK)),
                  _resident((1, GLA_HV))],
        out_specs=[pl.BlockSpec((bm, GLA_HK), row),
                   pl.BlockSpec((bm, 2 * GLA_HK), row),
                   pl.BlockSpec((bm, GLA_HV), row),
                   pl.BlockSpec((GLA_HV, bm), lambda i: (0, i)),
                   pl.BlockSpec((1, bm // CHUNK, GLA_HK), lambda i: (i, 0, 0))],
        out_shape=[jax.ShapeDtypeStruct((n, GLA_HK), BF16),
                   jax.ShapeDtypeStruct((n, 2 * GLA_HK), BF16),
                   jax.ShapeDtypeStruct((n, GLA_HV), BF16),
                   jax.ShapeDtypeStruct((GLA_HV, n), BF16),
                   jax.ShapeDtypeStruct((n // bm, bm // CHUNK, GLA_HK), F32)],
        compiler_params=_params("arbitrary"),
        name="gla_in_proj",
    )(x2, wqkz, wvT, wa_p, wa2_p, b_a.reshape(1, GLA_HK), g_out.reshape(1, GLA_HV))


def _gla_rec_kernel(q_ref, kdm_ref, vT_ref, sz_ref, dec_ref, o_ref, st_ref):
    @pl.when(pl.program_id(1) == 0)
    def _():
        st_ref[...] = jnp.zeros_like(st_ref)

    pair = 2 * CHUNK
    chunks_per_block = dec_ref.shape[1]
    eps = RMS_EPS * GLA_DK
    units = [(p, h) for p in range(GLA_TC // pair) for h in range(GLA_HEADS)]

    def pair_update(p, h):
        return _dot(vT_ref[h * GLA_DV:(h + 1) * GLA_DV, p * pair:(p + 1) * pair],
                    kdm_ref[p * pair:(p + 1) * pair, h * 2 * GLA_DK:(h + 1) * 2 * GLA_DK])

    kv_next = pair_update(*units[0])
    for n, (p, h) in enumerate(units):
        kv2 = kv_next
        if n + 1 < len(units):
            kv_next = pair_update(*units[n + 1])
        ks = slice(h * GLA_DK, (h + 1) * GLA_DK)
        vs = slice(h * GLA_DV, (h + 1) * GLA_DV)
        for e in range(2):
            c = 2 * p + e
            rows = slice(c * CHUNK, (c + 1) * CHUNK)
            dec = dec_ref[c // chunks_per_block, c % chunks_per_block:c % chunks_per_block + 1, ks]
            s_new = st_ref[h] * dec + kv2[:, e * GLA_DK:(e + 1) * GLA_DK]
            st_ref[h] = s_new
            o = _dot_nt(q_ref[rows, ks], s_new.astype(BF16))
            ms = jnp.mean(o * o, axis=-1, keepdims=True)
            o_ref[rows, vs] = (o * lax.rsqrt(ms + eps) * sz_ref[rows, vs]).astype(BF16)


def _gla_recurrence(q, kdm, vT, sz, dec, batch, seq):
    n = q.shape[0]
    tc = GLA_TC
    steps = seq // tc
    blocks = tc // PROJ_BM
    row = lambda b, t: (b * steps + t, 0)
    return pl.pallas_call(
        _gla_rec_kernel,
        grid=(batch, steps),
        in_specs=[pl.BlockSpec((tc, GLA_HK), row),
                  pl.BlockSpec((tc, 2 * GLA_HK), row),
                  pl.BlockSpec((GLA_HV, tc), lambda b, t: (0, b * steps + t)),
                  pl.BlockSpec((tc, GLA_HV), row),
                  pl.BlockSpec((blocks, dec.shape[1], GLA_HK), lambda b, t: (b * steps + t, 0, 0))],
        out_specs=pl.BlockSpec((tc, GLA_HV), row),
        out_shape=jax.ShapeDtypeStruct((n, GLA_HV), BF16),
        scratch_shapes=[pltpu.VMEM((GLA_HEADS, GLA_DV, GLA_DK), F32)],
        compiler_params=_params("arbitrary", "arbitrary"),
        name="gla_recurrence",
    )(q, kdm, vT, sz, dec)


def _out_ln_kernel(o_ref, x_ref, w_ref, g_ref, b_ref, y_ref):
    r = DEEPNORM_ALPHA * x_ref[...] + _dot(o_ref[...], w_ref[...])
    mu = jnp.mean(r, axis=-1, keepdims=True)
    rc = r - mu
    var = jnp.mean(rc * rc, axis=-1, keepdims=True)
    y_ref[...] = rc * lax.rsqrt(var + LN_EPS) * g_ref[...] + b_ref[...]


def _out_proj_ln(o, x2, w_out, g, b):
    n = x2.shape[0]
    bm = OUT_BM
    return pl.pallas_call(
        _out_ln_kernel,
        grid=(n // bm,),
        in_specs=[pl.BlockSpec((bm, o.shape[1]), lambda i: (i, 0)),
                  pl.BlockSpec((bm, D_MODEL), lambda i: (i, 0)),
                  _resident((o.shape[1], D_MODEL)),
                  _resident((1, D_MODEL)),
                  _resident((1, D_MODEL))],
        out_specs=pl.BlockSpec((bm, D_MODEL), lambda i: (i, 0)),
        out_shape=jax.ShapeDtypeStruct((n, D_MODEL), F32),
        compiler_params=_params("arbitrary"),
        name="out_proj_ln",
    )(o, x2, w_out.astype(BF16), g.reshape(1, D_MODEL), b.reshape(1, D_MODEL))


def _mla_in_kernel(x_ref, wc_ref, wz_ref, wr_ref, gq_ref, gkv_ref, tab_ref,
                   cq_ref, ckv_ref, z_ref, kr_ref):
    xb = x_ref[...].astype(BF16)
    c = _dot(xb, wc_ref[...])
    cq_ref[...] = _rms_rows(c[:, :MLA_Q_RANK], gq_ref[...]).astype(BF16)
    ckv_ref[...] = _rms_rows(c[:, MLA_Q_RANK:], gkv_ref[...]).astype(BF16)
    z_ref[...] = _silu(_dot(xb, wz_ref[...])).astype(BF16)
    a = _dot(xb, wr_ref[...]) * tab_ref[...]
    kr_ref[...] = (a + pltpu.roll(a, LANES // 2, axis=1)).astype(BF16)


def _mla_in_proj(x2, w_in, g_q, g_kv, tab):
    n = x2.shape[0]
    bm = PROJ_BM
    half = MLA_ROPE // 2
    nc = MLA_Q_RANK + MLA_KV_RANK
    wc = w_in[:, :nc].astype(BF16)
    wr = w_in[:, nc:nc + MLA_ROPE]
    wz = w_in[:, nc + MLA_ROPE:].astype(BF16)
    t1, t2 = wr[:, :half], wr[:, half:]
    wr4 = jnp.concatenate([t1, t2, t2, t1], axis=1).astype(BF16)
    row = lambda i: (i, 0)
    return pl.pallas_call(
        _mla_in_kernel,
        grid=(n // bm,),
        in_specs=[pl.BlockSpec((bm, D_MODEL), row),
                  _resident((D_MODEL, nc)),
                  _resident((D_MODEL, MLA_HV)),
                  _resident((D_MODEL, LANES)),
                  _resident((1, MLA_Q_RANK)),
                  _resident((1, MLA_KV_RANK)),
                  pl.BlockSpec((bm, LANES), row)],
        out_specs=[pl.BlockSpec((bm, MLA_Q_RANK), row),
                   pl.BlockSpec((bm, MLA_KV_RANK), row),
                   pl.BlockSpec((bm, MLA_HV), row),
                   pl.BlockSpec((bm, LANES), row)],
        out_shape=[jax.ShapeDtypeStruct((n, MLA_Q_RANK), BF16),
                   jax.ShapeDtypeStruct((n, MLA_KV_RANK), BF16),
                   jax.ShapeDtypeStruct((n, MLA_HV), BF16),
                   jax.ShapeDtypeStruct((n, LANES), BF16)],
        compiler_params=_params("arbitrary"),
        name="mla_in_proj",
    )(x2, wc, wz, wr4, g_q.reshape(1, -1), g_kv.reshape(1, -1), tab)


def _mla_up_kernel(cq_ref, ckv_ref, wqT_ref, wk_ref, wvT_ref, tabT_ref,
                   qT_ref, k_ref, vT_ref):
    cq = cq_ref[...]
    ckv = ckv_ref[...]
    scale = MLA_QK ** -0.5 * math.log2(math.e)
    tab = tabT_ref[...] * scale
    group = 4 * QHEAD
    for g0 in range(0, MLA_HEADS * QHEAD, group):
        acc = _dot_nt(wqT_ref[g0:g0 + group, :], cq)
        for h0 in range(0, group, QHEAD):
            r0 = g0 + h0
            qT_ref[r0:r0 + LANES, :] = (acc[h0:h0 + LANES] * scale).astype(BF16)
            qT_ref[r0 + LANES:r0 + QHEAD, :] = (acc[h0 + LANES:h0 + QHEAD] * tab).astype(BF16)
    k_ref[...] = _dot(ckv, wk_ref[...]).astype(BF16)
    vT_ref[...] = _dot_nt(wvT_ref[...], ckv).astype(BF16)


def _mla_up_proj(cq, ckv, w_uq, w_ukv, tabT):
    n = cq.shape[0]
    bm = OUT_BM
    half = MLA_ROPE // 2
    wq = w_uq.reshape(MLA_Q_RANK, MLA_HEADS, MLA_QK)
    nope = wq[:, :, :MLA_NOPE]
    t1 = wq[:, :, MLA_NOPE:MLA_NOPE + half]
    t2 = wq[:, :, MLA_NOPE + half:]
    wq4 = jnp.concatenate([nope, t1, t2, t2, t1], axis=2)
    wqT = wq4.reshape(MLA_Q_RANK, MLA_HEADS * QHEAD).T.astype(BF16)
    wkv = w_ukv.reshape(MLA_KV_RANK, MLA_HEADS, MLA_NOPE + MLA_V)
    wk = wkv[:, :, :MLA_NOPE].reshape(MLA_KV_RANK, MLA_HEADS * MLA_NOPE).astype(BF16)
    wvT = wkv[:, :, MLA_NOPE:].reshape(MLA_KV_RANK, MLA_HV).T.astype(BF16)
    row = lambda i: (i, 0)
    col = lambda i: (0, i)
    return pl.pallas_call(
        _mla_up_kernel,
        grid=(n // bm,),
        in_specs=[pl.BlockSpec((bm, MLA_Q_RANK), row),
                  pl.BlockSpec((bm, MLA_KV_RANK), row),
                  _resident((MLA_HEADS * QHEAD, MLA_Q_RANK)),
                  _resident((MLA_KV_RANK, MLA_HEADS * MLA_NOPE)),
                  _resident((MLA_HV, MLA_KV_RANK)),
                  pl.BlockSpec((LANES, bm), col)],
        out_specs=[pl.BlockSpec((MLA_HEADS * QHEAD, bm), col),
                   pl.BlockSpec((bm, MLA_HEADS * MLA_NOPE), row),
                   pl.BlockSpec((MLA_HV, bm), col)],
        out_shape=[jax.ShapeDtypeStruct((MLA_HEADS * QHEAD, n), BF16),
                   jax.ShapeDtypeStruct((n, MLA_HEADS * MLA_NOPE), BF16),
                   jax.ShapeDtypeStruct((MLA_HV, n), BF16)],
        compiler_params=_params("arbitrary"),
        name="mla_up_proj",
    )(cq, ckv, wqT, wk, wvT, tabT)


def _mla_attn_kernel(qT_ref, k_ref, kr_ref, vT_ref, z_ref, o_ref, s_ref):
    t = ATT_T
    nt = qT_ref.shape[1] // t
    kk = lax.broadcasted_iota(jnp.int32, (t, t), 0) // CHUNK
    qq = lax.broadcasted_iota(jnp.int32, (t, t), 1) // CHUNK
    visible = kk <= qq
    ones = jnp.ones((BF16_ROWS, t), BF16)
    blocks = [(i, j) for i in range(nt) for j in range(i + 1)]

    def head_steps(h, slot):
        m = [None] * nt
        acc = [None] * nt

        def score(idx, i, j):
            ks = slice(j * t, (j + 1) * t)
            kcat = jnp.concatenate([k_ref[ks, h * MLA_NOPE:(h + 1) * MLA_NOPE],
                                    kr_ref[ks, :]], axis=1)
            s = _dot(kcat, qT_ref[h * QHEAD:(h + 1) * QHEAD, i * t:(i + 1) * t])
            if j == i:
                s = jnp.where(visible, s, -jnp.inf)
            m_blk = jnp.max(s, axis=0, keepdims=True)
            m[i] = m_blk if m[i] is None else jnp.maximum(m[i], m_blk)
            s_ref[slot, idx] = s

        def value(idx, i, j):
            p = jnp.exp2(s_ref[slot, idx] - m[i]).astype(BF16)
            v_aug = jnp.concatenate([vT_ref[h * MLA_V:(h + 1) * MLA_V, j * t:(j + 1) * t],
                                     ones], axis=0)
            pv = _dot(v_aug, p)
            acc[i] = pv if acc[i] is None else acc[i] + pv
            if j == i:
                a = acc[i]
                o = (a[:MLA_V] * (1.0 / a[MLA_V:MLA_V + 1])).T
                rows = slice(i * t, (i + 1) * t)
                cols = slice(h * MLA_V, (h + 1) * MLA_V)
                o_ref[rows, cols] = (o * z_ref[rows, cols]).astype(BF16)

        return ([functools.partial(score, idx, i, j) for idx, (i, j) in enumerate(blocks)],
                [functools.partial(value, idx, i, j) for idx, (i, j) in enumerate(blocks)])

    steps = [head_steps(h, h % 2) for h in range(ATT_HB)]
    for f in steps[0][0]:
        f()
    for h in range(ATT_HB):
        nxt = steps[h + 1][0] if h + 1 < ATT_HB else []
        for n, f in enumerate(steps[h][1]):
            if n < len(nxt):
                nxt[n]()
            f()


def _mla_attention(qT, k, kr, vT, z, batch, seq):
    n = k.shape[0]
    hb = ATT_HB
    nt = seq // ATT_T
    n_blocks = nt * (nt + 1) // 2
    return pl.pallas_call(
        _mla_attn_kernel,
        grid=(batch, MLA_HEADS // hb),
        in_specs=[pl.BlockSpec((hb * QHEAD, seq), lambda b, h: (h, b)),
                  pl.BlockSpec((seq, hb * MLA_NOPE), lambda b, h: (b, h)),
                  pl.BlockSpec((seq, LANES), lambda b, h: (b, 0)),
                  pl.BlockSpec((hb * MLA_V, seq), lambda b, h: (h, b)),
                  pl.BlockSpec((seq, hb * MLA_V), lambda b, h: (b, h))],
        out_specs=pl.BlockSpec((seq, hb * MLA_V), lambda b, h: (b, h)),
        out_shape=jax.ShapeDtypeStruct((n, MLA_HV), BF16),
        scratch_shapes=[pltpu.VMEM((2, n_blocks, ATT_T, ATT_T), F32)],
        compiler_params=_params("arbitrary", "arbitrary"),
        name="mla_attention",
    )(qT, k, kr, vT, z)


def kernel(x, positions, gla_w_in, gla_w_a2, gla_b_a, gla_g_out, gla_w_out,
           mla_w_in, mla_g_q, mla_w_uq, mla_g_kv, mla_w_ukv, mla_w_out, ln_g, ln_b):
    batch, seq, d = x.shape
    assert d == D_MODEL and seq % GLA_TC == 0 and seq % ATT_T == 0
    x2 = x.reshape(batch * seq, d)

    q, kdm, sz, vT, dec = _gla_in_proj(x2, gla_w_in[0], gla_w_a2[0], gla_b_a[0], gla_g_out[0])
    o = _gla_recurrence(q, kdm, vT, sz, dec, batch, seq)
    x2 = _out_proj_ln(o, x2, gla_w_out[0], ln_g[0], ln_b[0])

    tabT, tab = _rope_tables(positions)
    cq, ckv, z, kr = _mla_in_proj(x2, mla_w_in[0], mla_g_q[0], mla_g_kv[0], tab)
    qT, kn, vT = _mla_up_proj(cq, ckv, mla_w_uq[0], mla_w_ukv[0], tabT)
    o = _mla_attention(qT, kn, kr, vT, z, batch, seq)
    x2 = _out_proj_ln(o, x2, mla_w_out[0], ln_g[1], ln_b[1])
    return x2.reshape(batch, seq, d)
```

```python
import functools
import math

import jax
import jax.numpy as jnp
from jax import lax
from jax.experimental import pallas as pl
from jax.experimental.pallas import tpu as pltpu

BF16 = jnp.bfloat16
F32 = jnp.float32

D_MODEL = 2048
DEPTH = 2
CHUNK = 64
DEEPNORM_ALPHA = float((2 * DEPTH) ** 0.25)
LN_EPS = 1e-5
RMS_EPS = 1e-6

GLA_HEADS = 4
GLA_DK = 256
GLA_DV = 512
GLA_HK = GLA_HEADS * GLA_DK
GLA_HV = GLA_HEADS * GLA_DV
GLA_GATE_RANK = 16
GLA_TAU = 16.0

MLA_HEADS = 16
MLA_Q_RANK = 512
MLA_KV_RANK = 512
MLA_NOPE = 128
MLA_ROPE = 64
MLA_V = 128
MLA_QK = MLA_NOPE + MLA_ROPE
MLA_HV = MLA_HEADS * MLA_V
ROPE_THETA = 10000.0

LANES = 128
SUBLANES = 8
BF16_ROWS = 16
MXU_DIM = 256
VMEM_LIMIT = 56 * 1024 * 1024

PROJ_BM = 256
OUT_BM = 512
OUT_SUB = 128
GLA_TC = 512
ATT_T = 256
ATT_HB = 4
ROPE_TAB_R = 2048
QHEAD = 2 * LANES


def _params(*sem):
    return pltpu.CompilerParams(dimension_semantics=sem, vmem_limit_bytes=VMEM_LIMIT)


def _resident(shape):
    zeros = (0,) * len(shape)
    return pl.BlockSpec(shape, lambda *_: zeros, pipeline_mode=pl.Buffered(1))


def _dot(a, b):
    return jnp.dot(a, b, preferred_element_type=F32)


def _dot_nt(a, b):
    return lax.dot_general(a, b, (((1,), (1,)), ((), ())), preferred_element_type=F32)


def _silu(z):
    return z * (1.0 / (1.0 + jnp.exp(-z)))


def _rms_rows(x, g):
    ms = jnp.mean(x * x, axis=-1, keepdims=True)
    return x * lax.rsqrt(ms + RMS_EPS) * g


def _rope_tab_kernel(pos_ref, freq_ref, tabT_ref, tab_ref):
    pos = pos_ref[...].astype(F32)
    ang = freq_ref[...] * pos
    c = jnp.cos(ang)
    s = jnp.sin(ang)
    r = lax.broadcasted_iota(jnp.int32, ang.shape, 0)
    t = jnp.where(r < 2 * 32, c, jnp.where(r < 3 * 32, -s, s))
    tabT_ref[...] = t
    tab_ref[...] = t.T


def _rope_tables(positions):
    n = positions.size
    half = MLA_ROPE // 2
    inv_freq = ROPE_THETA ** (-jnp.arange(half, dtype=F32) / half)
    freq_col = jnp.tile(inv_freq, LANES // half).reshape(LANES, 1)
    pos_row = positions.reshape(1, n)
    return pl.pallas_call(
        _rope_tab_kernel,
        grid=(n // ROPE_TAB_R,),
        in_specs=[pl.BlockSpec((1, ROPE_TAB_R), lambda i: (0, i)),
                  pl.BlockSpec((LANES, 1), lambda i: (0, 0))],
        out_specs=[pl.BlockSpec((LANES, ROPE_TAB_R), lambda i: (0, i)),
                   pl.BlockSpec((ROPE_TAB_R, LANES), lambda i: (i, 0))],
        out_shape=[jax.ShapeDtypeStruct((LANES, n), F32),
                   jax.ShapeDtypeStruct((n, LANES), F32)],
        compiler_params=_params("arbitrary"),
        name="rope_tables",
    )(pos_row, freq_col)


def _chunk_cumsum(la):
    t, n = la.shape
    x = la.reshape(t // SUBLANES, SUBLANES, n)
    row = lax.broadcasted_iota(jnp.int32, x.shape, 1)
    s = 1
    while s < SUBLANES:
        x = x + jnp.where(row >= s, pltpu.roll(x, s, axis=1), 0.0)
        s *= 2
    tiles_per_chunk = CHUNK // SUBLANES
    out = []
    carry = None
    for i in range(t // SUBLANES):
        tile = x[i]
        if i % tiles_per_chunk != 0:
            tile = tile + carry
        carry = tile[SUBLANES - 1:SUBLANES, :]
        out.append(tile)
    return jnp.concatenate(out, axis=0)


def _gla_in_kernel(x_ref, w_ref, wvT_ref, wa_ref, wa2_ref, ba_ref, g_ref,
                   q_ref, kdm_ref, sz_ref, vT_ref, dec_ref):
    bm = x_ref.shape[0]
    z0 = 2 * GLA_HK + GLA_HV
    xb = x_ref[...].astype(BF16)
    q_ref[...] = _dot(xb, w_ref[:, :GLA_HK]).astype(BF16)
    z = _dot(xb, w_ref[:, z0:z0 + GLA_HV])
    sz_ref[...] = (_silu(z) * g_ref[...]).astype(BF16)
    vT_ref[...] = _dot_nt(wvT_ref[...], xb).astype(BF16)

    a = _dot(xb, wa_ref[...]).astype(BF16)
    pre = _dot(a, wa2_ref[...]) + ba_ref[...]
    la = (jnp.minimum(pre, 0.0) - jnp.log(1.0 + jnp.exp(-jnp.abs(pre)))) * (1.0 / GLA_TAU)
    cum = _chunk_cumsum(la)
    chunks = bm // CHUNK
    tots = [cum[(c + 1) * CHUNK - 1:(c + 1) * CHUNK, :] for c in range(chunks)]
    dec_ref[0] = jnp.exp(jnp.concatenate(tots, axis=0))
    tot_rows = jnp.concatenate([jnp.broadcast_to(t, (CHUNK, GLA_HK)) for t in tots], axis=0)
    kd = (_dot(xb, w_ref[:, GLA_HK:2 * GLA_HK]) * jnp.exp(tot_rows - cum)).astype(BF16)

    zeros = jnp.zeros((CHUNK, GLA_DK), BF16)
    for c in range(chunks):
        rows = slice(c * CHUNK, (c + 1) * CHUNK)
        for h in range(GLA_HEADS):
            c0 = h * 2 * GLA_DK
            mine = slice(c0 + (c % 2) * GLA_DK, c0 + (c % 2 + 1) * GLA_DK)
            other = slice(c0 + (1 - c % 2) * GLA_DK, c0 + (2 - c % 2) * GLA_DK)
            kdm_ref[rows, mine] = kd[rows, h * GLA_DK:(h + 1) * GLA_DK]
            kdm_ref[rows, other] = zeros


def _gla_in_proj(x2, w_in, w_a2, b_a, g_out):
    n = x2.shape[0]
    bm = PROJ_BM
    wb = w_in.astype(BF16)
    wvT = wb[:, 2 * GLA_HK:2 * GLA_HK + GLA_HV].T
    wa_p = jnp.pad(wb[:, 2 * GLA_HK + 2 * GLA_HV:], ((0, 0), (0, LANES - GLA_GATE_RANK)))
    wa2_p = jnp.pad(w_a2, ((0, LANES - GLA_GATE_RANK), (0, 0))).astype(BF16)
    row = lambda i: (i, 0)
    return pl.pallas_call(
        _gla_in_kernel,
        grid=(n // bm,),
        in_specs=[pl.BlockSpec((bm, D_MODEL), row),
                  _resident(wb.shape),
                  _resident((GLA_HV, D_MODEL)),
                  _resident((D_MODEL, LANES)),
                  _resident((LANES, GLA_HK)),
                  _resident((1, GLA_HK)),
                  _resident((1, GLA_HV))],
        out_specs=[pl.BlockSpec((bm, GLA_HK), row),
                   pl.BlockSpec((bm, 2 * GLA_HK), row),
                   pl.BlockSpec((bm, GLA_HV), row),
                   pl.BlockSpec((GLA_HV, bm), lambda i: (0, i)),
                   pl.BlockSpec((1, bm // CHUNK, GLA_HK), lambda i: (i, 0, 0))],
        out_shape=[jax.ShapeDtypeStruct((n, GLA_HK), BF16),
                   jax.ShapeDtypeStruct((n, 2 * GLA_HK), BF16),
                   jax.ShapeDtypeStruct((n, GLA_HV), BF16),
                   jax.ShapeDtypeStruct((GLA_HV, n), BF16),
                   jax.ShapeDtypeStruct((n // bm, bm // CHUNK, GLA_HK), F32)],
        compiler_params=_params("arbitrary"),
        name="gla_in_proj",
    )(x2, wb, wvT, wa_p, wa2_p, b_a.reshape(1, GLA_HK), g_out.reshape(1, GLA_HV))


def _gla_rec_kernel(q_ref, kdm_ref, vT_ref, sz_ref, dec_ref, o_ref, st_ref):
    @pl.when(pl.program_id(1) == 0)
    def _():
        st_ref[...] = jnp.zeros_like(st_ref)

    pair = 2 * CHUNK
    chunks_per_block = dec_ref.shape[1]
    eps = RMS_EPS * GLA_DK
    units = [(p, h) for p in range(GLA_TC // pair) for h in range(GLA_HEADS)]

    def pair_update(p, h):
        return _dot(vT_ref[h * GLA_DV:(h + 1) * GLA_DV, p * pair:(p + 1) * pair],
                    kdm_ref[p * pair:(p + 1) * pair, h * 2 * GLA_DK:(h + 1) * 2 * GLA_DK])

    kv_next = pair_update(*units[0])
    for n, (p, h) in enumerate(units):
        kv2 = kv_next
        if n + 1 < len(units):
            kv_next = pair_update(*units[n + 1])
        ks = slice(h * GLA_DK, (h + 1) * GLA_DK)
        vs = slice(h * GLA_DV, (h + 1) * GLA_DV)
        for e in range(2):
            c = 2 * p + e
            rows = slice(c * CHUNK, (c + 1) * CHUNK)
            dec = dec_ref[c // chunks_per_block, c % chunks_per_block:c % chunks_per_block + 1, ks]
            s_new = st_ref[h] * dec + kv2[:, e * GLA_DK:(e + 1) * GLA_DK]
            st_ref[h] = s_new
            o = _dot_nt(q_ref[rows, ks], s_new.astype(BF16))
            ms = jnp.mean(o * o, axis=-1, keepdims=True)
            o_ref[rows, vs] = (o * lax.rsqrt(ms + eps) * sz_ref[rows, vs]).astype(BF16)


def _gla_recurrence(q, kdm, vT, sz, dec, batch, seq):
    n = q.shape[0]
    tc = GLA_TC
    steps = seq // tc
    blocks = tc // PROJ_BM
    row = lambda b, t: (b * steps + t, 0)
    return pl.pallas_call(
        _gla_rec_kernel,
        grid=(batch, steps),
        in_specs=[pl.BlockSpec((tc, GLA_HK), row),
                  pl.BlockSpec((tc, 2 * GLA_HK), row),
                  pl.BlockSpec((GLA_HV, tc), lambda b, t: (0, b * steps + t)),
                  pl.BlockSpec((tc, GLA_HV), row),
                  pl.BlockSpec((blocks, dec.shape[1], GLA_HK), lambda b, t: (b * steps + t, 0, 0))],
        out_specs=pl.BlockSpec((tc, GLA_HV), row),
        out_shape=jax.ShapeDtypeStruct((n, GLA_HV), BF16),
        scratch_shapes=[pltpu.VMEM((GLA_HEADS, GLA_DV, GLA_DK), F32)],
        compiler_params=_params("arbitrary", "arbitrary"),
        name="gla_recurrence",
    )(q, kdm, vT, sz, dec)


def _out_ln_kernel(o_ref, x_ref, w_ref, g_ref, b_ref, y_ref):
    sub = OUT_SUB
    nsub = o_ref.shape[0] // sub

    def proj(r):
        return _dot(o_ref[r * sub:(r + 1) * sub, :], w_ref[...])

    y_next = proj(0)
    for r in range(nsub):
        rows = slice(r * sub, (r + 1) * sub)
        y = y_next
        if r + 1 < nsub:
            y_next = proj(r + 1)
        v = DEEPNORM_ALPHA * x_ref[rows, :] + y
        mu = jnp.mean(v, axis=-1, keepdims=True)
        vc = v - mu
        var = jnp.mean(vc * vc, axis=-1, keepdims=True)
        y_ref[rows, :] = vc * lax.rsqrt(var + LN_EPS) * g_ref[...] + b_ref[...]


def _out_proj_ln(o, x2, w_out, g, b):
    n = x2.shape[0]
    bm = OUT_BM
    return pl.pallas_call(
        _out_ln_kernel,
        grid=(n // bm,),
        in_specs=[pl.BlockSpec((bm, o.shape[1]), lambda i: (i, 0)),
                  pl.BlockSpec((bm, D_MODEL), lambda i: (i, 0)),
                  _resident((o.shape[1], D_MODEL)),
                  _resident((1, D_MODEL)),
                  _resident((1, D_MODEL))],
        out_specs=pl.BlockSpec((bm, D_MODEL), lambda i: (i, 0)),
        out_shape=jax.ShapeDtypeStruct((n, D_MODEL), F32),
        compiler_params=_params("arbitrary"),
        name="out_proj_ln",
    )(o, x2, w_out.astype(BF16), g.reshape(1, D_MODEL), b.reshape(1, D_MODEL))


def _mla_in_kernel(x_ref, wc_ref, wz_ref, wr_ref, gq_ref, gkv_ref, tab_ref,
                   cq_ref, ckv_ref, z_ref, kr_ref):
    xb = x_ref[...].astype(BF16)
    c = _dot(xb, wc_ref[...])
    cq_ref[...] = _rms_rows(c[:, :MLA_Q_RANK], gq_ref[...]).astype(BF16)
    ckv_ref[...] = _rms_rows(c[:, MLA_Q_RANK:], gkv_ref[...]).astype(BF16)
    z_ref[...] = _silu(_dot(xb, wz_ref[...])).astype(BF16)
    a = _dot(xb, wr_ref[...]) * tab_ref[...]
    kr_ref[...] = (a + pltpu.roll(a, LANES // 2, axis=1)).astype(BF16)


def _mla_in_proj(x2, w_in, g_q, g_kv, tab):
    n = x2.shape[0]
    bm = PROJ_BM
    half = MLA_ROPE // 2
    nc = MLA_Q_RANK + MLA_KV_RANK
    wc = w_in[:, :nc].astype(BF16)
    wr = w_in[:, nc:nc + MLA_ROPE]
    wz = w_in[:, nc + MLA_ROPE:].astype(BF16)
    t1, t2 = wr[:, :half], wr[:, half:]
    wr4 = jnp.concatenate([t1, t2, t2, t1], axis=1).astype(BF16)
    row = lambda i: (i, 0)
    return pl.pallas_call(
        _mla_in_kernel,
        grid=(n // bm,),
        in_specs=[pl.BlockSpec((bm, D_MODEL), row),
                  _resident((D_MODEL, nc)),
                  _resident((D_MODEL, MLA_HV)),
                  _resident((D_MODEL, LANES)),
                  _resident((1, MLA_Q_RANK)),
                  _resident((1, MLA_KV_RANK)),
                  pl.BlockSpec((bm, LANES), row)],
        out_specs=[pl.BlockSpec((bm, MLA_Q_RANK), row),
                   pl.BlockSpec((bm, MLA_KV_RANK), row),
                   pl.BlockSpec((bm, MLA_HV), row),
                   pl.BlockSpec((bm, LANES), row)],
        out_shape=[jax.ShapeDtypeStruct((n, MLA_Q_RANK), BF16),
                   jax.ShapeDtypeStruct((n, MLA_KV_RANK), BF16),
                   jax.ShapeDtypeStruct((n, MLA_HV), BF16),
                   jax.ShapeDtypeStruct((n, LANES), BF16)],
        compiler_params=_params("arbitrary"),
        name="mla_in_proj",
    )(x2, wc, wz, wr4, g_q.reshape(1, -1), g_kv.reshape(1, -1), tab)


def _mla_up_kernel(cq_ref, ckv_ref, wqT_ref, wk_ref, wvT_ref, tabT_ref,
                   qT_ref, k_ref, vT_ref):
    cq = cq_ref[...]
    ckv = ckv_ref[...]
    scale = MLA_QK ** -0.5 * math.log2(math.e)
    tab = tabT_ref[...] * scale
    group = 4 * QHEAD
    for g0 in range(0, MLA_HEADS * QHEAD, group):
        acc = _dot_nt(wqT_ref[g0:g0 + group, :], cq)
        for h0 in range(0, group, QHEAD):
            r0 = g0 + h0
            qT_ref[r0:r0 + LANES, :] = (acc[h0:h0 + LANES] * scale).astype(BF16)
            qT_ref[r0 + LANES:r0 + QHEAD, :] = (acc[h0 + LANES:h0 + QHEAD] * tab).astype(BF16)
    k_ref[...] = _dot(ckv, wk_ref[...]).astype(BF16)
    vT_ref[...] = _dot_nt(wvT_ref[...], ckv).astype(BF16)


def _mla_up_proj(cq, ckv, w_uq, w_ukv, tabT):
    n = cq.shape[0]
    bm = OUT_BM
    half = MLA_ROPE // 2
    wq = w_uq.astype(BF16).reshape(MLA_Q_RANK, MLA_HEADS, MLA_QK)
    nope = wq[:, :, :MLA_NOPE]
    t1 = wq[:, :, MLA_NOPE:MLA_NOPE + half]
    t2 = wq[:, :, MLA_NOPE + half:]
    wq4 = jnp.concatenate([nope, t1, t2, t2, t1], axis=2)
    wqT = wq4.reshape(MLA_Q_RANK, MLA_HEADS * QHEAD).T
    wkv = w_ukv.astype(BF16).reshape(MLA_KV_RANK, MLA_HEADS, MLA_NOPE + MLA_V)
    wk = wkv[:, :, :MLA_NOPE].reshape(MLA_KV_RANK, MLA_HEADS * MLA_NOPE)
    wvT = wkv[:, :, MLA_NOPE:].reshape(MLA_KV_RANK, MLA_HV).T
    row = lambda i: (i, 0)
    col = lambda i: (0, i)
    return pl.pallas_call(
        _mla_up_kernel,
        grid=(n // bm,),
        in_specs=[pl.BlockSpec((bm, MLA_Q_RANK), row),
                  pl.BlockSpec((bm, MLA_KV_RANK), row),
                  _resident((MLA_HEADS * QHEAD, MLA_Q_RANK)),
                  _resident((MLA_KV_RANK, MLA_HEADS * MLA_NOPE)),
                  _resident((MLA_HV, MLA_KV_RANK)),
                  pl.BlockSpec((LANES, bm), col)],
        out_specs=[pl.BlockSpec((MLA_HEADS * QHEAD, bm), col),
                   pl.BlockSpec((bm, MLA_HEADS * MLA_NOPE), row),
                   pl.BlockSpec((MLA_HV, bm), col)],
        out_shape=[jax.ShapeDtypeStruct((MLA_HEADS * QHEAD, n), BF16),
                   jax.ShapeDtypeStruct((n, MLA_HEADS * MLA_NOPE), BF16),
                   jax.ShapeDtypeStruct((MLA_HV, n), BF16)],
        compiler_params=_params("arbitrary"),
        name="mla_up_proj",
    )(cq, ckv, wqT, wk, wvT, tabT)


def _mla_attn_kernel(qT_ref, k_ref, kr_ref, vT_ref, z_ref, o_ref, s_ref):
    t = ATT_T
    nt = qT_ref.shape[1] // t
    kk = lax.broadcasted_iota(jnp.int32, (t, t), 0) // CHUNK
    qq = lax.broadcasted_iota(jnp.int32, (t, t), 1) // CHUNK
    visible = kk <= qq
    ones = jnp.ones((BF16_ROWS, t), BF16)
    blocks = [(i, j) for i in range(nt) for j in range(i + 1)]

    def head_steps(h, slot):
        m = [None] * nt
        acc = [None] * nt

        def score(idx, i, j):
            ks = slice(j * t, (j + 1) * t)
            kcat = jnp.concatenate([k_ref[ks, h * MLA_NOPE:(h + 1) * MLA_NOPE],
                                    kr_ref[ks, :]], axis=1)
            s = _dot(kcat, qT_ref[h * QHEAD:(h + 1) * QHEAD, i * t:(i + 1) * t])
            if j == i:
                s = jnp.where(visible, s, -jnp.inf)
            m_blk = jnp.max(s, axis=0, keepdims=True)
            m[i] = m_blk if m[i] is None else jnp.maximum(m[i], m_blk)
            s_ref[slot, idx] = s

        def value(idx, i, j):
            p = jnp.exp2(s_ref[slot, idx] - m[i]).astype(BF16)
            v_aug = jnp.concatenate([vT_ref[h * MLA_V:(h + 1) * MLA_V, j * t:(j + 1) * t],
                                     ones], axis=0)
            pv = _dot(v_aug, p)
            acc[i] = pv if acc[i] is None else acc[i] + pv
            if j == i:
                a = acc[i]
                o = (a[:MLA_V] * (1.0 / a[MLA_V:MLA_V + 1])).T
                rows = slice(i * t, (i + 1) * t)
                cols = slice(h * MLA_V, (h + 1) * MLA_V)
                o_ref[rows, cols] = (o * z_ref[rows, cols]).astype(BF16)

        return ([functools.partial(score, idx, i, j) for idx, (i, j) in enumerate(blocks)],
                [functools.partial(value, idx, i, j) for idx, (i, j) in enumerate(blocks)])

    steps = [head_steps(h, h % 2) for h in range(ATT_HB)]
    for f in steps[0][0]:
        f()
    for h in range(ATT_HB):
        nxt = steps[h + 1][0] if h + 1 < ATT_HB else []
        for n, f in enumerate(steps[h][1]):
            if n < len(nxt):
                nxt[n]()
            f()


def _mla_attention(qT, k, kr, vT, z, batch, seq):
    n = k.shape[0]
    hb = ATT_HB
    nt = seq // ATT_T
    n_blocks = nt * (nt + 1) // 2
    return pl.pallas_call(
        _mla_attn_kernel,
        grid=(batch, MLA_HEADS // hb),
        in_specs=[pl.BlockSpec((hb * QHEAD, seq), lambda b, h: (h, b)),
                  pl.BlockSpec((seq, hb * MLA_NOPE), lambda b, h: (b, h)),
                  pl.BlockSpec((seq, LANES), lambda b, h: (b, 0)),
                  pl.BlockSpec((hb * MLA_V, seq), lambda b, h: (h, b)),
                  pl.BlockSpec((seq, hb * MLA_V), lambda b, h: (b, h))],
        out_specs=pl.BlockSpec((seq, hb * MLA_V), lambda b, h: (b, h)),
        out_shape=jax.ShapeDtypeStruct((n, MLA_HV), BF16),
        scratch_shapes=[pltpu.VMEM((2, n_blocks, ATT_T, ATT_T), F32)],
        compiler_params=_params("arbitrary", "arbitrary"),
        name="mla_attention",
    )(qT, k, kr, vT, z)


def kernel(x, positions, gla_w_in, gla_w_a2, gla_b_a, gla_g_out, gla_w_out,
           mla_w_in, mla_g_q, mla_w_uq, mla_g_kv, mla_w_ukv, mla_w_out, ln_g, ln_b):
    batch, seq, d = x.shape
    assert d == D_MODEL and seq % GLA_TC == 0 and seq % ATT_T == 0
    x2 = x.reshape(batch * seq, d)

    q, kdm, sz, vT, dec = _gla_in_proj(x2, gla_w_in[0], gla_w_a2[0], gla_b_a[0], gla_g_out[0])
    o = _gla_recurrence(q, kdm, vT, sz, dec, batch, seq)
    x2 = _out_proj_ln(o, x2, gla_w_out[0], ln_g[0], ln_b[0])

    tabT, tab = _rope_tables(positions)
    cq, ckv, z, kr = _mla_in_proj(x2, mla_w_in[0], mla_g_q[0], mla_g_kv[0], tab)
    qT, kn, vT = _mla_up_proj(cq, ckv, mla_w_uq[0], mla_w_ukv[0], tabT)
    o = _mla_attention(qT, kn, kr, vT, z, batch, seq)
    x2 = _out_proj_ln(o, x2, mla_w_out[0], ln_g[1], ln_b[1])
    return x2.reshape(batch, seq, d)
```

```python
import functools
import math

import jax
import jax.numpy as jnp
from jax import lax
from jax.experimental import pallas as pl
from jax.experimental.pallas import tpu as pltpu

BF16 = jnp.bfloat16
F32 = jnp.float32

D_MODEL = 2048
DEPTH = 2
CHUNK = 64
DEEPNORM_ALPHA = float((2 * DEPTH) ** 0.25)
LN_EPS = 1e-5
RMS_EPS = 1e-6

GLA_HEADS = 4
GLA_DK = 256
GLA_DV = 512
GLA_HK = GLA_HEADS * GLA_DK
GLA_HV = GLA_HEADS * GLA_DV
GLA_GATE_RANK = 16
GLA_TAU = 16.0

MLA_HEADS = 16
MLA_Q_RANK = 512
MLA_KV_RANK = 512
MLA_NOPE = 128
MLA_ROPE = 64
MLA_V = 128
MLA_QK = MLA_NOPE + MLA_ROPE
MLA_HV = MLA_HEADS * MLA_V
ROPE_THETA = 10000.0

LANES = 128
SUBLANES = 8
BF16_ROWS = 16
MXU_DIM = 256
VMEM_LIMIT = 56 * 1024 * 1024

PROJ_BM = 256
OUT_BM = 512
OUT_SUB = 128
GLA_TC = 512
ATT_T = 256
ATT_HB = 4
ATT_LAG = 8
ATT_RING = 20
ROPE_TAB_R = 2048
QHEAD = 2 * LANES


def _params(*sem):
    return pltpu.CompilerParams(dimension_semantics=sem, vmem_limit_bytes=VMEM_LIMIT)


def _resident(shape):
    zeros = (0,) * len(shape)
    return pl.BlockSpec(shape, lambda *_: zeros, pipeline_mode=pl.Buffered(1))


def _dot(a, b):
    return jnp.dot(a, b, preferred_element_type=F32)


def _dot_nt(a, b):
    return lax.dot_general(a, b, (((1,), (1,)), ((), ())), preferred_element_type=F32)


def _silu(z):
    return z * (1.0 / (1.0 + jnp.exp(-z)))


def _rms_rows(x, g):
    ms = jnp.mean(x * x, axis=-1, keepdims=True)
    return x * lax.rsqrt(ms + RMS_EPS) * g


def _rope_tab_kernel(pos_ref, freq_ref, tabT_ref, tab_ref):
    pos = pos_ref[...].astype(F32)
    ang = freq_ref[...] * pos
    c = jnp.cos(ang)
    s = jnp.sin(ang)
    t = jnp.concatenate([c, c, -s, s], axis=0)
    tabT_ref[...] = t
    tab_ref[...] = t.T


def _rope_tables(positions):
    n = positions.size
    half = MLA_ROPE // 2
    inv_freq = ROPE_THETA ** (-jnp.arange(half, dtype=F32) / half)
    freq_col = inv_freq.reshape(half, 1)
    pos_row = positions.reshape(1, n)
    return pl.pallas_call(
        _rope_tab_kernel,
        grid=(n // ROPE_TAB_R,),
        in_specs=[pl.BlockSpec((1, ROPE_TAB_R), lambda i: (0, i)),
                  pl.BlockSpec((half, 1), lambda i: (0, 0))],
        out_specs=[pl.BlockSpec((LANES, ROPE_TAB_R), lambda i: (0, i)),
                   pl.BlockSpec((ROPE_TAB_R, LANES), lambda i: (i, 0))],
        out_shape=[jax.ShapeDtypeStruct((LANES, n), F32),
                   jax.ShapeDtypeStruct((n, LANES), F32)],
        compiler_params=_params("arbitrary"),
        name="rope_tables",
    )(pos_row, freq_col)


def _chunk_cumsum(la):
    t, n = la.shape
    x = la.reshape(t // SUBLANES, SUBLANES, n)
    row = lax.broadcasted_iota(jnp.int32, x.shape, 1)
    s = 1
    while s < SUBLANES:
        x = x + jnp.where(row >= s, pltpu.roll(x, s, axis=1), 0.0)
        s *= 2
    tiles_per_chunk = CHUNK // SUBLANES
    out = []
    carry = None
    for i in range(t // SUBLANES):
        tile = x[i]
        if i % tiles_per_chunk != 0:
            tile = tile + carry
        carry = tile[SUBLANES - 1:SUBLANES, :]
        out.append(tile)
    return jnp.concatenate(out, axis=0)


def _gla_in_kernel(x_ref, w_ref, wvT_ref, wa_ref, wa2_ref, ba_ref, g_ref,
                   q_ref, kdm_ref, sz_ref, vT_ref, dec_ref):
    bm = x_ref.shape[0]
    z0 = 2 * GLA_HK + GLA_HV
    xb = x_ref[...].astype(BF16)
    q_ref[...] = _dot(xb, w_ref[:, :GLA_HK]).astype(BF16)
    z = _dot(xb, w_ref[:, z0:z0 + GLA_HV])
    sz_ref[...] = (_silu(z) * g_ref[...]).astype(BF16)
    vT_ref[...] = _dot_nt(wvT_ref[...], xb).astype(BF16)

    a = _dot(xb, wa_ref[...]).astype(BF16)
    pre = _dot(a, wa2_ref[...]) + ba_ref[...]
    la = (jnp.minimum(pre, 0.0) - jnp.log(1.0 + jnp.exp(-jnp.abs(pre)))) * (1.0 / GLA_TAU)
    cum = _chunk_cumsum(la)
    chunks = bm // CHUNK
    tots = [cum[(c + 1) * CHUNK - 1:(c + 1) * CHUNK, :] for c in range(chunks)]
    dec_ref[0] = jnp.exp(jnp.concatenate(tots, axis=0))
    tot_rows = jnp.concatenate([jnp.broadcast_to(t, (CHUNK, GLA_HK)) for t in tots], axis=0)
    kd = (_dot(xb, w_ref[:, GLA_HK:2 * GLA_HK]) * jnp.exp(tot_rows - cum)).astype(BF16)

    zeros = jnp.zeros((CHUNK, GLA_DK), BF16)
    for c in range(chunks):
        rows = slice(c * CHUNK, (c + 1) * CHUNK)
        for h in range(GLA_HEADS):
            c0 = h * 2 * GLA_DK
            mine = slice(c0 + (c % 2) * GLA_DK, c0 + (c % 2 + 1) * GLA_DK)
            other = slice(c0 + (1 - c % 2) * GLA_DK, c0 + (2 - c % 2) * GLA_DK)
            kdm_ref[rows, mine] = kd[rows, h * GLA_DK:(h + 1) * GLA_DK]
            kdm_ref[rows, other] = zeros


def _gla_in_proj(x2, w_in, w_a2, b_a, g_out):
    n = x2.shape[0]
    bm = PROJ_BM
    wb = w_in.astype(BF16)
    wvT = wb[:, 2 * GLA_HK:2 * GLA_HK + GLA_HV].T
    wa_p = jnp.pad(wb[:, 2 * GLA_HK + 2 * GLA_HV:], ((0, 0), (0, LANES - GLA_GATE_RANK)))
    wa2_p = jnp.pad(w_a2, ((0, LANES - GLA_GATE_RANK), (0, 0))).astype(BF16)
    row = lambda i: (i, 0)
    return pl.pallas_call(
        _gla_in_kernel,
        grid=(n // bm,),
        in_specs=[pl.BlockSpec((bm, D_MODEL), row),
                  _resident(wb.shape),
                  _resident((GLA_HV, D_MODEL)),
                  _resident((D_MODEL, LANES)),
                  _resident((LANES, GLA_HK)),
                  _resident((1, GLA_HK)),
                  _resident((1, GLA_HV))],
        out_specs=[pl.BlockSpec((bm, GLA_HK), row),
                   pl.BlockSpec((bm, 2 * GLA_HK), row),
                   pl.BlockSpec((bm, GLA_HV), row),
                   pl.BlockSpec((GLA_HV, bm), lambda i: (0, i)),
                   pl.BlockSpec((1, bm // CHUNK, GLA_HK), lambda i: (i, 0, 0))],
        out_shape=[jax.ShapeDtypeStruct((n, GLA_HK), BF16),
                   jax.ShapeDtypeStruct((n, 2 * GLA_HK), BF16),
                   jax.ShapeDtypeStruct((n, GLA_HV), BF16),
                   jax.ShapeDtypeStruct((GLA_HV, n), BF16),
                   jax.ShapeDtypeStruct((n // bm, bm // CHUNK, GLA_HK), F32)],
        compiler_params=_params("arbitrary"),
        name="gla_in_proj",
    )(x2, wb, wvT, wa_p, wa2_p, b_a.reshape(1, GLA_HK), g_out.reshape(1, GLA_HV))


def _gla_rec_kernel(q_ref, kdm_ref, vT_ref, sz_ref, dec_ref, o_ref, st_ref):
    @pl.when(pl.program_id(1) == 0)
    def _():
        st_ref[...] = jnp.zeros_like(st_ref)

    pair = 2 * CHUNK
    chunks_per_block = dec_ref.shape[1]
    eps = RMS_EPS * GLA_DK
    units = [(p, h) for p in range(GLA_TC // pair) for h in range(GLA_HEADS)]

    def pair_update(p, h):
        return _dot(vT_ref[h * GLA_DV:(h + 1) * GLA_DV, p * pair:(p + 1) * pair],
                    kdm_ref[p * pair:(p + 1) * pair, h * 2 * GLA_DK:(h + 1) * 2 * GLA_DK])

    kv_next = pair_update(*units[0])
    for n, (p, h) in enumerate(units):
        kv2 = kv_next
        if n + 1 < len(units):
            kv_next = pair_update(*units[n + 1])
        ks = slice(h * GLA_DK, (h + 1) * GLA_DK)
        vs = slice(h * GLA_DV, (h + 1) * GLA_DV)
        for e in range(2):
            c = 2 * p + e
            rows = slice(c * CHUNK, (c + 1) * CHUNK)
            dec = dec_ref[c // chunks_per_block, c % chunks_per_block:c % chunks_per_block + 1, ks]
            s_new = st_ref[h] * dec + kv2[:, e * GLA_DK:(e + 1) * GLA_DK]
            st_ref[h] = s_new
            o = _dot_nt(q_ref[rows, ks], s_new.astype(BF16))
            ms = jnp.mean(o * o, axis=-1, keepdims=True)
            o_ref[rows, vs] = (o * lax.rsqrt(ms + eps) * sz_ref[rows, vs]).astype(BF16)


def _gla_recurrence(q, kdm, vT, sz, dec, batch, seq):
    n = q.shape[0]
    tc = GLA_TC
    steps = seq // tc
    blocks = tc // PROJ_BM
    row = lambda b, t: (b * steps + t, 0)
    return pl.pallas_call(
        _gla_rec_kernel,
        grid=(batch, steps),
        in_specs=[pl.BlockSpec((tc, GLA_HK), row),
                  pl.BlockSpec((tc, 2 * GLA_HK), row),
                  pl.BlockSpec((GLA_HV, tc), lambda b, t: (0, b * steps + t)),
                  pl.BlockSpec((tc, GLA_HV), row),
                  pl.BlockSpec((blocks, dec.shape[1], GLA_HK), lambda b, t: (b * steps + t, 0, 0))],
        out_specs=pl.BlockSpec((tc, GLA_HV), row),
        out_shape=jax.ShapeDtypeStruct((n, GLA_HV), BF16),
        scratch_shapes=[pltpu.VMEM((GLA_HEADS, GLA_DV, GLA_DK), F32)],
        compiler_params=_params("arbitrary", "arbitrary"),
        name="gla_recurrence",
    )(q, kdm, vT, sz, dec)


def _out_ln_kernel(o_ref, x_ref, w_ref, g_ref, b_ref, y_ref):
    sub = OUT_SUB
    nsub = o_ref.shape[0] // sub

    def proj(r):
        return _dot(o_ref[r * sub:(r + 1) * sub, :], w_ref[...])

    y_next = proj(0)
    for r in range(nsub):
        rows = slice(r * sub, (r + 1) * sub)
        y = y_next
        if r + 1 < nsub:
            y_next = proj(r + 1)
        v = DEEPNORM_ALPHA * x_ref[rows, :] + y
        mu = jnp.mean(v, axis=-1, keepdims=True)
        vc = v - mu
        var = jnp.mean(vc * vc, axis=-1, keepdims=True)
        y_ref[rows, :] = vc * lax.rsqrt(var + LN_EPS) * g_ref[...] + b_ref[...]


def _out_proj_ln(o, x2, w_out, g, b):
    n = x2.shape[0]
    bm = OUT_BM
    return pl.pallas_call(
        _out_ln_kernel,
        grid=(n // bm,),
        in_specs=[pl.BlockSpec((bm, o.shape[1]), lambda i: (i, 0)),
                  pl.BlockSpec((bm, D_MODEL), lambda i: (i, 0)),
                  _resident((o.shape[1], D_MODEL)),
                  _resident((1, D_MODEL)),
                  _resident((1, D_MODEL))],
        out_specs=pl.BlockSpec((bm, D_MODEL), lambda i: (i, 0)),
        out_shape=jax.ShapeDtypeStruct((n, D_MODEL), F32),
        compiler_params=_params("arbitrary"),
        name="out_proj_ln",
    )(o, x2, w_out.astype(BF16), g.reshape(1, D_MODEL), b.reshape(1, D_MODEL))


def _mla_in_kernel(x_ref, wc_ref, wz_ref, wr_ref, gq_ref, gkv_ref, tab_ref,
                   cq_ref, ckv_ref, z_ref, kr_ref):
    xb = x_ref[...].astype(BF16)
    c = _dot(xb, wc_ref[...])
    cq_ref[...] = _rms_rows(c[:, :MLA_Q_RANK], gq_ref[...]).astype(BF16)
    ckv_ref[...] = _rms_rows(c[:, MLA_Q_RANK:], gkv_ref[...]).astype(BF16)
    z_ref[...] = _silu(_dot(xb, wz_ref[...])).astype(BF16)
    a = _dot(xb, wr_ref[...]) * tab_ref[...]
    kr_ref[...] = (a + pltpu.roll(a, LANES // 2, axis=1)).astype(BF16)


def _mla_in_proj(x2, w_in, g_q, g_kv, tab):
    n = x2.shape[0]
    bm = PROJ_BM
    half = MLA_ROPE // 2
    nc = MLA_Q_RANK + MLA_KV_RANK
    wc = w_in[:, :nc].astype(BF16)
    wr = w_in[:, nc:nc + MLA_ROPE]
    wz = w_in[:, nc + MLA_ROPE:].astype(BF16)
    t1, t2 = wr[:, :half], wr[:, half:]
    wr4 = jnp.concatenate([t1, t2, t2, t1], axis=1).astype(BF16)
    row = lambda i: (i, 0)
    return pl.pallas_call(
        _mla_in_kernel,
        grid=(n // bm,),
        in_specs=[pl.BlockSpec((bm, D_MODEL), row),
                  _resident((D_MODEL, nc)),
                  _resident((D_MODEL, MLA_HV)),
                  _resident((D_MODEL, LANES)),
                  _resident((1, MLA_Q_RANK)),
                  _resident((1, MLA_KV_RANK)),
                  pl.BlockSpec((bm, LANES), row)],
        out_specs=[pl.BlockSpec((bm, MLA_Q_RANK), row),
                   pl.BlockSpec((bm, MLA_KV_RANK), row),
                   pl.BlockSpec((bm, MLA_HV), row),
                   pl.BlockSpec((bm, LANES), row)],
        out_shape=[jax.ShapeDtypeStruct((n, MLA_Q_RANK), BF16),
                   jax.ShapeDtypeStruct((n, MLA_KV_RANK), BF16),
                   jax.ShapeDtypeStruct((n, MLA_HV), BF16),
                   jax.ShapeDtypeStruct((n, LANES), BF16)],
        compiler_params=_params("arbitrary"),
        name="mla_in_proj",
    )(x2, wc, wz, wr4, g_q.reshape(1, -1), g_kv.reshape(1, -1), tab)


def _mla_up_kernel(cq_ref, ckv_ref, wqT_ref, wk_ref, wvT_ref, tabT_ref,
                   qT_ref, k_ref, vT_ref):
    cq = cq_ref[...]
    ckv = ckv_ref[...]
    scale = MLA_QK ** -0.5 * math.log2(math.e)
    tab = tabT_ref[...] * scale
    group = 4 * QHEAD
    for g0 in range(0, MLA_HEADS * QHEAD, group):
        acc = _dot_nt(wqT_ref[g0:g0 + group, :], cq)
        for h0 in range(0, group, QHEAD):
            r0 = g0 + h0
            qT_ref[r0:r0 + LANES, :] = (acc[h0:h0 + LANES] * scale).astype(BF16)
            qT_ref[r0 + LANES:r0 + QHEAD, :] = (acc[h0 + LANES:h0 + QHEAD] * tab).astype(BF16)
    k_ref[...] = _dot(ckv, wk_ref[...]).astype(BF16)
    vT_ref[...] = _dot_nt(wvT_ref[...], ckv).astype(BF16)


def _mla_up_proj(cq, ckv, w_uq, w_ukv, tabT):
    n = cq.shape[0]
    bm = OUT_BM
    half = MLA_ROPE // 2
    wq = w_uq.astype(BF16).reshape(MLA_Q_RANK, MLA_HEADS, MLA_QK)
    nope = wq[:, :, :MLA_NOPE]
    t1 = wq[:, :, MLA_NOPE:MLA_NOPE + half]
    t2 = wq[:, :, MLA_NOPE + half:]
    wq4 = jnp.concatenate([nope, t1, t2, t2, t1], axis=2)
    wqT = wq4.reshape(MLA_Q_RANK, MLA_HEADS * QHEAD).T
    wkv = w_ukv.astype(BF16).reshape(MLA_KV_RANK, MLA_HEADS, MLA_NOPE + MLA_V)
    wk = wkv[:, :, :MLA_NOPE].reshape(MLA_KV_RANK, MLA_HEADS * MLA_NOPE)
    wvT = wkv[:, :, MLA_NOPE:].reshape(MLA_KV_RANK, MLA_HV).T
    row = lambda i: (i, 0)
    col = lambda i: (0, i)
    return pl.pallas_call(
        _mla_up_kernel,
        grid=(n // bm,),
        in_specs=[pl.BlockSpec((bm, MLA_Q_RANK), row),
                  pl.BlockSpec((bm, MLA_KV_RANK), row),
                  _resident((MLA_HEADS * QHEAD, MLA_Q_RANK)),
                  _resident((MLA_KV_RANK, MLA_HEADS * MLA_NOPE)),
                  _resident((MLA_HV, MLA_KV_RANK)),
                  pl.BlockSpec((LANES, bm), col)],
        out_specs=[pl.BlockSpec((MLA_HEADS * QHEAD, bm), col),
                   pl.BlockSpec((bm, MLA_HEADS * MLA_NOPE), row),
                   pl.BlockSpec((MLA_HV, bm), col)],
        out_shape=[jax.ShapeDtypeStruct((MLA_HEADS * QHEAD, n), BF16),
                   jax.ShapeDtypeStruct((n, MLA_HEADS * MLA_NOPE), BF16),
                   jax.ShapeDtypeStruct((MLA_HV, n), BF16)],
        compiler_params=_params("arbitrary"),
        name="mla_up_proj",
    )(cq, ckv, wqT, wk, wvT, tabT)


def _mla_attn_kernel(qT_ref, k_ref, kr_ref, vT_ref, z_ref, o_ref, s_ref):
    t = ATT_T
    tq = 2 * t
    nq = qT_ref.shape[1] // tq
    kk = lax.broadcasted_iota(jnp.int32, (t, t), 0) // CHUNK
    qq = lax.broadcasted_iota(jnp.int32, (t, t), 1) // CHUNK
    visible = kk <= qq
    visible_wide = jnp.concatenate([visible, jnp.ones((t, t), jnp.bool_)], axis=1)
    ones = jnp.ones((BF16_ROWS, t), BF16)
    units = []
    for i in range(nq):
        units += [(i, j, True) for j in range(2 * i + 1)] + [(i, 2 * i + 1, False)]

    ring = s_ref.shape[0]

    def head_steps(h):
        base = h * len(units)
        m = [None] * nq
        acc = [None] * nq
        hq = slice(h * QHEAD, (h + 1) * QHEAD)

        def score(idx, i, j, wide):
            ks = slice(j * t, (j + 1) * t)
            kcat = jnp.concatenate([k_ref[ks, h * MLA_NOPE:(h + 1) * MLA_NOPE],
                                    kr_ref[ks, :]], axis=1)
            if wide:
                s = _dot(kcat, qT_ref[hq, i * tq:(i + 1) * tq])
                if j == 2 * i:
                    s = jnp.where(visible_wide, s, -jnp.inf)
                m_blk = jnp.max(s, axis=0, keepdims=True)
                m[i] = m_blk if m[i] is None else jnp.maximum(m[i], m_blk)
                s_ref[(base + idx) % ring] = s
            else:
                s = _dot(kcat, qT_ref[hq, i * tq + t:(i + 1) * tq])
                s = jnp.where(visible, s, -jnp.inf)
                m_blk = jnp.max(s, axis=0, keepdims=True)
                m[i] = jnp.concatenate([m[i][:, :t], jnp.maximum(m[i][:, t:], m_blk)], axis=1)
                s_ref[(base + idx) % ring, :, :t] = s

        def value(idx, i, j, wide):
            v_aug = jnp.concatenate([vT_ref[h * MLA_V:(h + 1) * MLA_V, j * t:(j + 1) * t],
                                     ones], axis=0)
            if wide:
                p = jnp.exp2(s_ref[(base + idx) % ring] - m[i]).astype(BF16)
                pv = _dot(v_aug, p)
                acc[i] = pv if acc[i] is None else acc[i] + pv
            else:
                p = jnp.exp2(s_ref[(base + idx) % ring, :, :t] - m[i][:, t:]).astype(BF16)
                pv = _dot(v_aug, p)
                a = jnp.concatenate([acc[i][:, :t], acc[i][:, t:] + pv], axis=1)
                o = (a[:MLA_V] * (1.0 / a[MLA_V:MLA_V + 1])).T
                rows = slice(i * tq, (i + 1) * tq)
                cols = slice(h * MLA_V, (h + 1) * MLA_V)
                o_ref[rows, cols] = (o * z_ref[rows, cols]).astype(BF16)

        return ([functools.partial(score, idx, *u) for idx, u in enumerate(units)],
                [functools.partial(value, idx, *u) for idx, u in enumerate(units)])

    score_steps, value_steps = [], []
    for h in range(ATT_HB):
        sc, va = head_steps(h)
        score_steps += sc
        value_steps += va
    assert ATT_LAG >= 2 * nq and ring >= ATT_LAG + 2 * nq
    for n in range(len(score_steps) + ATT_LAG):
        if n < len(score_steps):
            score_steps[n]()
        if n >= ATT_LAG:
            value_steps[n - ATT_LAG]()


def _mla_attention(qT, k, kr, vT, z, batch, seq):
    n = k.shape[0]
    hb = ATT_HB
    return pl.pallas_call(
        _mla_attn_kernel,
        grid=(batch, MLA_HEADS // hb),
        in_specs=[pl.BlockSpec((hb * QHEAD, seq), lambda b, h: (h, b)),
                  pl.BlockSpec((seq, hb * MLA_NOPE), lambda b, h: (b, h)),
                  pl.BlockSpec((seq, LANES), lambda b, h: (b, 0)),
                  pl.BlockSpec((hb * MLA_V, seq), lambda b, h: (h, b)),
                  pl.BlockSpec((seq, hb * MLA_V), lambda b, h: (b, h))],
        out_specs=pl.BlockSpec((seq, hb * MLA_V), lambda b, h: (b, h)),
        out_shape=jax.ShapeDtypeStruct((n, MLA_HV), BF16),
        scratch_shapes=[pltpu.VMEM((ATT_RING, ATT_T, 2 * ATT_T), F32)],
        compiler_params=_params("arbitrary", "arbitrary"),
        name="mla_attention",
    )(qT, k, kr, vT, z)


def kernel(x, positions, gla_w_in, gla_w_a2, gla_b_a, gla_g_out, gla_w_out,
           mla_w_in, mla_g_q, mla_w_uq, mla_g_kv, mla_w_ukv, mla_w_out, ln_g, ln_b):
    batch, seq, d = x.shape
    assert d == D_MODEL and seq % GLA_TC == 0 and seq % (2 * ATT_T) == 0
    x2 = x.reshape(batch * seq, d)

    q, kdm, sz, vT, dec = _gla_in_proj(x2, gla_w_in[0], gla_w_a2[0], gla_b_a[0], gla_g_out[0])
    o = _gla_recurrence(q, kdm, vT, sz, dec, batch, seq)
    x2 = _out_proj_ln(o, x2, gla_w_out[0], ln_g[0], ln_b[0])

    tabT, tab = _rope_tables(positions)
    cq, ckv, z, kr = _mla_in_proj(x2, mla_w_in[0], mla_g_q[0], mla_g_kv[0], tab)
    qT, kn, vT = _mla_up_proj(cq, ckv, mla_w_uq[0], mla_w_ukv[0], tabT)
    o = _mla_attention(qT, kn, kr, vT, z, batch, seq)
    x2 = _out_proj_ln(o, x2, mla_w_out[0], ln_g[1], ln_b[1])
    return x2.reshape(batch, seq, d)
```

```python
import functools
import math

import jax
import jax.numpy as jnp
from jax import lax
from jax.experimental import pallas as pl
from jax.experimental.pallas import tpu as pltpu

BF16 = jnp.bfloat16
F32 = jnp.float32

D_MODEL = 2048
DEPTH = 2
CHUNK = 64
DEEPNORM_ALPHA = float((2 * DEPTH) ** 0.25)
LN_EPS = 1e-5
RMS_EPS = 1e-6

GLA_HEADS = 4
GLA_DK = 256
GLA_DV = 512
GLA_HK = GLA_HEADS * GLA_DK
GLA_HV = GLA_HEADS * GLA_DV
GLA_GATE_RANK = 16
GLA_TAU = 16.0

MLA_HEADS = 16
MLA_Q_RANK = 512
MLA_KV_RANK = 512
MLA_NOPE = 128
MLA_ROPE = 64
MLA_V = 128
MLA_QK = MLA_NOPE + MLA_ROPE
MLA_HV = MLA_HEADS * MLA_V
ROPE_THETA = 10000.0

LANES = 128
SUBLANES = 8
BF16_ROWS = 16
VMEM_LIMIT = 56 * 1024 * 1024
OUT_VMEM_LIMIT = 60 * 1024 * 1024

PROJ_BM = 256
OUT_BM = 1024
OUT_SUB = 128
UP_BM = 512
MLA_IN_BM = 512
GLA_TC = 512
ATT_T = 256
ATT_HB = 4
ROPE_TAB_R = 2048
QHEAD = 2 * LANES


def _params(*sem, vmem=None):
    return pltpu.CompilerParams(dimension_semantics=sem, vmem_limit_bytes=vmem or VMEM_LIMIT)


def _resident(shape):
    zeros = (0,) * len(shape)
    return pl.BlockSpec(shape, lambda *_: zeros, pipeline_mode=pl.Buffered(1))


def _dot(a, b):
    return jnp.dot(a, b, preferred_element_type=F32)


def _dot_nt(a, b):
    return lax.dot_general(a, b, (((1,), (1,)), ((), ())), preferred_element_type=F32)


def _silu(z):
    return z * (1.0 / (1.0 + jnp.exp(-z)))


def _rms_rows(x, g):
    ms = jnp.mean(x * x, axis=-1, keepdims=True)
    return x * lax.rsqrt(ms + RMS_EPS) * g


def _rope_tab_kernel(pos_ref, freq_ref, tabT_ref, tab_ref):
    pos = pos_ref[...].astype(F32)
    ang = freq_ref[...] * pos
    c = jnp.cos(ang)
    s = jnp.sin(ang)
    t = jnp.concatenate([c, c, -s, s], axis=0)
    tabT_ref[...] = t
    tab_ref[...] = t.T


def _rope_tables(positions):
    n = positions.size
    half = MLA_ROPE // 2
    inv_freq = ROPE_THETA ** (-jnp.arange(half, dtype=F32) / half)
    freq_col = inv_freq.reshape(half, 1)
    pos_row = positions.reshape(1, n)
    return pl.pallas_call(
        _rope_tab_kernel,
        grid=(n // ROPE_TAB_R,),
        in_specs=[pl.BlockSpec((1, ROPE_TAB_R), lambda i: (0, i)),
                  pl.BlockSpec((half, 1), lambda i: (0, 0))],
        out_specs=[pl.BlockSpec((LANES, ROPE_TAB_R), lambda i: (0, i)),
                   pl.BlockSpec((ROPE_TAB_R, LANES), lambda i: (i, 0))],
        out_shape=[jax.ShapeDtypeStruct((LANES, n), F32),
                   jax.ShapeDtypeStruct((n, LANES), F32)],
        compiler_params=_params("arbitrary"),
        name="rope_tables",
    )(pos_row, freq_col)


def _chunk_cumsum(la):
    t, n = la.shape
    x = la.reshape(t // SUBLANES, SUBLANES, n)
    row = lax.broadcasted_iota(jnp.int32, x.shape, 1)
    s = 1
    while s < SUBLANES:
        x = x + jnp.where(row >= s, pltpu.roll(x, s, axis=1), 0.0)
        s *= 2
    tiles_per_chunk = CHUNK // SUBLANES
    out = []
    carry = None
    for i in range(t // SUBLANES):
        tile = x[i]
        if i % tiles_per_chunk != 0:
            tile = tile + carry
        carry = tile[SUBLANES - 1:SUBLANES, :]
        out.append(tile)
    return jnp.concatenate(out, axis=0)


def _gla_in_kernel(x_ref, w_ref, wvT_ref, wa_ref, wa2_ref, ba_ref, g_ref,
                   q_ref, kdm_ref, sz_ref, vT_ref, dec_ref):
    bm = x_ref.shape[0]
    z0 = 2 * GLA_HK + GLA_HV
    xb = x_ref[...].astype(BF16)
    q_ref[...] = _dot(xb, w_ref[:, :GLA_HK]).astype(BF16)
    z = _dot(xb, w_ref[:, z0:z0 + GLA_HV])
    sz_ref[...] = (_silu(z) * g_ref[...]).astype(BF16)
    vT_ref[...] = _dot_nt(wvT_ref[...], xb).astype(BF16)

    a = _dot(xb, wa_ref[...]).astype(BF16)
    pre = _dot(a, wa2_ref[...]) + ba_ref[...]
    la = (jnp.minimum(pre, 0.0) - jnp.log(1.0 + jnp.exp(-jnp.abs(pre)))) * (1.0 / GLA_TAU)
    cum = _chunk_cumsum(la)
    chunks = bm // CHUNK
    tots = [cum[(c + 1) * CHUNK - 1:(c + 1) * CHUNK, :] for c in range(chunks)]
    dec_ref[0] = jnp.exp(jnp.concatenate(tots, axis=0))
    tot_rows = jnp.concatenate([jnp.broadcast_to(t, (CHUNK, GLA_HK)) for t in tots], axis=0)
    kd = (_dot(xb, w_ref[:, GLA_HK:2 * GLA_HK]) * jnp.exp(tot_rows - cum)).astype(BF16)

    zeros = jnp.zeros((CHUNK, GLA_DK), BF16)
    for c in range(chunks):
        rows = slice(c * CHUNK, (c + 1) * CHUNK)
        for h in range(GLA_HEADS):
            c0 = h * 2 * GLA_DK
            mine = slice(c0 + (c % 2) * GLA_DK, c0 + (c % 2 + 1) * GLA_DK)
            other = slice(c0 + (1 - c % 2) * GLA_DK, c0 + (2 - c % 2) * GLA_DK)
            kdm_ref[rows, mine] = kd[rows, h * GLA_DK:(h + 1) * GLA_DK]
            kdm_ref[rows, other] = zeros


def _gla_in_proj(x2, w_in, w_a2, b_a, g_out):
    n = x2.shape[0]
    bm = PROJ_BM
    wb = w_in.astype(BF16)
    wvT = wb[:, 2 * GLA_HK:2 * GLA_HK + GLA_HV].T
    wa_p = jnp.pad(wb[:, 2 * GLA_HK + 2 * GLA_HV:], ((0, 0), (0, LANES - GLA_GATE_RANK)))
    wa2_p = jnp.pad(w_a2, ((0, LANES - GLA_GATE_RANK), (0, 0))).astype(BF16)
    row = lambda i: (i, 0)
    return pl.pallas_call(
        _gla_in_kernel,
        grid=(n // bm,),
        in_specs=[pl.BlockSpec((bm, D_MODEL), row),
                  _resident(wb.shape),
                  _resident((GLA_HV, D_MODEL)),
                  _resident((D_MODEL, LANES)),
                  _resident((LANES, GLA_HK)),
                  _resident((1, GLA_HK)),
                  _resident((1, GLA_HV))],
        out_specs=[pl.BlockSpec((bm, GLA_HK), row),
                   pl.BlockSpec((bm, 2 * GLA_HK), row),
                   pl.BlockSpec((bm, GLA_HV), row),
                   pl.BlockSpec((GLA_HV, bm), lambda i: (0, i)),
                   pl.BlockSpec((1, bm // CHUNK, GLA_HK), lambda i: (i, 0, 0))],
        out_shape=[jax.ShapeDtypeStruct((n, GLA_HK), BF16),
                   jax.ShapeDtypeStruct((n, 2 * GLA_HK), BF16),
                   jax.ShapeDtypeStruct((n, GLA_HV), BF16),
                   jax.ShapeDtypeStruct((GLA_HV, n), BF16),
                   jax.ShapeDtypeStruct((n // bm, bm // CHUNK, GLA_HK), F32)],
        compiler_params=_params("arbitrary"),
        name="gla_in_proj",
    )(x2, wb, wvT, wa_p, wa2_p, b_a.reshape(1, GLA_HK), g_out.reshape(1, GLA_HV))


def _gla_rec_kernel(q_ref, kdm_ref, vT_ref, sz_ref, dec_ref, o_ref, st_ref):
    @pl.when(pl.program_id(1) == 0)
    def _():
        st_ref[...] = jnp.zeros_like(st_ref)

    pair = 2 * CHUNK
    chunks_per_block = dec_ref.shape[1]
    eps = RMS_EPS * GLA_DK
    units = [(p, h) for p in range(GLA_TC // pair) for h in range(GLA_HEADS)]

    def pair_update(p, h):
        return _dot(vT_ref[h * GLA_DV:(h + 1) * GLA_DV, p * pair:(p + 1) * pair],
                    kdm_ref[p * pair:(p + 1) * pair, h * 2 * GLA_DK:(h + 1) * 2 * GLA_DK])

    kv_next = pair_update(*units[0])
    for n, (p, h) in enumerate(units):
        kv2 = kv_next
        if n + 1 < len(units):
            kv_next = pair_update(*units[n + 1])
        ks = slice(h * GLA_DK, (h + 1) * GLA_DK)
        vs = slice(h * GLA_DV, (h + 1) * GLA_DV)
        for e in range(2):
            c = 2 * p + e
            rows = slice(c * CHUNK, (c + 1) * CHUNK)
            dec = dec_ref[c // chunks_per_block, c % chunks_per_block:c % chunks_per_block + 1, ks]
            s_new = st_ref[h] * dec + kv2[:, e * GLA_DK:(e + 1) * GLA_DK]
            st_ref[h] = s_new
            o = _dot_nt(q_ref[rows, ks], s_new.astype(BF16))
            ms = jnp.mean(o * o, axis=-1, keepdims=True)
            o_ref[rows, vs] = (o * lax.rsqrt(ms + eps) * sz_ref[rows, vs]).astype(BF16)


def _gla_recurrence(q, kdm, vT, sz, dec, batch, seq):
    n = q.shape[0]
    tc = GLA_TC
    steps = seq // tc
    blocks = tc // PROJ_BM
    row = lambda b, t: (b * steps + t, 0)
    return pl.pallas_call(
        _gla_rec_kernel,
        grid=(batch, steps),
        in_specs=[pl.BlockSpec((tc, GLA_HK), row),
                  pl.BlockSpec((tc, 2 * GLA_HK), row),
                  pl.BlockSpec((GLA_HV, tc), lambda b, t: (0, b * steps + t)),
                  pl.BlockSpec((tc, GLA_HV), row),
                  pl.BlockSpec((blocks, dec.shape[1], GLA_HK), lambda b, t: (b * steps + t, 0, 0))],
        out_specs=pl.BlockSpec((tc, GLA_HV), row),
        out_shape=jax.ShapeDtypeStruct((n, GLA_HV), BF16),
        scratch_shapes=[pltpu.VMEM((GLA_HEADS, GLA_DV, GLA_DK), F32)],
        compiler_params=_params("arbitrary", "arbitrary"),
        name="gla_recurrence",
    )(q, kdm, vT, sz, dec)


def _out_ln_kernel(o_ref, x_ref, w_ref, g_ref, b_ref, y_ref):
    sub = OUT_SUB
    nsub = o_ref.shape[0] // sub

    def proj(r):
        return _dot(o_ref[r * sub:(r + 1) * sub, :], w_ref[...])

    y_next = proj(0)
    for r in range(nsub):
        rows = slice(r * sub, (r + 1) * sub)
        y = y_next
        if r + 1 < nsub:
            y_next = proj(r + 1)
        v = DEEPNORM_ALPHA * x_ref[rows, :] + y
        mu = jnp.mean(v, axis=-1, keepdims=True)
        vc = v - mu
        var = jnp.mean(vc * vc, axis=-1, keepdims=True)
        y_ref[rows, :] = vc * lax.rsqrt(var + LN_EPS) * g_ref[...] + b_ref[...]


def _out_proj_ln(o, x2, w_out, g, b):
    n = x2.shape[0]
    bm = OUT_BM
    return pl.pallas_call(
        _out_ln_kernel,
        grid=(n // bm,),
        in_specs=[pl.BlockSpec((bm, o.shape[1]), lambda i: (i, 0)),
                  pl.BlockSpec((bm, D_MODEL), lambda i: (i, 0)),
                  _resident((o.shape[1], D_MODEL)),
                  _resident((1, D_MODEL)),
                  _resident((1, D_MODEL))],
        out_specs=pl.BlockSpec((bm, D_MODEL), lambda i: (i, 0)),
        out_shape=jax.ShapeDtypeStruct((n, D_MODEL), F32),
        compiler_params=_params("arbitrary", vmem=OUT_VMEM_LIMIT),
        name="out_proj_ln",
    )(o, x2, w_out.astype(BF16), g.reshape(1, D_MODEL), b.reshape(1, D_MODEL))


def _mla_in_kernel(x_ref, wc_ref, wz_ref, wr_ref, gq_ref, gkv_ref, tab_ref,
                   cq_ref, ckv_ref, z_ref, kr_ref):
    xb = x_ref[...].astype(BF16)
    c = _dot(xb, wc_ref[...])
    cq_ref[...] = _rms_rows(c[:, :MLA_Q_RANK], gq_ref[...]).astype(BF16)
    ckv_ref[...] = _rms_rows(c[:, MLA_Q_RANK:], gkv_ref[...]).astype(BF16)
    z_ref[...] = _silu(_dot(xb, wz_ref[...])).astype(BF16)
    a = _dot(xb, wr_ref[...]) * tab_ref[...]
    kr_ref[...] = (a + pltpu.roll(a, LANES // 2, axis=1)).astype(BF16)


def _mla_in_proj(x2, w_in, g_q, g_kv, tab):
    n = x2.shape[0]
    bm = MLA_IN_BM
    half = MLA_ROPE // 2
    nc = MLA_Q_RANK + MLA_KV_RANK
    wc = w_in[:, :nc].astype(BF16)
    wr = w_in[:, nc:nc + MLA_ROPE]
    wz = w_in[:, nc + MLA_ROPE:].astype(BF16)
    t1, t2 = wr[:, :half], wr[:, half:]
    wr4 = jnp.concatenate([t1, t2, t2, t1], axis=1).astype(BF16)
    row = lambda i: (i, 0)
    return pl.pallas_call(
        _mla_in_kernel,
        grid=(n // bm,),
        in_specs=[pl.BlockSpec((bm, D_MODEL), row),
                  _resident((D_MODEL, nc)),
                  _resident((D_MODEL, MLA_HV)),
                  _resident((D_MODEL, LANES)),
                  _resident((1, MLA_Q_RANK)),
                  _resident((1, MLA_KV_RANK)),
                  pl.BlockSpec((bm, LANES), row)],
        out_specs=[pl.BlockSpec((bm, MLA_Q_RANK), row),
                   pl.BlockSpec((bm, MLA_KV_RANK), row),
                   pl.BlockSpec((bm, MLA_HV), row),
                   pl.BlockSpec((bm, LANES), row)],
        out_shape=[jax.ShapeDtypeStruct((n, MLA_Q_RANK), BF16),
                   jax.ShapeDtypeStruct((n, MLA_KV_RANK), BF16),
                   jax.ShapeDtypeStruct((n, MLA_HV), BF16),
                   jax.ShapeDtypeStruct((n, LANES), BF16)],
        compiler_params=_params("arbitrary"),
        name="mla_in_proj",
    )(x2, wc, wz, wr4, g_q.reshape(1, -1), g_kv.reshape(1, -1), tab)


def _mla_up_kernel(cq_ref, ckv_ref, wqT_ref, wk_ref, wvT_ref, tabT_ref,
                   qT_ref, k_ref, vT_ref):
    cq = cq_ref[...]
    ckv = ckv_ref[...]
    scale = MLA_QK ** -0.5 * math.log2(math.e)
    tab = tabT_ref[...] * scale
    group = 4 * QHEAD
    for g0 in range(0, MLA_HEADS * QHEAD, group):
        acc = _dot_nt(wqT_ref[g0:g0 + group, :], cq)
        for h0 in range(0, group, QHEAD):
            r0 = g0 + h0
            qT_ref[r0:r0 + LANES, :] = (acc[h0:h0 + LANES] * scale).astype(BF16)
            qT_ref[r0 + LANES:r0 + QHEAD, :] = (acc[h0 + LANES:h0 + QHEAD] * tab).astype(BF16)
    k_ref[...] = _dot(ckv, wk_ref[...]).astype(BF16)
    vT_ref[...] = _dot_nt(wvT_ref[...], ckv).astype(BF16)


def _mla_up_proj(cq, ckv, w_uq, w_ukv, tabT):
    n = cq.shape[0]
    bm = UP_BM
    half = MLA_ROPE // 2
    wq = w_uq.astype(BF16).reshape(MLA_Q_RANK, MLA_HEADS, MLA_QK)
    nope = wq[:, :, :MLA_NOPE]
    t1 = wq[:, :, MLA_NOPE:MLA_NOPE + half]
    t2 = wq[:, :, MLA_NOPE + half:]
    wq4 = jnp.concatenate([nope, t1, t2, t2, t1], axis=2)
    wqT = wq4.reshape(MLA_Q_RANK, MLA_HEADS * QHEAD).T
    wkv = w_ukv.astype(BF16).reshape(MLA_KV_RANK, MLA_HEADS, MLA_NOPE + MLA_V)
    wk = wkv[:, :, :MLA_NOPE].reshape(MLA_KV_RANK, MLA_HEADS * MLA_NOPE)
    wvT = wkv[:, :, MLA_NOPE:].reshape(MLA_KV_RANK, MLA_HV).T
    row = lambda i: (i, 0)
    col = lambda i: (0, i)
    return pl.pallas_call(
        _mla_up_kernel,
        grid=(n // bm,),
        in_specs=[pl.BlockSpec((bm, MLA_Q_RANK), row),
                  pl.BlockSpec((bm, MLA_KV_RANK), row),
                  _resident((MLA_HEADS * QHEAD, MLA_Q_RANK)),
                  _resident((MLA_KV_RANK, MLA_HEADS * MLA_NOPE)),
                  _resident((MLA_HV, MLA_KV_RANK)),
                  pl.BlockSpec((LANES, bm), col)],
        out_specs=[pl.BlockSpec((MLA_HEADS * QHEAD, bm), col),
                   pl.BlockSpec((bm, MLA_HEADS * MLA_NOPE), row),
                   pl.BlockSpec((MLA_HV, bm), col)],
        out_shape=[jax.ShapeDtypeStruct((MLA_HEADS * QHEAD, n), BF16),
                   jax.ShapeDtypeStruct((n, MLA_HEADS * MLA_NOPE), BF16),
                   jax.ShapeDtypeStruct((MLA_HV, n), BF16)],
        compiler_params=_params("arbitrary"),
        name="mla_up_proj",
    )(cq, ckv, wqT, wk, wvT, tabT)


def _mla_attn_kernel(qT_ref, k_ref, kr_ref, vT_ref, z_ref, o_ref, s_ref):
    t = ATT_T
    nt = qT_ref.shape[1] // t
    kk = lax.broadcasted_iota(jnp.int32, (t, t), 0) // CHUNK
    qq = lax.broadcasted_iota(jnp.int32, (t, t), 1) // CHUNK
    visible = kk <= qq
    ones = jnp.ones((BF16_ROWS, t), BF16)
    blocks = [(i, j) for i in range(nt) for j in range(i + 1)]

    def head_steps(h, slot):
        m = [None] * nt
        acc = [None] * nt

        def score(idx, i, j):
            ks = slice(j * t, (j + 1) * t)
            kcat = jnp.concatenate([k_ref[ks, h * MLA_NOPE:(h + 1) * MLA_NOPE],
                                    kr_ref[ks, :]], axis=1)
            s = _dot(kcat, qT_ref[h * QHEAD:(h + 1) * QHEAD, i * t:(i + 1) * t])
            if j == i:
                s = jnp.where(visible, s, -jnp.inf)
            m_blk = jnp.max(s, axis=0, keepdims=True)
            m[i] = m_blk if m[i] is None else jnp.maximum(m[i], m_blk)
            s_ref[slot, idx] = s

        def value(idx, i, j):
            p = jnp.exp2(s_ref[slot, idx] - m[i]).astype(BF16)
            v_aug = jnp.concatenate([vT_ref[h * MLA_V:(h + 1) * MLA_V, j * t:(j + 1) * t],
                                     ones], axis=0)
            pv = _dot(v_aug, p)
            acc[i] = pv if acc[i] is None else acc[i] + pv
            if j == i:
                a = acc[i]
                o = (a[:MLA_V] * (1.0 / a[MLA_V:MLA_V + 1])).T
                rows = slice(i * t, (i + 1) * t)
                cols = slice(h * MLA_V, (h + 1) * MLA_V)
                o_ref[rows, cols] = (o * z_ref[rows, cols]).astype(BF16)

        return ([functools.partial(score, idx, i, j) for idx, (i, j) in enumerate(blocks)],
                [functools.partial(value, idx, i, j) for idx, (i, j) in enumerate(blocks)])

    steps = [head_steps(h, h % 2) for h in range(ATT_HB)]
    for f in steps[0][0]:
        f()
    for h in range(ATT_HB):
        nxt = steps[h + 1][0] if h + 1 < ATT_HB else []
        for n, f in enumerate(steps[h][1]):
            if n < len(nxt):
                nxt[n]()
            f()


def _mla_attention(qT, k, kr, vT, z, batch, seq):
    n = k.shape[0]
    hb = ATT_HB
    nt = seq // ATT_T
    n_blocks = nt * (nt + 1) // 2
    return pl.pallas_call(
        _mla_attn_kernel,
        grid=(batch, MLA_HEADS // hb),
        in_specs=[pl.BlockSpec((hb * QHEAD, seq), lambda b, h: (h, b)),
                  pl.BlockSpec((seq, hb * MLA_NOPE), lambda b, h: (b, h)),
                  pl.BlockSpec((seq, LANES), lambda b, h: (b, 0)),
                  pl.BlockSpec((hb * MLA_V, seq), lambda b, h: (h, b)),
                  pl.BlockSpec((seq, hb * MLA_V), lambda b, h: (b, h))],
        out_specs=pl.BlockSpec((seq, hb * MLA_V), lambda b, h: (b, h)),
        out_shape=jax.ShapeDtypeStruct((n, MLA_HV), BF16),
        scratch_shapes=[pltpu.VMEM((2, n_blocks, ATT_T, ATT_T), F32)],
        compiler_params=_params("arbitrary", "arbitrary"),
        name="mla_attention",
    )(qT, k, kr, vT, z)


def kernel(x, positions, gla_w_in, gla_w_a2, gla_b_a, gla_g_out, gla_w_out,
           mla_w_in, mla_g_q, mla_w_uq, mla_g_kv, mla_w_ukv, mla_w_out, ln_g, ln_b):
    batch, seq, d = x.shape
    assert d == D_MODEL and seq % GLA_TC == 0 and seq % ATT_T == 0
    x2 = x.reshape(batch * seq, d)

    q, kdm, sz, vT, dec = _gla_in_proj(x2, gla_w_in[0], gla_w_a2[0], gla_b_a[0], gla_g_out[0])
    o = _gla_recurrence(q, kdm, vT, sz, dec, batch, seq)
    x2 = _out_proj_ln(o, x2, gla_w_out[0], ln_g[0], ln_b[0])

    tabT, tab = _rope_tables(positions)
    cq, ckv, z, kr = _mla_in_proj(x2, mla_w_in[0], mla_g_q[0], mla_g_kv[0], tab)
    qT, kn, vT = _mla_up_proj(cq, ckv, mla_w_uq[0], mla_w_ukv[0], tabT)
    o = _mla_attention(qT, kn, kr, vT, z, batch, seq)
    x2 = _out_proj_ln(o, x2, mla_w_out[0], ln_g[1], ln_b[1])
    return x2.reshape(batch, seq, d)
```

```python
import functools
import math

import jax
import jax.numpy as jnp
from jax import lax
from jax.experimental import pallas as pl
from jax.experimental.pallas import tpu as pltpu

BF16 = jnp.bfloat16
F32 = jnp.float32

D_MODEL = 2048
DEPTH = 2
CHUNK = 64
DEEPNORM_ALPHA = float((2 * DEPTH) ** 0.25)
LN_EPS = 1e-5
RMS_EPS = 1e-6

GLA_HEADS = 4
GLA_DK = 256
GLA_DV = 512
GLA_HK = GLA_HEADS * GLA_DK
GLA_HV = GLA_HEADS * GLA_DV
GLA_GATE_RANK = 16
GLA_TAU = 16.0

MLA_HEADS = 16
MLA_Q_RANK = 512
MLA_KV_RANK = 512
MLA_NOPE = 128
MLA_ROPE = 64
MLA_V = 128
MLA_QK = MLA_NOPE + MLA_ROPE
MLA_HV = MLA_HEADS * MLA_V
ROPE_THETA = 10000.0

LANES = 128
SUBLANES = 8
BF16_ROWS = 16
VMEM_LIMIT = 56 * 1024 * 1024
OUT_VMEM_LIMIT = 60 * 1024 * 1024

PROJ_BM = 256
OUT_BM = 1024
OUT_SUB = 128
UP_BM = 512
MLA_IN_BM = 512
GLA_TC = 512
ATT_T = 256
ATT_HB = 4
ROPE_TAB_R = 2048
PREP_BR = 256
QHEAD = 2 * LANES


def _params(*sem, vmem=None):
    return pltpu.CompilerParams(dimension_semantics=sem, vmem_limit_bytes=vmem or VMEM_LIMIT)


def _resident(shape):
    zeros = (0,) * len(shape)
    return pl.BlockSpec(shape, lambda *_: zeros, pipeline_mode=pl.Buffered(1))


def _dot(a, b):
    return jnp.dot(a, b, preferred_element_type=F32)


def _dot_nt(a, b):
    return lax.dot_general(a, b, (((1,), (1,)), ((), ())), preferred_element_type=F32)


def _silu(z):
    return z * (1.0 / (1.0 + jnp.exp(-z)))


def _rms_rows(x, g):
    ms = jnp.mean(x * x, axis=-1, keepdims=True)
    return x * lax.rsqrt(ms + RMS_EPS) * g


def _prep_cast_kernel(w_ref, o_ref):
    o_ref[...] = w_ref[...].astype(BF16)


def _prep_cast(w):
    rows, cols = w.shape
    br = PREP_BR
    return pl.pallas_call(
        _prep_cast_kernel,
        grid=(rows // br,),
        in_specs=[pl.BlockSpec((br, cols), lambda i: (i, 0))],
        out_specs=pl.BlockSpec((br, cols), lambda i: (i, 0)),
        out_shape=jax.ShapeDtypeStruct((rows, cols), BF16),
        compiler_params=_params("arbitrary"),
        name="prep_cast",
    )(w)


def _prep_gla_kernel(w_ref, wb_ref, wvT_ref, wa_ref):
    w = w_ref[...]
    v0 = 2 * GLA_HK
    a0 = 2 * GLA_HK + 2 * GLA_HV
    wb_ref[...] = w.astype(BF16)
    wvT_ref[...] = w[:, v0:v0 + GLA_HV].T.astype(BF16)
    wa_ref[...] = jnp.zeros(wa_ref.shape, BF16)
    wa_ref[:, :GLA_GATE_RANK] = w[:, a0:].astype(BF16)


def _prep_gla(w_in):
    rows, cols = w_in.shape
    br = PREP_BR
    row = lambda i: (i, 0)
    return pl.pallas_call(
        _prep_gla_kernel,
        grid=(rows // br,),
        in_specs=[pl.BlockSpec((br, cols), row)],
        out_specs=[pl.BlockSpec((br, cols), row),
                   pl.BlockSpec((GLA_HV, br), lambda i: (0, i)),
                   pl.BlockSpec((br, LANES), row)],
        out_shape=[jax.ShapeDtypeStruct((rows, cols), BF16),
                   jax.ShapeDtypeStruct((GLA_HV, rows), BF16),
                   jax.ShapeDtypeStruct((rows, LANES), BF16)],
        compiler_params=_params("arbitrary"),
        name="prep_gla",
    )(w_in)


def _prep_mla_kernel(w_ref, wc_ref, wz_ref, wr_ref):
    w = w_ref[...]
    nc = MLA_Q_RANK + MLA_KV_RANK
    half = MLA_ROPE // 2
    wc_ref[...] = w[:, :nc].astype(BF16)
    wz_ref[...] = w[:, nc + MLA_ROPE:].astype(BF16)
    t1 = w[:, nc:nc + half]
    t2 = w[:, nc + half:nc + MLA_ROPE]
    wr_ref[...] = jnp.concatenate([t1, t2, t2, t1], axis=1).astype(BF16)


def _prep_mla(w_in):
    rows, cols = w_in.shape
    br = PREP_BR
    nc = MLA_Q_RANK + MLA_KV_RANK
    row = lambda i: (i, 0)
    return pl.pallas_call(
        _prep_mla_kernel,
        grid=(rows // br,),
        in_specs=[pl.BlockSpec((br, cols), row)],
        out_specs=[pl.BlockSpec((br, nc), row),
                   pl.BlockSpec((br, MLA_HV), row),
                   pl.BlockSpec((br, LANES), row)],
        out_shape=[jax.ShapeDtypeStruct((rows, nc), BF16),
                   jax.ShapeDtypeStruct((rows, MLA_HV), BF16),
                   jax.ShapeDtypeStruct((rows, LANES), BF16)],
        compiler_params=_params("arbitrary"),
        name="prep_mla",
    )(w_in)


def _rope_tab_kernel(pos_ref, freq_ref, tabT_ref, tab_ref):
    pos = pos_ref[...].astype(F32)
    ang = freq_ref[...] * pos
    c = jnp.cos(ang)
    s = jnp.sin(ang)
    t = jnp.concatenate([c, c, -s, s], axis=0)
    tabT_ref[...] = t
    tab_ref[...] = t.T


def _rope_tables(positions):
    n = positions.size
    half = MLA_ROPE // 2
    inv_freq = ROPE_THETA ** (-jnp.arange(half, dtype=F32) / half)
    freq_col = inv_freq.reshape(half, 1)
    pos_row = positions.reshape(1, n)
    return pl.pallas_call(
        _rope_tab_kernel,
        grid=(n // ROPE_TAB_R,),
        in_specs=[pl.BlockSpec((1, ROPE_TAB_R), lambda i: (0, i)),
                  pl.BlockSpec((half, 1), lambda i: (0, 0))],
        out_specs=[pl.BlockSpec((LANES, ROPE_TAB_R), lambda i: (0, i)),
                   pl.BlockSpec((ROPE_TAB_R, LANES), lambda i: (i, 0))],
        out_shape=[jax.ShapeDtypeStruct((LANES, n), F32),
                   jax.ShapeDtypeStruct((n, LANES), F32)],
        compiler_params=_params("arbitrary"),
        name="rope_tables",
    )(pos_row, freq_col)


def _chunk_cumsum(la):
    t, n = la.shape
    x = la.reshape(t // SUBLANES, SUBLANES, n)
    row = lax.broadcasted_iota(jnp.int32, x.shape, 1)
    s = 1
    while s < SUBLANES:
        x = x + jnp.where(row >= s, pltpu.roll(x, s, axis=1), 0.0)
        s *= 2
    tiles_per_chunk = CHUNK // SUBLANES
    out = []
    carry = None
    for i in range(t // SUBLANES):
        tile = x[i]
        if i % tiles_per_chunk != 0:
            tile = tile + carry
        carry = tile[SUBLANES - 1:SUBLANES, :]
        out.append(tile)
    return jnp.concatenate(out, axis=0)


def _gla_in_kernel(x_ref, w_ref, wvT_ref, wa_ref, wa2_ref, ba_ref, g_ref,
                   q_ref, kdm_ref, sz_ref, vT_ref, dec_ref):
    bm = x_ref.shape[0]
    z0 = 2 * GLA_HK + GLA_HV
    xb = x_ref[...].astype(BF16)
    q_ref[...] = _dot(xb, w_ref[:, :GLA_HK]).astype(BF16)
    z = _dot(xb, w_ref[:, z0:z0 + GLA_HV])
    sz_ref[...] = (_silu(z) * g_ref[...]).astype(BF16)
    vT_ref[...] = _dot_nt(wvT_ref[...], xb).astype(BF16)

    a = _dot(xb, wa_ref[...]).astype(BF16)
    pre = _dot(a, wa2_ref[...]) + ba_ref[...]
    la = (jnp.minimum(pre, 0.0) - jnp.log(1.0 + jnp.exp(-jnp.abs(pre)))) * (1.0 / GLA_TAU)
    cum = _chunk_cumsum(la)
    chunks = bm // CHUNK
    tots = [cum[(c + 1) * CHUNK - 1:(c + 1) * CHUNK, :] for c in range(chunks)]
    dec_ref[0] = jnp.exp(jnp.concatenate(tots, axis=0))
    tot_rows = jnp.concatenate([jnp.broadcast_to(t, (CHUNK, GLA_HK)) for t in tots], axis=0)
    kd = (_dot(xb, w_ref[:, GLA_HK:2 * GLA_HK]) * jnp.exp(tot_rows - cum)).astype(BF16)

    zeros = jnp.zeros((CHUNK, GLA_DK), BF16)
    for c in range(chunks):
        rows = slice(c * CHUNK, (c + 1) * CHUNK)
        for h in range(GLA_HEADS):
            c0 = h * 2 * GLA_DK
            mine = slice(c0 + (c % 2) * GLA_DK, c0 + (c % 2 + 1) * GLA_DK)
            other = slice(c0 + (1 - c % 2) * GLA_DK, c0 + (2 - c % 2) * GLA_DK)
            kdm_ref[rows, mine] = kd[rows, h * GLA_DK:(h + 1) * GLA_DK]
            kdm_ref[rows, other] = zeros


def _gla_in_proj(x2, w_in, w_a2, b_a, g_out):
    n = x2.shape[0]
    bm = PROJ_BM
    wb, wvT, wa_p = _prep_gla(w_in)
    wa2_p = jnp.pad(w_a2, ((0, LANES - GLA_GATE_RANK), (0, 0))).astype(BF16)
    row = lambda i: (i, 0)
    return pl.pallas_call(
        _gla_in_kernel,
        grid=(n // bm,),
        in_specs=[pl.BlockSpec((bm, D_MODEL), row),
                  _resident(wb.shape),
                  _resident((GLA_HV, D_MODEL)),
                  _resident((D_MODEL, LANES)),
                  _resident((LANES, GLA_HK)),
                  _resident((1, GLA_HK)),
                  _resident((1, GLA_HV))],
        out_specs=[pl.BlockSpec((bm, GLA_HK), row),
                   pl.BlockSpec((bm, 2 * GLA_HK), row),
                   pl.BlockSpec((bm, GLA_HV), row),
                   pl.BlockSpec((GLA_HV, bm), lambda i: (0, i)),
                   pl.BlockSpec((1, bm // CHUNK, GLA_HK), lambda i: (i, 0, 0))],
        out_shape=[jax.ShapeDtypeStruct((n, GLA_HK), BF16),
                   jax.ShapeDtypeStruct((n, 2 * GLA_HK), BF16),
                   jax.ShapeDtypeStruct((n, GLA_HV), BF16),
                   jax.ShapeDtypeStruct((GLA_HV, n), BF16),
                   jax.ShapeDtypeStruct((n // bm, bm // CHUNK, GLA_HK), F32)],
        compiler_params=_params("arbitrary"),
        name="gla_in_proj",
    )(x2, wb, wvT, wa_p, wa2_p, b_a.reshape(1, GLA_HK), g_out.reshape(1, GLA_HV))


def _gla_rec_kernel(q_ref, kdm_ref, vT_ref, sz_ref, dec_ref, o_ref, st_ref):
    @pl.when(pl.program_id(1) == 0)
    def _():
        st_ref[...] = jnp.zeros_like(st_ref)

    pair = 2 * CHUNK
    chunks_per_block = dec_ref.shape[1]
    eps = RMS_EPS * GLA_DK
    units = [(p, h) for p in range(GLA_TC // pair) for h in range(GLA_HEADS)]

    def pair_update(p, h):
        return _dot(vT_ref[h * GLA_DV:(h + 1) * GLA_DV, p * pair:(p + 1) * pair],
                    kdm_ref[p * pair:(p + 1) * pair, h * 2 * GLA_DK:(h + 1) * 2 * GLA_DK])

    kv_next = pair_update(*units[0])
    for n, (p, h) in enumerate(units):
        kv2 = kv_next
        if n + 1 < len(units):
            kv_next = pair_update(*units[n + 1])
        ks = slice(h * GLA_DK, (h + 1) * GLA_DK)
        vs = slice(h * GLA_DV, (h + 1) * GLA_DV)
        for e in range(2):
            c = 2 * p + e
            rows = slice(c * CHUNK, (c + 1) * CHUNK)
            dec = dec_ref[c // chunks_per_block, c % chunks_per_block:c % chunks_per_block + 1, ks]
            s_new = st_ref[h] * dec + kv2[:, e * GLA_DK:(e + 1) * GLA_DK]
            st_ref[h] = s_new
            o = _dot_nt(q_ref[rows, ks], s_new.astype(BF16))
            ms = jnp.mean(o * o, axis=-1, keepdims=True)
            o_ref[rows, vs] = (o * lax.rsqrt(ms + eps) * sz_ref[rows, vs]).astype(BF16)


def _gla_recurrence(q, kdm, vT, sz, dec, batch, seq):
    n = q.shape[0]
    tc = GLA_TC
    steps = seq // tc
    blocks = tc // PROJ_BM
    row = lambda b, t: (b * steps + t, 0)
    return pl.pallas_call(
        _gla_rec_kernel,
        grid=(batch, steps),
        in_specs=[pl.BlockSpec((tc, GLA_HK), row),
                  pl.BlockSpec((tc, 2 * GLA_HK), row),
                  pl.BlockSpec((GLA_HV, tc), lambda b, t: (0, b * steps + t)),
                  pl.BlockSpec((tc, GLA_HV), row),
                  pl.BlockSpec((blocks, dec.shape[1], GLA_HK), lambda b, t: (b * steps + t, 0, 0))],
        out_specs=pl.BlockSpec((tc, GLA_HV), row),
        out_shape=jax.ShapeDtypeStruct((n, GLA_HV), BF16),
        scratch_shapes=[pltpu.VMEM((GLA_HEADS, GLA_DV, GLA_DK), F32)],
        compiler_params=_params("arbitrary", "arbitrary"),
        name="gla_recurrence",
    )(q, kdm, vT, sz, dec)


def _out_ln_kernel(o_ref, x_ref, w_ref, g_ref, b_ref, y_ref):
    sub = OUT_SUB
    nsub = o_ref.shape[0] // sub

    def proj(r):
        return _dot(o_ref[r * sub:(r + 1) * sub, :], w_ref[...])

    y_next = proj(0)
    for r in range(nsub):
        rows = slice(r * sub, (r + 1) * sub)
        y = y_next
        if r + 1 < nsub:
            y_next = proj(r + 1)
        v = DEEPNORM_ALPHA * x_ref[rows, :] + y
        mu = jnp.mean(v, axis=-1, keepdims=True)
        vc = v - mu
        var = jnp.mean(vc * vc, axis=-1, keepdims=True)
        y_ref[rows, :] = vc * lax.rsqrt(var + LN_EPS) * g_ref[...] + b_ref[...]


def _out_proj_ln(o, x2, w_out, g, b):
    n = x2.shape[0]
    bm = OUT_BM
    return pl.pallas_call(
        _out_ln_kernel,
        grid=(n // bm,),
        in_specs=[pl.BlockSpec((bm, o.shape[1]), lambda i: (i, 0)),
                  pl.BlockSpec((bm, D_MODEL), lambda i: (i, 0)),
                  _resident((o.shape[1], D_MODEL)),
                  _resident((1, D_MODEL)),
                  _resident((1, D_MODEL))],
        out_specs=pl.BlockSpec((bm, D_MODEL), lambda i: (i, 0)),
        out_shape=jax.ShapeDtypeStruct((n, D_MODEL), F32),
        compiler_params=_params("arbitrary", vmem=OUT_VMEM_LIMIT),
        name="out_proj_ln",
    )(o, x2, _prep_cast(w_out), g.reshape(1, D_MODEL), b.reshape(1, D_MODEL))


def _mla_in_kernel(x_ref, wc_ref, wz_ref, wr_ref, gq_ref, gkv_ref, tab_ref,
                   cq_ref, ckv_ref, z_ref, kr_ref):
    xb = x_ref[...].astype(BF16)
    c = _dot(xb, wc_ref[...])
    cq_ref[...] = _rms_rows(c[:, :MLA_Q_RANK], gq_ref[...]).astype(BF16)
    ckv_ref[...] = _rms_rows(c[:, MLA_Q_RANK:], gkv_ref[...]).astype(BF16)
    z_ref[...] = _silu(_dot(xb, wz_ref[...])).astype(BF16)
    a = _dot(xb, wr_ref[...]) * tab_ref[...]
    kr_ref[...] = (a + pltpu.roll(a, LANES // 2, axis=1)).astype(BF16)


def _mla_in_proj(x2, w_in, g_q, g_kv, tab):
    n = x2.shape[0]
    bm = MLA_IN_BM
    nc = MLA_Q_RANK + MLA_KV_RANK
    wc, wz, wr4 = _prep_mla(w_in)
    row = lambda i: (i, 0)
    return pl.pallas_call(
        _mla_in_kernel,
        grid=(n // bm,),
        in_specs=[pl.BlockSpec((bm, D_MODEL), row),
                  _resident((D_MODEL, nc)),
                  _resident((D_MODEL, MLA_HV)),
                  _resident((D_MODEL, LANES)),
                  _resident((1, MLA_Q_RANK)),
                  _resident((1, MLA_KV_RANK)),
                  pl.BlockSpec((bm, LANES), row)],
        out_specs=[pl.BlockSpec((bm, MLA_Q_RANK), row),
                   pl.BlockSpec((bm, MLA_KV_RANK), row),
                   pl.BlockSpec((bm, MLA_HV), row),
                   pl.BlockSpec((bm, LANES), row)],
        out_shape=[jax.ShapeDtypeStruct((n, MLA_Q_RANK), BF16),
                   jax.ShapeDtypeStruct((n, MLA_KV_RANK), BF16),
                   jax.ShapeDtypeStruct((n, MLA_HV), BF16),
                   jax.ShapeDtypeStruct((n, LANES), BF16)],
        compiler_params=_params("arbitrary"),
        name="mla_in_proj",
    )(x2, wc, wz, wr4, g_q.reshape(1, -1), g_kv.reshape(1, -1), tab)


def _mla_up_kernel(cq_ref, ckv_ref, wqT_ref, wk_ref, wvT_ref, tabT_ref,
                   qT_ref, k_ref, vT_ref):
    cq = cq_ref[...]
    ckv = ckv_ref[...]
    scale = MLA_QK ** -0.5 * math.log2(math.e)
    tab = tabT_ref[...] * scale
    group = 4 * QHEAD
    for g0 in range(0, MLA_HEADS * QHEAD, group):
        acc = _dot_nt(wqT_ref[g0:g0 + group, :], cq)
        for h0 in range(0, group, QHEAD):
            r0 = g0 + h0
            qT_ref[r0:r0 + LANES, :] = (acc[h0:h0 + LANES] * scale).astype(BF16)
            qT_ref[r0 + LANES:r0 + QHEAD, :] = (acc[h0 + LANES:h0 + QHEAD] * tab).astype(BF16)
    k_ref[...] = _dot(ckv, wk_ref[...]).astype(BF16)
    vT_ref[...] = _dot_nt(wvT_ref[...], ckv).astype(BF16)


def _mla_up_proj(cq, ckv, w_uq, w_ukv, tabT):
    n = cq.shape[0]
    bm = UP_BM
    half = MLA_ROPE // 2
    wq = w_uq.astype(BF16).reshape(MLA_Q_RANK, MLA_HEADS, MLA_QK)
    nope = wq[:, :, :MLA_NOPE]
    t1 = wq[:, :, MLA_NOPE:MLA_NOPE + half]
    t2 = wq[:, :, MLA_NOPE + half:]
    wq4 = jnp.concatenate([nope, t1, t2, t2, t1], axis=2)
    wqT = wq4.reshape(MLA_Q_RANK, MLA_HEADS * QHEAD).T
    wkv = w_ukv.astype(BF16).reshape(MLA_KV_RANK, MLA_HEADS, MLA_NOPE + MLA_V)
    wk = wkv[:, :, :MLA_NOPE].reshape(MLA_KV_RANK, MLA_HEADS * MLA_NOPE)
    wvT = wkv[:, :, MLA_NOPE:].reshape(MLA_KV_RANK, MLA_HV).T
    row = lambda i: (i, 0)
    col = lambda i: (0, i)
    return pl.pallas_call(
        _mla_up_kernel,
        grid=(n // bm,),
        in_specs=[pl.BlockSpec((bm, MLA_Q_RANK), row),
                  pl.BlockSpec((bm, MLA_KV_RANK), row),
                  _resident((MLA_HEADS * QHEAD, MLA_Q_RANK)),
                  _resident((MLA_KV_RANK, MLA_HEADS * MLA_NOPE)),
                  _resident((MLA_HV, MLA_KV_RANK)),
                  pl.BlockSpec((LANES, bm), col)],
        out_specs=[pl.BlockSpec((MLA_HEADS * QHEAD, bm), col),
                   pl.BlockSpec((bm, MLA_HEADS * MLA_NOPE), row),
                   pl.BlockSpec((MLA_HV, bm), col)],
        out_shape=[jax.ShapeDtypeStruct((MLA_HEADS * QHEAD, n), BF16),
                   jax.ShapeDtypeStruct((n, MLA_HEADS * MLA_NOPE), BF16),
                   jax.ShapeDtypeStruct((MLA_HV, n), BF16)],
        compiler_params=_params("arbitrary"),
        name="mla_up_proj",
    )(cq, ckv, wqT, wk, wvT, tabT)


def _mla_attn_kernel(qT_ref, k_ref, kr_ref, vT_ref, z_ref, o_ref, s_ref):
    t = ATT_T
    nt = qT_ref.shape[1] // t
    kk = lax.broadcasted_iota(jnp.int32, (t, t), 0) // CHUNK
    qq = lax.broadcasted_iota(jnp.int32, (t, t), 1) // CHUNK
    visible = kk <= qq
    ones = jnp.ones((BF16_ROWS, t), BF16)
    blocks = [(i, j) for i in range(nt) for j in range(i + 1)]

    def head_steps(h, slot):
        m = [None] * nt
        acc = [None] * nt

        def score(idx, i, j):
            ks = slice(j * t, (j + 1) * t)
            kcat = jnp.concatenate([k_ref[ks, h * MLA_NOPE:(h + 1) * MLA_NOPE],
                                    kr_ref[ks, :]], axis=1)
            s = _dot(kcat, qT_ref[h * QHEAD:(h + 1) * QHEAD, i * t:(i + 1) * t])
            if j == i:
                s = jnp.where(visible, s, -jnp.inf)
            m_blk = jnp.max(s, axis=0, keepdims=True)
            m[i] = m_blk if m[i] is None else jnp.maximum(m[i], m_blk)
            s_ref[slot, idx] = s

        def value(idx, i, j):
            p = jnp.exp2(s_ref[slot, idx] - m[i]).astype(BF16)
            v_aug = jnp.concatenate([vT_ref[h * MLA_V:(h + 1) * MLA_V, j * t:(j + 1) * t],
                                     ones], axis=0)
            pv = _dot(v_aug, p)
            acc[i] = pv if acc[i] is None else acc[i] + pv
            if j == i:
                a = acc[i]
                o = (a[:MLA_V] * (1.0 / a[MLA_V:MLA_V + 1])).T
                rows = slice(i * t, (i + 1) * t)
                cols = slice(h * MLA_V, (h + 1) * MLA_V)
                o_ref[rows, cols] = (o * z_ref[rows, cols]).astype(BF16)

        return ([functools.partial(score, idx, i, j) for idx, (i, j) in enumerate(blocks)],
                [functools.partial(value, idx, i, j) for idx, (i, j) in enumerate(blocks)])

    steps = [head_steps(h, h % 2) for h in range(ATT_HB)]
    for f in steps[0][0]:
        f()
    for h in range(ATT_HB):
        nxt = steps[h + 1][0] if h + 1 < ATT_HB else []
        for n, f in enumerate(steps[h][1]):
            if n < len(nxt):
                nxt[n]()
            f()


def _mla_attention(qT, k, kr, vT, z, batch, seq):
    n = k.shape[0]
    hb = ATT_HB
    nt = seq // ATT_T
    n_blocks = nt * (nt + 1) // 2
    return pl.pallas_call(
        _mla_attn_kernel,
        grid=(batch, MLA_HEADS // hb),
        in_specs=[pl.BlockSpec((hb * QHEAD, seq), lambda b, h: (h, b)),
                  pl.BlockSpec((seq, hb * MLA_NOPE), lambda b, h: (b, h)),
                  pl.BlockSpec((seq, LANES), lambda b, h: (b, 0)),
                  pl.BlockSpec((hb * MLA_V, seq), lambda b, h: (h, b)),
                  pl.BlockSpec((seq, hb * MLA_V), lambda b, h: (b, h))],
        out_specs=pl.BlockSpec((seq, hb * MLA_V), lambda b, h: (b, h)),
        out_shape=jax.ShapeDtypeStruct((n, MLA_HV), BF16),
        scratch_shapes=[pltpu.VMEM((2, n_blocks, ATT_T, ATT_T), F32)],
        compiler_params=_params("arbitrary", "arbitrary"),
        name="mla_attention",
    )(qT, k, kr, vT, z)


def kernel(x, positions, gla_w_in, gla_w_a2, gla_b_a, gla_g_out, gla_w_out,
           mla_w_in, mla_g_q, mla_w_uq, mla_g_kv, mla_w_ukv, mla_w_out, ln_g, ln_b):
    batch, seq, d = x.shape
    assert d == D_MODEL and seq % GLA_TC == 0 and seq % ATT_T == 0
    x2 = x.reshape(batch * seq, d)

    q, kdm, sz, vT, dec = _gla_in_proj(x2, gla_w_in[0], gla_w_a2[0], gla_b_a[0], gla_g_out[0])
    o = _gla_recurrence(q, kdm, vT, sz, dec, batch, seq)
    x2 = _out_proj_ln(o, x2, gla_w_out[0], ln_g[0], ln_b[0])

    tabT, tab = _rope_tables(positions)
    cq, ckv, z, kr = _mla_in_proj(x2, mla_w_in[0], mla_g_q[0], mla_g_kv[0], tab)
    qT, kn, vT = _mla_up_proj(cq, ckv, mla_w_uq[0], mla_w_ukv[0], tabT)
    o = _mla_attention(qT, kn, kr, vT, z, batch, seq)
    x2 = _out_proj_ln(o, x2, mla_w_out[0], ln_g[1], ln_b[1])
    return x2.reshape(batch, seq, d)
```

```python
import functools
import math

import jax
import jax.numpy as jnp
from jax import lax
from jax.experimental import pallas as pl
from jax.experimental.pallas import tpu as pltpu

BF16 = jnp.bfloat16
F32 = jnp.float32

D_MODEL = 2048
DEPTH = 2
CHUNK = 64
DEEPNORM_ALPHA = float((2 * DEPTH) ** 0.25)
LN_EPS = 1e-5
RMS_EPS = 1e-6

GLA_HEADS = 4
GLA_DK = 256
GLA_DV = 512
GLA_HK = GLA_HEADS * GLA_DK
GLA_HV = GLA_HEADS * GLA_DV
GLA_GATE_RANK = 16
GLA_TAU = 16.0

MLA_HEADS = 16
MLA_Q_RANK = 512
MLA_KV_RANK = 512
MLA_NOPE = 128
MLA_ROPE = 64
MLA_V = 128
MLA_QK = MLA_NOPE + MLA_ROPE
MLA_HV = MLA_HEADS * MLA_V
ROPE_THETA = 10000.0

LANES = 128
SUBLANES = 8
BF16_ROWS = 16
VMEM_LIMIT = 56 * 1024 * 1024

PROJ_BM = 256
OUT_BM = 512
OUT_SUB = 128
UP_BM = 512
MLA_IN_BM = 512
GLA_TC = 512
ATT_T = 256
ATT_HB = 4
ROPE_TAB_R = 2048
PREP_BR = 256
QHEAD = 2 * LANES


def _params(*sem):
    return pltpu.CompilerParams(dimension_semantics=sem, vmem_limit_bytes=VMEM_LIMIT)


def _resident(shape):
    zeros = (0,) * len(shape)
    return pl.BlockSpec(shape, lambda *_: zeros, pipeline_mode=pl.Buffered(1))


def _dot(a, b):
    return jnp.dot(a, b, preferred_element_type=F32)


def _dot_nt(a, b):
    return lax.dot_general(a, b, (((1,), (1,)), ((), ())), preferred_element_type=F32)


def _silu(z):
    return z * (1.0 / (1.0 + jnp.exp(-z)))


def _rms_rows(x, g):
    ms = jnp.mean(x * x, axis=-1, keepdims=True)
    return x * lax.rsqrt(ms + RMS_EPS) * g


def _prep_transpose_kernel(*refs):
    *w_refs, o_ref = refs
    rows = [r[...] for r in w_refs]
    w = rows[0] if len(rows) == 1 else jnp.concatenate(rows, axis=0)
    o_ref[...] = w.T.astype(BF16)


def _prep_transposed(wT, n_out, br, parts, first_block):
    d = wT.shape[1]
    return pl.pallas_call(
        _prep_transpose_kernel,
        grid=(n_out,),
        in_specs=[pl.BlockSpec((br, d), functools.partial(lambda k, j: (first_block(j) + k, 0), k))
                  for k in range(parts)],
        out_specs=pl.BlockSpec((d, br * parts), lambda j: (0, j)),
        out_shape=jax.ShapeDtypeStruct((d, n_out * br * parts), BF16),
        compiler_params=_params("arbitrary"),
        name="prep_transposed",
    )(*([wT] * parts))


def _prep_rows_kernel(w_ref, o_ref):
    o_ref[...] = w_ref[...].astype(BF16)


def _prep_rows(wT, block0, n_blocks, br):
    d = wT.shape[1]
    return pl.pallas_call(
        _prep_rows_kernel,
        grid=(n_blocks,),
        in_specs=[pl.BlockSpec((br, d), lambda j: (block0 + j, 0))],
        out_specs=pl.BlockSpec((br, d), lambda j: (j, 0)),
        out_shape=jax.ShapeDtypeStruct((n_blocks * br, d), BF16),
        compiler_params=_params("arbitrary"),
        name="prep_rows",
    )(wT)


def _rope_tab_kernel(pos_ref, freq_ref, tabT_ref, tab_ref):
    pos = pos_ref[...].astype(F32)
    ang = freq_ref[...] * pos
    c = jnp.cos(ang)
    s = jnp.sin(ang)
    t = jnp.concatenate([c, c, -s, s], axis=0)
    tabT_ref[...] = t
    tab_ref[...] = t.T


def _rope_tables(positions):
    n = positions.size
    half = MLA_ROPE // 2
    inv_freq = ROPE_THETA ** (-jnp.arange(half, dtype=F32) / half)
    freq_col = inv_freq.reshape(half, 1)
    pos_row = positions.reshape(1, n)
    return pl.pallas_call(
        _rope_tab_kernel,
        grid=(n // ROPE_TAB_R,),
        in_specs=[pl.BlockSpec((1, ROPE_TAB_R), lambda i: (0, i)),
                  pl.BlockSpec((half, 1), lambda i: (0, 0))],
        out_specs=[pl.BlockSpec((LANES, ROPE_TAB_R), lambda i: (0, i)),
                   pl.BlockSpec((ROPE_TAB_R, LANES), lambda i: (i, 0))],
        out_shape=[jax.ShapeDtypeStruct((LANES, n), F32),
                   jax.ShapeDtypeStruct((n, LANES), F32)],
        compiler_params=_params("arbitrary"),
        name="rope_tables",
    )(pos_row, freq_col)


def _chunk_cumsum(la):
    t, n = la.shape
    x = la.reshape(t // SUBLANES, SUBLANES, n)
    row = lax.broadcasted_iota(jnp.int32, x.shape, 1)
    s = 1
    while s < SUBLANES:
        x = x + jnp.where(row >= s, pltpu.roll(x, s, axis=1), 0.0)
        s *= 2
    tiles_per_chunk = CHUNK // SUBLANES
    out = []
    carry = None
    for i in range(t // SUBLANES):
        tile = x[i]
        if i % tiles_per_chunk != 0:
            tile = tile + carry
        carry = tile[SUBLANES - 1:SUBLANES, :]
        out.append(tile)
    return jnp.concatenate(out, axis=0)


def _gla_in_kernel(x_ref, w_ref, wvT_ref, wa_ref, wa2_ref, ba_ref, g_ref,
                   q_ref, kdm_ref, sz_ref, vT_ref, dec_ref):
    bm = x_ref.shape[0]
    z0 = 2 * GLA_HK
    xb = x_ref[...].astype(BF16)
    q_ref[...] = _dot(xb, w_ref[:, :GLA_HK]).astype(BF16)
    z = _dot(xb, w_ref[:, z0:z0 + GLA_HV])
    sz_ref[...] = (_silu(z) * g_ref[...]).astype(BF16)
    vT_ref[...] = _dot_nt(wvT_ref[...], xb).astype(BF16)

    a = _dot(xb, wa_ref[...]).astype(BF16)
    pre = _dot(a, wa2_ref[...]) + ba_ref[...]
    la = (jnp.minimum(pre, 0.0) - jnp.log(1.0 + jnp.exp(-jnp.abs(pre)))) * (1.0 / GLA_TAU)
    cum = _chunk_cumsum(la)
    chunks = bm // CHUNK
    tots = [cum[(c + 1) * CHUNK - 1:(c + 1) * CHUNK, :] for c in range(chunks)]
    dec_ref[0] = jnp.exp(jnp.concatenate(tots, axis=0))
    tot_rows = jnp.concatenate([jnp.broadcast_to(t, (CHUNK, GLA_HK)) for t in tots], axis=0)
    kd = (_dot(xb, w_ref[:, GLA_HK:2 * GLA_HK]) * jnp.exp(tot_rows - cum)).astype(BF16)

    zeros = jnp.zeros((CHUNK, GLA_DK), BF16)
    for c in range(chunks):
        rows = slice(c * CHUNK, (c + 1) * CHUNK)
        for h in range(GLA_HEADS):
            c0 = h * 2 * GLA_DK
            mine = slice(c0 + (c % 2) * GLA_DK, c0 + (c % 2 + 1) * GLA_DK)
            other = slice(c0 + (1 - c % 2) * GLA_DK, c0 + (2 - c % 2) * GLA_DK)
            kdm_ref[rows, mine] = kd[rows, h * GLA_DK:(h + 1) * GLA_DK]
            kdm_ref[rows, other] = zeros


def _gla_in_proj(x2, w_in, w_a2, b_a, g_out):
    n = x2.shape[0]
    bm = PROJ_BM
    br = PREP_BR
    wT = jnp.swapaxes(w_in, 0, 1)
    v_blocks = GLA_HV // br
    qk_blocks = 2 * GLA_HK // br
    wb = _prep_transposed(wT, qk_blocks + v_blocks, br, 1,
                          lambda j: j + v_blocks * (j // qk_blocks))
    wvT = _prep_rows(wT, qk_blocks, v_blocks, br)
    wa_p = jnp.pad(w_in[:, 2 * GLA_HK + 2 * GLA_HV:],
                   ((0, 0), (0, LANES - GLA_GATE_RANK))).astype(BF16)
    wa2_p = jnp.pad(w_a2, ((0, LANES - GLA_GATE_RANK), (0, 0))).astype(BF16)
    row = lambda i: (i, 0)
    return pl.pallas_call(
        _gla_in_kernel,
        grid=(n // bm,),
        in_specs=[pl.BlockSpec((bm, D_MODEL), row),
                  _resident(wb.shape),
                  _resident((GLA_HV, D_MODEL)),
                  _resident((D_MODEL, LANES)),
                  _resident((LANES, GLA_HK)),
                  _resident((1, GLA_HK)),
                  _resident((1, GLA_HV))],
        out_specs=[pl.BlockSpec((bm, GLA_HK), row),
                   pl.BlockSpec((bm, 2 * GLA_HK), row),
                   pl.BlockSpec((bm, GLA_HV), row),
                   pl.BlockSpec((GLA_HV, bm), lambda i: (0, i)),
                   pl.BlockSpec((1, bm // CHUNK, GLA_HK), lambda i: (i, 0, 0))],
        out_shape=[jax.ShapeDtypeStruct((n, GLA_HK), BF16),
                   jax.ShapeDtypeStruct((n, 2 * GLA_HK), BF16),
                   jax.ShapeDtypeStruct((n, GLA_HV), BF16),
                   jax.ShapeDtypeStruct((GLA_HV, n), BF16),
                   jax.ShapeDtypeStruct((n // bm, bm // CHUNK, GLA_HK), F32)],
        compiler_params=_params("arbitrary"),
        name="gla_in_proj",
    )(x2, wb, wvT, wa_p, wa2_p, b_a.reshape(1, GLA_HK), g_out.reshape(1, GLA_HV))


def _gla_rec_kernel(q_ref, kdm_ref, vT_ref, sz_ref, dec_ref, o_ref, st_ref):
    @pl.when(pl.program_id(1) == 0)
    def _():
        st_ref[...] = jnp.zeros_like(st_ref)

    pair = 2 * CHUNK
    chunks_per_block = dec_ref.shape[1]
    eps = RMS_EPS * GLA_DK
    units = [(p, h) for p in range(GLA_TC // pair) for h in range(GLA_HEADS)]

    def pair_update(p, h):
        return _dot(vT_ref[h * GLA_DV:(h + 1) * GLA_DV, p * pair:(p + 1) * pair],
                    kdm_ref[p * pair:(p + 1) * pair, h * 2 * GLA_DK:(h + 1) * 2 * GLA_DK])

    kv_next = pair_update(*units[0])
    for n, (p, h) in enumerate(units):
        kv2 = kv_next
        if n + 1 < len(units):
            kv_next = pair_update(*units[n + 1])
        ks = slice(h * GLA_DK, (h + 1) * GLA_DK)
        vs = slice(h * GLA_DV, (h + 1) * GLA_DV)
        for e in range(2):
            c = 2 * p + e
            rows = slice(c * CHUNK, (c + 1) * CHUNK)
            dec = dec_ref[c // chunks_per_block, c % chunks_per_block:c % chunks_per_block + 1, ks]
            s_new = st_ref[h] * dec + kv2[:, e * GLA_DK:(e + 1) * GLA_DK]
            st_ref[h] = s_new
            o = _dot_nt(q_ref[rows, ks], s_new.astype(BF16))
            ms = jnp.mean(o * o, axis=-1, keepdims=True)
            o_ref[rows, vs] = (o * lax.rsqrt(ms + eps) * sz_ref[rows, vs]).astype(BF16)


def _gla_recurrence(q, kdm, vT, sz, dec, batch, seq):
    n = q.shape[0]
    tc = GLA_TC
    steps = seq // tc
    blocks = tc // PROJ_BM
    row = lambda b, t: (b * steps + t, 0)
    return pl.pallas_call(
        _gla_rec_kernel,
        grid=(batch, steps),
        in_specs=[pl.BlockSpec((tc, GLA_HK), row),
                  pl.BlockSpec((tc, 2 * GLA_HK), row),
                  pl.BlockSpec((GLA_HV, tc), lambda b, t: (0, b * steps + t)),
                  pl.BlockSpec((tc, GLA_HV), row),
                  pl.BlockSpec((blocks, dec.shape[1], GLA_HK), lambda b, t: (b * steps + t, 0, 0))],
        out_specs=pl.BlockSpec((tc, GLA_HV), row),
        out_shape=jax.ShapeDtypeStruct((n, GLA_HV), BF16),
        scratch_shapes=[pltpu.VMEM((GLA_HEADS, GLA_DV, GLA_DK), F32)],
        compiler_params=_params("arbitrary", "arbitrary"),
        name="gla_recurrence",
    )(q, kdm, vT, sz, dec)


def _out_ln_kernel(o_ref, x_ref, w_ref, g_ref, b_ref, y_ref):
    sub = OUT_SUB
    nsub = o_ref.shape[0] // sub

    def proj(r):
        return _dot(o_ref[r * sub:(r + 1) * sub, :], w_ref[...])

    y_next = proj(0)
    for r in range(nsub):
        rows = slice(r * sub, (r + 1) * sub)
        y = y_next
        if r + 1 < nsub:
            y_next = proj(r + 1)
        v = DEEPNORM_ALPHA * x_ref[rows, :] + y
        mu = jnp.mean(v, axis=-1, keepdims=True)
        vc = v - mu
        var = jnp.mean(vc * vc, axis=-1, keepdims=True)
        y_ref[rows, :] = vc * lax.rsqrt(var + LN_EPS) * g_ref[...] + b_ref[...]


def _out_proj_ln(o, x2, w_out, g, b):
    n = x2.shape[0]
    bm = OUT_BM
    return pl.pallas_call(
        _out_ln_kernel,
        grid=(n // bm,),
        in_specs=[pl.BlockSpec((bm, o.shape[1]), lambda i: (i, 0)),
                  pl.BlockSpec((bm, D_MODEL), lambda i: (i, 0)),
                  _resident((o.shape[1], D_MODEL)),
                  _resident((1, D_MODEL)),
                  _resident((1, D_MODEL))],
        out_specs=pl.BlockSpec((bm, D_MODEL), lambda i: (i, 0)),
        out_shape=jax.ShapeDtypeStruct((n, D_MODEL), F32),
        compiler_params=_params("arbitrary"),
        name="out_proj_ln",
    )(o, x2, w_out.astype(BF16), g.reshape(1, D_MODEL), b.reshape(1, D_MODEL))


def _mla_in_kernel(x_ref, wc_ref, wz_ref, wr_ref, gq_ref, gkv_ref, tab_ref,
                   cq_ref, ckv_ref, z_ref, kr_ref):
    xb = x_ref[...].astype(BF16)
    c = _dot(xb, wc_ref[...])
    cq_ref[...] = _rms_rows(c[:, :MLA_Q_RANK], gq_ref[...]).astype(BF16)
    ckv_ref[...] = _rms_rows(c[:, MLA_Q_RANK:], gkv_ref[...]).astype(BF16)
    z_ref[...] = _silu(_dot(xb, wz_ref[...])).astype(BF16)
    a = _dot(xb, wr_ref[...]) * tab_ref[...]
    kr_ref[...] = (a + pltpu.roll(a, LANES // 2, axis=1)).astype(BF16)


def _mla_in_proj(x2, w_in, g_q, g_kv, tab):
    n = x2.shape[0]
    bm = MLA_IN_BM
    nc = MLA_Q_RANK + MLA_KV_RANK
    half = MLA_ROPE // 2
    wT = jnp.swapaxes(w_in, 0, 1)
    wc = _prep_transposed(wT, nc // PREP_BR, PREP_BR, 1, lambda j: j)
    wz = _prep_transposed(wT, MLA_HV // (2 * MLA_ROPE), MLA_ROPE, 2,
                          lambda j: nc // MLA_ROPE + 1 + 2 * j)
    wr = w_in[:, nc:nc + MLA_ROPE]
    t1, t2 = wr[:, :half], wr[:, half:]
    wr4 = jnp.concatenate([t1, t2, t2, t1], axis=1).astype(BF16)
    row = lambda i: (i, 0)
    return pl.pallas_call(
        _mla_in_kernel,
        grid=(n // bm,),
        in_specs=[pl.BlockSpec((bm, D_MODEL), row),
                  _resident((D_MODEL, nc)),
                  _resident((D_MODEL, MLA_HV)),
                  _resident((D_MODEL, LANES)),
                  _resident((1, MLA_Q_RANK)),
                  _resident((1, MLA_KV_RANK)),
                  pl.BlockSpec((bm, LANES), row)],
        out_specs=[pl.BlockSpec((bm, MLA_Q_RANK), row),
                   pl.BlockSpec((bm, MLA_KV_RANK), row),
                   pl.BlockSpec((bm, MLA_HV), row),
                   pl.BlockSpec((bm, LANES), row)],
        out_shape=[jax.ShapeDtypeStruct((n, MLA_Q_RANK), BF16),
                   jax.ShapeDtypeStruct((n, MLA_KV_RANK), BF16),
                   jax.ShapeDtypeStruct((n, MLA_HV), BF16),
                   jax.ShapeDtypeStruct((n, LANES), BF16)],
        compiler_params=_params("arbitrary"),
        name="mla_in_proj",
    )(x2, wc, wz, wr4, g_q.reshape(1, -1), g_kv.reshape(1, -1), tab)


def _mla_up_kernel(cq_ref, ckv_ref, wqT_ref, wk_ref, wvT_ref, tabT_ref,
                   qT_ref, k_ref, vT_ref):
    cq = cq_ref[...]
    ckv = ckv_ref[...]
    scale = MLA_QK ** -0.5 * math.log2(math.e)
    tab = tabT_ref[...] * scale
    group = 4 * QHEAD
    for g0 in range(0, MLA_HEADS * QHEAD, group):
        acc = _dot_nt(wqT_ref[g0:g0 + group, :], cq)
        for h0 in range(0, group, QHEAD):
            r0 = g0 + h0
            qT_ref[r0:r0 + LANES, :] = (acc[h0:h0 + LANES] * scale).astype(BF16)
            qT_ref[r0 + LANES:r0 + QHEAD, :] = (acc[h0 + LANES:h0 + QHEAD] * tab).astype(BF16)
    k_ref[...] = _dot(ckv, wk_ref[...]).astype(BF16)
    vT_ref[...] = _dot_nt(wvT_ref[...], ckv).astype(BF16)


def _mla_up_proj(cq, ckv, w_uq, w_ukv, tabT):
    n = cq.shape[0]
    bm = UP_BM
    half = MLA_ROPE // 2
    wq = w_uq.astype(BF16).reshape(MLA_Q_RANK, MLA_HEADS, MLA_QK)
    nope = wq[:, :, :MLA_NOPE]
    t1 = wq[:, :, MLA_NOPE:MLA_NOPE + half]
    t2 = wq[:, :, MLA_NOPE + half:]
    wq4 = jnp.concatenate([nope, t1, t2, t2, t1], axis=2)
    wqT = wq4.reshape(MLA_Q_RANK, MLA_HEADS * QHEAD).T
    wkv = w_ukv.astype(BF16).reshape(MLA_KV_RANK, MLA_HEADS, MLA_NOPE + MLA_V)
    wk = wkv[:, :, :MLA_NOPE].reshape(MLA_KV_RANK, MLA_HEADS * MLA_NOPE)
    wvT = wkv[:, :, MLA_NOPE:].reshape(MLA_KV_RANK, MLA_HV).T
    row = lambda i: (i, 0)
    col = lambda i: (0, i)
    return pl.pallas_call(
        _mla_up_kernel,
        grid=(n // bm,),
        in_specs=[pl.BlockSpec((bm, MLA_Q_RANK), row),
                  pl.BlockSpec((bm, MLA_KV_RANK), row),
                  _resident((MLA_HEADS * QHEAD, MLA_Q_RANK)),
                  _resident((MLA_KV_RANK, MLA_HEADS * MLA_NOPE)),
                  _resident((MLA_HV, MLA_KV_RANK)),
                  pl.BlockSpec((LANES, bm), col)],
        out_specs=[pl.BlockSpec((MLA_HEADS * QHEAD, bm), col),
                   pl.BlockSpec((bm, MLA_HEADS * MLA_NOPE), row),
                   pl.BlockSpec((MLA_HV, bm), col)],
        out_shape=[jax.ShapeDtypeStruct((MLA_HEADS * QHEAD, n), BF16),
                   jax.ShapeDtypeStruct((n, MLA_HEADS * MLA_NOPE), BF16),
                   jax.ShapeDtypeStruct((MLA_HV, n), BF16)],
        compiler_params=_params("arbitrary"),
        name="mla_up_proj",
    )(cq, ckv, wqT, wk, wvT, tabT)


def _mla_attn_kernel(qT_ref, k_ref, kr_ref, vT_ref, z_ref, o_ref, s_ref):
    t = ATT_T
    nt = qT_ref.shape[1] // t
    kk = lax.broadcasted_iota(jnp.int32, (t, t), 0) // CHUNK
    qq = lax.broadcasted_iota(jnp.int32, (t, t), 1) // CHUNK
    visible = kk <= qq
    ones = jnp.ones((BF16_ROWS, t), BF16)
    blocks = [(i, j) for i in range(nt) for j in range(i + 1)]

    def head_steps(h, slot):
        m = [None] * nt
        acc = [None] * nt

        def score(idx, i, j):
            ks = slice(j * t, (j + 1) * t)
            kcat = jnp.concatenate([k_ref[ks, h * MLA_NOPE:(h + 1) * MLA_NOPE],
                                    kr_ref[ks, :]], axis=1)
            s = _dot(kcat, qT_ref[h * QHEAD:(h + 1) * QHEAD, i * t:(i + 1) * t])
            if j == i:
                s = jnp.where(visible, s, -jnp.inf)
            m_blk = jnp.max(s, axis=0, keepdims=True)
            m[i] = m_blk if m[i] is None else jnp.maximum(m[i], m_blk)
            s_ref[slot, idx] = s

        def value(idx, i, j):
            p = jnp.exp2(s_ref[slot, idx] - m[i]).astype(BF16)
            v_aug = jnp.concatenate([vT_ref[h * MLA_V:(h + 1) * MLA_V, j * t:(j + 1) * t],
                                     ones], axis=0)
            pv = _dot(v_aug, p)
            acc[i] = pv if acc[i] is None else acc[i] + pv
            if j == i:
                a = acc[i]
                o = (a[:MLA_V] * (1.0 / a[MLA_V:MLA_V + 1])).T
                rows = slice(i * t, (i + 1) * t)
                cols = slice(h * MLA_V, (h + 1) * MLA_V)
                o_ref[rows, cols] = (o * z_ref[rows, cols]).astype(BF16)

        return ([functools.partial(score, idx, i, j) for idx, (i, j) in enumerate(blocks)],
                [functools.partial(value, idx, i, j) for idx, (i, j) in enumerate(blocks)])

    steps = [head_steps(h, h % 2) for h in range(ATT_HB)]
    for f in steps[0][0]:
        f()
    for h in range(ATT_HB):
        nxt = steps[h + 1][0] if h + 1 < ATT_HB else []
        for n, f in enumerate(steps[h][1]):
            if n < len(nxt):
                nxt[n]()
            f()


def _mla_attention(qT, k, kr, vT, z, batch, seq):
    n = k.shape[0]
    hb = ATT_HB
    nt = seq // ATT_T
    n_blocks = nt * (nt + 1) // 2
    return pl.pallas_call(
        _mla_attn_kernel,
        grid=(batch, MLA_HEADS // hb),
        in_specs=[pl.BlockSpec((hb * QHEAD, seq), lambda b, h: (h, b)),
                  pl.BlockSpec((seq, hb * MLA_NOPE), lambda b, h: (b, h)),
                  pl.BlockSpec((seq, LANES), lambda b, h: (b, 0)),
                  pl.BlockSpec((hb * MLA_V, seq), lambda b, h: (h, b)),
                  pl.BlockSpec((seq, hb * MLA_V), lambda b, h: (b, h))],
        out_specs=pl.BlockSpec((seq, hb * MLA_V), lambda b, h: (b, h)),
        out_shape=jax.ShapeDtypeStruct((n, MLA_HV), BF16),
        scratch_shapes=[pltpu.VMEM((2, n_blocks, ATT_T, ATT_T), F32)],
        compiler_params=_params("arbitrary", "arbitrary"),
        name="mla_attention",
    )(qT, k, kr, vT, z)


def kernel(x, positions, gla_w_in, gla_w_a2, gla_b_a, gla_g_out, gla_w_out,
           mla_w_in, mla_g_q, mla_w_uq, mla_g_kv, mla_w_ukv, mla_w_out, ln_g, ln_b):
    batch, seq, d = x.shape
    assert d == D_MODEL and seq % GLA_TC == 0 and seq % ATT_T == 0
    x2 = x.reshape(batch * seq, d)

    q, kdm, sz, vT, dec = _gla_in_proj(x2, gla_w_in[0], gla_w_a2[0], gla_b_a[0], gla_g_out[0])
    o = _gla_recurrence(q, kdm, vT, sz, dec, batch, seq)
    x2 = _out_proj_ln(o, x2, gla_w_out[0], ln_g[0], ln_b[0])

    tabT, tab = _rope_tables(positions)
    cq, ckv, z, kr = _mla_in_proj(x2, mla_w_in[0], mla_g_q[0], mla_g_kv[0], tab)
    qT, kn, vT = _mla_up_proj(cq, ckv, mla_w_uq[0], mla_w_ukv[0], tabT)
    o = _mla_attention(qT, kn, kr, vT, z, batch, seq)
    x2 = _out_proj_ln(o, x2, mla_w_out[0], ln_g[1], ln_b[1])
    return x2.reshape(batch, seq, d)
```

```python
import functools
import math

import jax
import jax.numpy as jnp
from jax import lax
from jax.experimental import pallas as pl
from jax.experimental.pallas import tpu as pltpu

BF16 = jnp.bfloat16
F32 = jnp.float32

D_MODEL = 2048
DEPTH = 2
CHUNK = 64
DEEPNORM_ALPHA = float((2 * DEPTH) ** 0.25)
LN_EPS = 1e-5
RMS_EPS = 1e-6

GLA_HEADS = 4
GLA_DK = 256
GLA_DV = 512
GLA_HK = GLA_HEADS * GLA_DK
GLA_HV = GLA_HEADS * GLA_DV
GLA_GATE_RANK = 16
GLA_TAU = 16.0

MLA_HEADS = 16
MLA_Q_RANK = 512
MLA_KV_RANK = 512
MLA_NOPE = 128
MLA_ROPE = 64
MLA_V = 128
MLA_QK = MLA_NOPE + MLA_ROPE
MLA_HV = MLA_HEADS * MLA_V
ROPE_THETA = 10000.0

LANES = 128
SUBLANES = 8
BF16_ROWS = 16
VMEM_LIMIT = 56 * 1024 * 1024

PROJ_BM = 256
OUT_BM = 512
OUT_SUB = 128
UP_BM = 512
MLA_IN_BM = 512
GLA_TC = 512
ATT_T = 256
ATT_HB = 4
ROPE_TAB_R = 2048
PREP_BR = 256
WEIGHT_PAD = 2 * LANES
QHEAD = 2 * LANES


def _params(*sem):
    return pltpu.CompilerParams(dimension_semantics=sem, vmem_limit_bytes=VMEM_LIMIT)


def _resident(shape):
    zeros = (0,) * len(shape)
    return pl.BlockSpec(shape, lambda *_: zeros, pipeline_mode=pl.Buffered(1))


def _dot(a, b):
    return jnp.dot(a, b, preferred_element_type=F32)


def _dot_nt(a, b):
    return lax.dot_general(a, b, (((1,), (1,)), ((), ())), preferred_element_type=F32)


def _silu(z):
    return z * (1.0 / (1.0 + jnp.exp(-z)))


def _rms_rows(x, g):
    ms = jnp.mean(x * x, axis=-1, keepdims=True)
    return x * lax.rsqrt(ms + RMS_EPS) * g


def _prep_transpose_kernel(*refs, n_out):
    *w_refs, o_ref = refs

    @pl.when(pl.program_id(0) < n_out)
    def _():
        rows = [r[...] for r in w_refs]
        w = rows[0] if len(rows) == 1 else jnp.concatenate(rows, axis=0)
        o_ref[...] = w.T.astype(BF16)

    @pl.when(pl.program_id(0) >= n_out)
    def _():
        o_ref[...] = jnp.zeros(o_ref.shape, BF16)


def _prep_transposed(wT, n_out, br, parts, first_block):
    d = wT.shape[1]
    assert br * parts == WEIGHT_PAD
    src_block = lambda k, j: (first_block(jnp.minimum(j, n_out - 1)) + k, 0)
    return pl.pallas_call(
        functools.partial(_prep_transpose_kernel, n_out=n_out),
        grid=(n_out + 1,),
        in_specs=[pl.BlockSpec((br, d), functools.partial(src_block, k)) for k in range(parts)],
        out_specs=pl.BlockSpec((d, br * parts), lambda j: (0, j)),
        out_shape=jax.ShapeDtypeStruct((d, (n_out + 1) * br * parts), BF16),
        compiler_params=_params("arbitrary"),
        name="prep_transposed",
    )(*([wT] * parts))


def _prep_rows_kernel(w_ref, o_ref):
    d = w_ref.shape[1]
    o_ref[:, :d] = w_ref[...].astype(BF16)
    if o_ref.shape[1] > d:
        o_ref[:, d:] = jnp.zeros((o_ref.shape[0], o_ref.shape[1] - d), BF16)


def _prep_rows(wT, block0, n_blocks, br, pad=0):
    d = wT.shape[1]
    return pl.pallas_call(
        _prep_rows_kernel,
        grid=(n_blocks,),
        in_specs=[pl.BlockSpec((br, d), lambda j: (block0 + j, 0))],
        out_specs=pl.BlockSpec((br, d + pad), lambda j: (j, 0)),
        out_shape=jax.ShapeDtypeStruct((n_blocks * br, d + pad), BF16),
        compiler_params=_params("arbitrary"),
        name="prep_rows",
    )(wT)


def _rope_tab_kernel(pos_ref, freq_ref, tabT_ref, tab_ref):
    pos = pos_ref[...].astype(F32)
    ang = freq_ref[...] * pos
    c = jnp.cos(ang)
    s = jnp.sin(ang)
    t = jnp.concatenate([c, c, -s, s], axis=0)
    tabT_ref[...] = t
    tab_ref[...] = t.T


def _rope_tables(positions):
    n = positions.size
    half = MLA_ROPE // 2
    inv_freq = ROPE_THETA ** (-jnp.arange(half, dtype=F32) / half)
    freq_col = inv_freq.reshape(half, 1)
    pos_row = positions.reshape(1, n)
    return pl.pallas_call(
        _rope_tab_kernel,
        grid=(n // ROPE_TAB_R,),
        in_specs=[pl.BlockSpec((1, ROPE_TAB_R), lambda i: (0, i)),
                  pl.BlockSpec((half, 1), lambda i: (0, 0))],
        out_specs=[pl.BlockSpec((LANES, ROPE_TAB_R), lambda i: (0, i)),
                   pl.BlockSpec((ROPE_TAB_R, LANES), lambda i: (i, 0))],
        out_shape=[jax.ShapeDtypeStruct((LANES, n), F32),
                   jax.ShapeDtypeStruct((n, LANES), F32)],
        compiler_params=_params("arbitrary"),
        name="rope_tables",
    )(pos_row, freq_col)


def _chunk_cumsum(la):
    t, n = la.shape
    x = la.reshape(t // SUBLANES, SUBLANES, n)
    row = lax.broadcasted_iota(jnp.int32, x.shape, 1)
    s = 1
    while s < SUBLANES:
        x = x + jnp.where(row >= s, pltpu.roll(x, s, axis=1), 0.0)
        s *= 2
    tiles_per_chunk = CHUNK // SUBLANES
    out = []
    carry = None
    for i in range(t // SUBLANES):
        tile = x[i]
        if i % tiles_per_chunk != 0:
            tile = tile + carry
        carry = tile[SUBLANES - 1:SUBLANES, :]
        out.append(tile)
    return jnp.concatenate(out, axis=0)


def _gla_in_kernel(x_ref, w_ref, wvT_ref, wa_ref, wa2_ref, ba_ref, g_ref,
                   q_ref, kdm_ref, sz_ref, vT_ref, dec_ref):
    bm = x_ref.shape[0]
    z0 = 2 * GLA_HK + GLA_HV
    xb = x_ref[...].astype(BF16)
    q_ref[...] = _dot(xb, w_ref[:, :GLA_HK]).astype(BF16)
    z = _dot(xb, w_ref[:, z0:z0 + GLA_HV])
    sz_ref[...] = (_silu(z) * g_ref[...]).astype(BF16)
    vT_ref[...] = _dot_nt(wvT_ref[:, :D_MODEL], xb).astype(BF16)

    a = _dot_nt(xb, wa_ref[...]).astype(BF16)
    pre = _dot(a, wa2_ref[...]) + ba_ref[...]
    la = (jnp.minimum(pre, 0.0) - jnp.log(1.0 + jnp.exp(-jnp.abs(pre)))) * (1.0 / GLA_TAU)
    cum = _chunk_cumsum(la)
    chunks = bm // CHUNK
    tots = [cum[(c + 1) * CHUNK - 1:(c + 1) * CHUNK, :] for c in range(chunks)]
    dec_ref[0] = jnp.exp(jnp.concatenate(tots, axis=0))
    tot_rows = jnp.concatenate([jnp.broadcast_to(t, (CHUNK, GLA_HK)) for t in tots], axis=0)
    kd = (_dot(xb, w_ref[:, GLA_HK:2 * GLA_HK]) * jnp.exp(tot_rows - cum)).astype(BF16)

    zeros = jnp.zeros((CHUNK, GLA_DK), BF16)
    for c in range(chunks):
        rows = slice(c * CHUNK, (c + 1) * CHUNK)
        for h in range(GLA_HEADS):
            c0 = h * 2 * GLA_DK
            mine = slice(c0 + (c % 2) * GLA_DK, c0 + (c % 2 + 1) * GLA_DK)
            other = slice(c0 + (1 - c % 2) * GLA_DK, c0 + (2 - c % 2) * GLA_DK)
            kdm_ref[rows, mine] = kd[rows, h * GLA_DK:(h + 1) * GLA_DK]
            kdm_ref[rows, other] = zeros


def _gla_in_proj(x2, w_in, w_a2, b_a, g_out):
    n = x2.shape[0]
    bm = PROJ_BM
    br = PREP_BR
    wT = jnp.swapaxes(w_in, 0, 1)
    a0 = 2 * GLA_HK + 2 * GLA_HV
    wb = _prep_transposed(wT, a0 // br, br, 1, lambda j: j)
    wvT = _prep_rows(wT, 2 * GLA_HK // br, GLA_HV // br, br, WEIGHT_PAD)
    waT = jnp.pad(_prep_rows(wT, a0 // GLA_GATE_RANK, 1, GLA_GATE_RANK),
                  ((0, LANES - GLA_GATE_RANK), (0, 0)))
    wa2_p = jnp.pad(w_a2, ((0, LANES - GLA_GATE_RANK), (0, 0))).astype(BF16)
    row = lambda i: (i, 0)
    return pl.pallas_call(
        _gla_in_kernel,
        grid=(n // bm,),
        in_specs=[pl.BlockSpec((bm, D_MODEL), row),
                  _resident(wb.shape),
                  _resident(wvT.shape),
                  _resident((LANES, D_MODEL)),
                  _resident((LANES, GLA_HK)),
                  _resident((1, GLA_HK)),
                  _resident((1, GLA_HV))],
        out_specs=[pl.BlockSpec((bm, GLA_HK), row),
                   pl.BlockSpec((bm, 2 * GLA_HK), row),
                   pl.BlockSpec((bm, GLA_HV), row),
                   pl.BlockSpec((GLA_HV, bm), lambda i: (0, i)),
                   pl.BlockSpec((1, bm // CHUNK, GLA_HK), lambda i: (i, 0, 0))],
        out_shape=[jax.ShapeDtypeStruct((n, GLA_HK), BF16),
                   jax.ShapeDtypeStruct((n, 2 * GLA_HK), BF16),
                   jax.ShapeDtypeStruct((n, GLA_HV), BF16),
                   jax.ShapeDtypeStruct((GLA_HV, n), BF16),
                   jax.ShapeDtypeStruct((n // bm, bm // CHUNK, GLA_HK), F32)],
        compiler_params=_params("arbitrary"),
        name="gla_in_proj",
    )(x2, wb, wvT, waT, wa2_p, b_a.reshape(1, GLA_HK), g_out.reshape(1, GLA_HV))


def _gla_rec_kernel(q_ref, kdm_ref, vT_ref, sz_ref, dec_ref, o_ref, st_ref):
    @pl.when(pl.program_id(1) == 0)
    def _():
        st_ref[...] = jnp.zeros_like(st_ref)

    pair = 2 * CHUNK
    chunks_per_block = dec_ref.shape[1]
    eps = RMS_EPS * GLA_DK
    units = [(p, h) for p in range(GLA_TC // pair) for h in range(GLA_HEADS)]

    def pair_update(p, h):
        return _dot(vT_ref[h * GLA_DV:(h + 1) * GLA_DV, p * pair:(p + 1) * pair],
                    kdm_ref[p * pair:(p + 1) * pair, h * 2 * GLA_DK:(h + 1) * 2 * GLA_DK])

    kv_next = pair_update(*units[0])
    for n, (p, h) in enumerate(units):
        kv2 = kv_next
        if n + 1 < len(units):
            kv_next = pair_update(*units[n + 1])
        ks = slice(h * GLA_DK, (h + 1) * GLA_DK)
        vs = slice(h * GLA_DV, (h + 1) * GLA_DV)
        for e in range(2):
            c = 2 * p + e
            rows = slice(c * CHUNK, (c + 1) * CHUNK)
            dec = dec_ref[c // chunks_per_block, c % chunks_per_block:c % chunks_per_block + 1, ks]
            s_new = st_ref[h] * dec + kv2[:, e * GLA_DK:(e + 1) * GLA_DK]
            st_ref[h] = s_new
            o = _dot_nt(q_ref[rows, ks], s_new.astype(BF16))
            ms = jnp.mean(o * o, axis=-1, keepdims=True)
            o_ref[rows, vs] = (o * lax.rsqrt(ms + eps) * sz_ref[rows, vs]).astype(BF16)


def _gla_recurrence(q, kdm, vT, sz, dec, batch, seq):
    n = q.shape[0]
    tc = GLA_TC
    steps = seq // tc
    blocks = tc // PROJ_BM
    row = lambda b, t: (b * steps + t, 0)
    return pl.pallas_call(
        _gla_rec_kernel,
        grid=(batch, steps),
        in_specs=[pl.BlockSpec((tc, GLA_HK), row),
                  pl.BlockSpec((tc, 2 * GLA_HK), row),
                  pl.BlockSpec((GLA_HV, tc), lambda b, t: (0, b * steps + t)),
                  pl.BlockSpec((tc, GLA_HV), row),
                  pl.BlockSpec((blocks, dec.shape[1], GLA_HK), lambda b, t: (b * steps + t, 0, 0))],
        out_specs=pl.BlockSpec((tc, GLA_HV), row),
        out_shape=jax.ShapeDtypeStruct((n, GLA_HV), BF16),
        scratch_shapes=[pltpu.VMEM((GLA_HEADS, GLA_DV, GLA_DK), F32)],
        compiler_params=_params("arbitrary", "arbitrary"),
        name="gla_recurrence",
    )(q, kdm, vT, sz, dec)


def _out_ln_kernel(o_ref, x_ref, w_ref, g_ref, b_ref, y_ref):
    sub = OUT_SUB
    nsub = o_ref.shape[0] // sub

    def proj(r):
        return _dot(o_ref[r * sub:(r + 1) * sub, :], w_ref[...])

    y_next = proj(0)
    for r in range(nsub):
        rows = slice(r * sub, (r + 1) * sub)
        y = y_next
        if r + 1 < nsub:
            y_next = proj(r + 1)
        v = DEEPNORM_ALPHA * x_ref[rows, :] + y
        mu = jnp.mean(v, axis=-1, keepdims=True)
        vc = v - mu
        var = jnp.mean(vc * vc, axis=-1, keepdims=True)
        y_ref[rows, :] = vc * lax.rsqrt(var + LN_EPS) * g_ref[...] + b_ref[...]


def _out_proj_ln(o, x2, w_out, g, b):
    n = x2.shape[0]
    bm = OUT_BM
    return pl.pallas_call(
        _out_ln_kernel,
        grid=(n // bm,),
        in_specs=[pl.BlockSpec((bm, o.shape[1]), lambda i: (i, 0)),
                  pl.BlockSpec((bm, D_MODEL), lambda i: (i, 0)),
                  _resident((o.shape[1], D_MODEL)),
                  _resident((1, D_MODEL)),
                  _resident((1, D_MODEL))],
        out_specs=pl.BlockSpec((bm, D_MODEL), lambda i: (i, 0)),
        out_shape=jax.ShapeDtypeStruct((n, D_MODEL), F32),
        compiler_params=_params("arbitrary"),
        name="out_proj_ln",
    )(o, x2, w_out.astype(BF16), g.reshape(1, D_MODEL), b.reshape(1, D_MODEL))


def _mla_in_kernel(x_ref, wc_ref, wz_ref, wr_ref, gq_ref, gkv_ref, tab_ref,
                   cq_ref, ckv_ref, z_ref, kr_ref):
    xb = x_ref[...].astype(BF16)
    c = _dot(xb, wc_ref[:, :MLA_Q_RANK + MLA_KV_RANK])
    cq_ref[...] = _rms_rows(c[:, :MLA_Q_RANK], gq_ref[...]).astype(BF16)
    ckv_ref[...] = _rms_rows(c[:, MLA_Q_RANK:], gkv_ref[...]).astype(BF16)
    z_ref[...] = _silu(_dot(xb, wz_ref[:, :MLA_HV])).astype(BF16)
    a = _dot_nt(xb, wr_ref[...]) * tab_ref[...]
    kr_ref[...] = (a + pltpu.roll(a, LANES // 2, axis=1)).astype(BF16)


def _mla_in_proj(x2, w_in, g_q, g_kv, tab):
    n = x2.shape[0]
    bm = MLA_IN_BM
    nc = MLA_Q_RANK + MLA_KV_RANK
    half = MLA_ROPE // 2
    wT = jnp.swapaxes(w_in, 0, 1)
    wc = _prep_transposed(wT, nc // PREP_BR, PREP_BR, 1, lambda j: j)
    wz = _prep_transposed(wT, MLA_HV // (4 * MLA_ROPE), MLA_ROPE, 4,
                          lambda j: nc // MLA_ROPE + 1 + 4 * j)
    wr = _prep_rows(wT, nc // MLA_ROPE, 1, MLA_ROPE)
    t1, t2 = wr[:half], wr[half:]
    wr4T = jnp.concatenate([t1, t2, t2, t1], axis=0)
    row = lambda i: (i, 0)
    return pl.pallas_call(
        _mla_in_kernel,
        grid=(n // bm,),
        in_specs=[pl.BlockSpec((bm, D_MODEL), row),
                  _resident(wc.shape),
                  _resident(wz.shape),
                  _resident((LANES, D_MODEL)),
                  _resident((1, MLA_Q_RANK)),
                  _resident((1, MLA_KV_RANK)),
                  pl.BlockSpec((bm, LANES), row)],
        out_specs=[pl.BlockSpec((bm, MLA_Q_RANK), row),
                   pl.BlockSpec((bm, MLA_KV_RANK), row),
                   pl.BlockSpec((bm, MLA_HV), row),
                   pl.BlockSpec((bm, LANES), row)],
        out_shape=[jax.ShapeDtypeStruct((n, MLA_Q_RANK), BF16),
                   jax.ShapeDtypeStruct((n, MLA_KV_RANK), BF16),
                   jax.ShapeDtypeStruct((n, MLA_HV), BF16),
                   jax.ShapeDtypeStruct((n, LANES), BF16)],
        compiler_params=_params("arbitrary"),
        name="mla_in_proj",
    )(x2, wc, wz, wr4T, g_q.reshape(1, -1), g_kv.reshape(1, -1), tab)


def _mla_up_kernel(cq_ref, ckv_ref, wqT_ref, wk_ref, wvT_ref, tabT_ref,
                   qT_ref, k_ref, vT_ref):
    cq = cq_ref[...]
    ckv = ckv_ref[...]
    scale = MLA_QK ** -0.5 * math.log2(math.e)
    tab = tabT_ref[...] * scale
    group = 4 * QHEAD
    for g0 in range(0, MLA_HEADS * QHEAD, group):
        acc = _dot_nt(wqT_ref[g0:g0 + group, :], cq)
        for h0 in range(0, group, QHEAD):
            r0 = g0 + h0
            qT_ref[r0:r0 + LANES, :] = (acc[h0:h0 + LANES] * scale).astype(BF16)
            qT_ref[r0 + LANES:r0 + QHEAD, :] = (acc[h0 + LANES:h0 + QHEAD] * tab).astype(BF16)
    k_ref[...] = _dot(ckv, wk_ref[...]).astype(BF16)
    vT_ref[...] = _dot_nt(wvT_ref[...], ckv).astype(BF16)


def _mla_up_proj(cq, ckv, w_uq, w_ukv, tabT):
    n = cq.shape[0]
    bm = UP_BM
    half = MLA_ROPE // 2
    wq = w_uq.astype(BF16).reshape(MLA_Q_RANK, MLA_HEADS, MLA_QK)
    nope = wq[:, :, :MLA_NOPE]
    t1 = wq[:, :, MLA_NOPE:MLA_NOPE + half]
    t2 = wq[:, :, MLA_NOPE + half:]
    wq4 = jnp.concatenate([nope, t1, t2, t2, t1], axis=2)
    wqT = wq4.reshape(MLA_Q_RANK, MLA_HEADS * QHEAD).T
    wkv = w_ukv.astype(BF16).reshape(MLA_KV_RANK, MLA_HEADS, MLA_NOPE + MLA_V)
    wk = wkv[:, :, :MLA_NOPE].reshape(MLA_KV_RANK, MLA_HEADS * MLA_NOPE)
    wvT = wkv[:, :, MLA_NOPE:].reshape(MLA_KV_RANK, MLA_HV).T
    row = lambda i: (i, 0)
    col = lambda i: (0, i)
    return pl.pallas_call(
        _mla_up_kernel,
        grid=(n // bm,),
        in_specs=[pl.BlockSpec((bm, MLA_Q_RANK), row),
                  pl.BlockSpec((bm, MLA_KV_RANK), row),
                  _resident(wqT.shape),
                  _resident(wk.shape),
                  _resident(wvT.shape),
                  pl.BlockSpec((LANES, bm), col)],
        out_specs=[pl.BlockSpec((MLA_HEADS * QHEAD, bm), col),
                   pl.BlockSpec((bm, MLA_HEADS * MLA_NOPE), row),
                   pl.BlockSpec((MLA_HV, bm), col)],
        out_shape=[jax.ShapeDtypeStruct((MLA_HEADS * QHEAD, n), BF16),
                   jax.ShapeDtypeStruct((n, MLA_HEADS * MLA_NOPE), BF16),
                   jax.ShapeDtypeStruct((MLA_HV, n), BF16)],
        compiler_params=_params("arbitrary"),
        name="mla_up_proj",
    )(cq, ckv, wqT, wk, wvT, tabT)


def _mla_attn_kernel(qT_ref, k_ref, kr_ref, vT_ref, z_ref, o_ref, s_ref):
    t = ATT_T
    nt = qT_ref.shape[1] // t
    kk = lax.broadcasted_iota(jnp.int32, (t, t), 0) // CHUNK
    qq = lax.broadcasted_iota(jnp.int32, (t, t), 1) // CHUNK
    visible = kk <= qq
    ones = jnp.ones((BF16_ROWS, t), BF16)
    blocks = [(i, j) for i in range(nt) for j in range(i + 1)]

    def head_steps(h, slot):
        m = [None] * nt
        acc = [None] * nt

        def score(idx, i, j):
            ks = slice(j * t, (j + 1) * t)
            kcat = jnp.concatenate([k_ref[ks, h * MLA_NOPE:(h + 1) * MLA_NOPE],
                                    kr_ref[ks, :]], axis=1)
            s = _dot(kcat, qT_ref[h * QHEAD:(h + 1) * QHEAD, i * t:(i + 1) * t])
            if j == i:
                s = jnp.where(visible, s, -jnp.inf)
            m_blk = jnp.max(s, axis=0, keepdims=True)
            m[i] = m_blk if m[i] is None else jnp.maximum(m[i], m_blk)
            s_ref[slot, idx] = s

        def value(idx, i, j):
            p = jnp.exp2(s_ref[slot, idx] - m[i]).astype(BF16)
            v_aug = jnp.concatenate([vT_ref[h * MLA_V:(h + 1) * MLA_V, j * t:(j + 1) * t],
                                     ones], axis=0)
            pv = _dot(v_aug, p)
            acc[i] = pv if acc[i] is None else acc[i] + pv
            if j == i:
                a = acc[i]
                o = (a[:MLA_V] * (1.0 / a[MLA_V:MLA_V + 1])).T
                rows = slice(i * t, (i + 1) * t)
                cols = slice(h * MLA_V, (h + 1) * MLA_V)
                o_ref[rows, cols] = (o * z_ref[rows, cols]).astype(BF16)

        return ([functools.partial(score, idx, i, j) for idx, (i, j) in enumerate(blocks)],
                [functools.partial(value, idx, i, j) for idx, (i, j) in enumerate(blocks)])

    steps = [head_steps(h, h % 2) for h in range(ATT_HB)]
    for f in steps[0][0]:
        f()
    for h in range(ATT_HB):
        nxt = steps[h + 1][0] if h + 1 < ATT_HB else []
        for n, f in enumerate(steps[h][1]):
            if n < len(nxt):
                nxt[n]()
            f()


def _mla_attention(qT, k, kr, vT, z, batch, seq):
    n = k.shape[0]
    hb = ATT_HB
    nt = seq // ATT_T
    n_blocks = nt * (nt + 1) // 2
    return pl.pallas_call(
        _mla_attn_kernel,
        grid=(batch, MLA_HEADS // hb),
        in_specs=[pl.BlockSpec((hb * QHEAD, seq), lambda b, h: (h, b)),
                  pl.BlockSpec((seq, hb * MLA_NOPE), lambda b, h: (b, h)),
                  pl.BlockSpec((seq, LANES), lambda b, h: (b, 0)),
                  pl.BlockSpec((hb * MLA_V, seq), lambda b, h: (h, b)),
                  pl.BlockSpec((seq, hb * MLA_V), lambda b, h: (b, h))],
        out_specs=pl.BlockSpec((seq, hb * MLA_V), lambda b, h: (b, h)),
        out_shape=jax.ShapeDtypeStruct((n, MLA_HV), BF16),
        scratch_shapes=[pltpu.VMEM((2, n_blocks, ATT_T, ATT_T), F32)],
        compiler_params=_params("arbitrary", "arbitrary"),
        name="mla_attention",
    )(qT, k, kr, vT, z)


def kernel(x, positions, gla_w_in, gla_w_a2, gla_b_a, gla_g_out, gla_w_out,
           mla_w_in, mla_g_q, mla_w_uq, mla_g_kv, mla_w_ukv, mla_w_out, ln_g, ln_b):
    batch, seq, d = x.shape
    assert d == D_MODEL and seq % GLA_TC == 0 and seq % ATT_T == 0
    x2 = x.reshape(batch * seq, d)

    q, kdm, sz, vT, dec = _gla_in_proj(x2, gla_w_in[0], gla_w_a2[0], gla_b_a[0], gla_g_out[0])
    o = _gla_recurrence(q, kdm, vT, sz, dec, batch, seq)
    x2 = _out_proj_ln(o, x2, gla_w_out[0], ln_g[0], ln_b[0])

    tabT, tab = _rope_tables(positions)
    cq, ckv, z, kr = _mla_in_proj(x2, mla_w_in[0], mla_g_q[0], mla_g_kv[0], tab)
    qT, kn, vT = _mla_up_proj(cq, ckv, mla_w_uq[0], mla_w_ukv[0], tabT)
    o = _mla_attention(qT, kn, kr, vT, z, batch, seq)
    x2 = _out_proj_ln(o, x2, mla_w_out[0], ln_g[1], ln_b[1])
    return x2.reshape(batch, seq, d)
```

```python
import functools
import math

import jax
import jax.numpy as jnp
from jax import lax
from jax.experimental import pallas as pl
from jax.experimental.pallas import tpu as pltpu

BF16 = jnp.bfloat16
F32 = jnp.float32

D_MODEL = 2048
DEPTH = 2
CHUNK = 64
DEEPNORM_ALPHA = float((2 * DEPTH) ** 0.25)
LN_EPS = 1e-5
RMS_EPS = 1e-6

GLA_HEADS = 4
GLA_DK = 256
GLA_DV = 512
GLA_HK = GLA_HEADS * GLA_DK
GLA_HV = GLA_HEADS * GLA_DV
GLA_GATE_RANK = 16
GLA_TAU = 16.0

MLA_HEADS = 16
MLA_Q_RANK = 512
MLA_KV_RANK = 512
MLA_NOPE = 128
MLA_ROPE = 64
MLA_V = 128
MLA_QK = MLA_NOPE + MLA_ROPE
MLA_HV = MLA_HEADS * MLA_V
ROPE_THETA = 10000.0

LANES = 128
SUBLANES = 8
BF16_ROWS = 16
VMEM_LIMIT = 56 * 1024 * 1024

PROJ_BM = 256
OUT_BM = 512
OUT_SUB = 128
UP_BM = 512
MLA_IN_BM = 512
GLA_TC = 512
ATT_T = 256
ATT_HB = 4
ROPE_TAB_R = 2048
PREP_BR = 256
WEIGHT_PAD = 2 * LANES
QHEAD = 2 * LANES


def _params(*sem):
    return pltpu.CompilerParams(dimension_semantics=sem, vmem_limit_bytes=VMEM_LIMIT)


def _resident(shape):
    zeros = (0,) * len(shape)
    return pl.BlockSpec(shape, lambda *_: zeros, pipeline_mode=pl.Buffered(1))


def _dot(a, b):
    return jnp.dot(a, b, preferred_element_type=F32)


def _dot_nt(a, b):
    return lax.dot_general(a, b, (((1,), (1,)), ((), ())), preferred_element_type=F32)


def _silu(z):
    return z * (1.0 / (1.0 + jnp.exp(-z)))


def _rms_rows(x, g):
    ms = jnp.mean(x * x, axis=-1, keepdims=True)
    return x * lax.rsqrt(ms + RMS_EPS) * g


def _prep_transpose_kernel(*refs, n_out):
    *w_refs, o_ref = refs

    @pl.when(pl.program_id(0) < n_out)
    def _():
        rows = [r[...] for r in w_refs]
        w = rows[0] if len(rows) == 1 else jnp.concatenate(rows, axis=0)
        o_ref[...] = w.T.astype(BF16)

    @pl.when(pl.program_id(0) >= n_out)
    def _():
        o_ref[...] = jnp.zeros(o_ref.shape, BF16)


def _prep_transposed(wT, n_out, br, parts, first_block):
    d = wT.shape[1]
    assert br * parts == WEIGHT_PAD
    src_block = lambda k, j: (first_block(jnp.minimum(j, n_out - 1)) + k, 0)
    return pl.pallas_call(
        functools.partial(_prep_transpose_kernel, n_out=n_out),
        grid=(n_out + 1,),
        in_specs=[pl.BlockSpec((br, d), functools.partial(src_block, k)) for k in range(parts)],
        out_specs=pl.BlockSpec((d, br * parts), lambda j: (0, j)),
        out_shape=jax.ShapeDtypeStruct((d, (n_out + 1) * br * parts), BF16),
        compiler_params=_params("arbitrary"),
        name="prep_transposed",
    )(*([wT] * parts))


def _prep_rows_kernel(w_ref, o_ref):
    d = w_ref.shape[1]
    o_ref[:, :d] = w_ref[...].astype(BF16)
    if o_ref.shape[1] > d:
        o_ref[:, d:] = jnp.zeros((o_ref.shape[0], o_ref.shape[1] - d), BF16)


def _prep_rows(wT, block0, n_blocks, br, pad=0):
    d = wT.shape[1]
    return pl.pallas_call(
        _prep_rows_kernel,
        grid=(n_blocks,),
        in_specs=[pl.BlockSpec((br, d), lambda j: (block0 + j, 0))],
        out_specs=pl.BlockSpec((br, d + pad), lambda j: (j, 0)),
        out_shape=jax.ShapeDtypeStruct((n_blocks * br, d + pad), BF16),
        compiler_params=_params("arbitrary"),
        name="prep_rows",
    )(wT)


def _rope_tab_kernel(pos_ref, freq_ref, tabT_ref, tab_ref):
    pos = pos_ref[...].astype(F32)
    ang = freq_ref[...] * pos
    c = jnp.cos(ang)
    s = jnp.sin(ang)
    t = jnp.concatenate([c, c, -s, s], axis=0)
    tabT_ref[...] = t
    tab_ref[...] = t.T


def _rope_tables(positions):
    n = positions.size
    half = MLA_ROPE // 2
    inv_freq = ROPE_THETA ** (-jnp.arange(half, dtype=F32) / half)
    freq_col = inv_freq.reshape(half, 1)
    pos_row = positions.reshape(1, n)
    return pl.pallas_call(
        _rope_tab_kernel,
        grid=(n // ROPE_TAB_R,),
        in_specs=[pl.BlockSpec((1, ROPE_TAB_R), lambda i: (0, i)),
                  pl.BlockSpec((half, 1), lambda i: (0, 0))],
        out_specs=[pl.BlockSpec((LANES, ROPE_TAB_R), lambda i: (0, i)),
                   pl.BlockSpec((ROPE_TAB_R, LANES), lambda i: (i, 0))],
        out_shape=[jax.ShapeDtypeStruct((LANES, n), F32),
                   jax.ShapeDtypeStruct((n, LANES), F32)],
        compiler_params=_params("arbitrary"),
        name="rope_tables",
    )(pos_row, freq_col)


def _chunk_cumsum(la):
    t, n = la.shape
    x = la.reshape(t // SUBLANES, SUBLANES, n)
    row = lax.broadcasted_iota(jnp.int32, x.shape, 1)
    s = 1
    while s < SUBLANES:
        x = x + jnp.where(row >= s, pltpu.roll(x, s, axis=1), 0.0)
        s *= 2
    tiles_per_chunk = CHUNK // SUBLANES
    out = []
    carry = None
    for i in range(t // SUBLANES):
        tile = x[i]
        if i % tiles_per_chunk != 0:
            tile = tile + carry
        carry = tile[SUBLANES - 1:SUBLANES, :]
        out.append(tile)
    return jnp.concatenate(out, axis=0)


def _gla_in_kernel(x_ref, w_ref, wvT_ref, wa_ref, wa2_ref, ba_ref, g_ref,
                   q_ref, kdm_ref, sz_ref, vT_ref, dec_ref):
    bm = x_ref.shape[0]
    z0 = 2 * GLA_HK + GLA_HV
    xb = x_ref[...].astype(BF16)
    q_ref[...] = _dot(xb, w_ref[:, :GLA_HK]).astype(BF16)
    z = _dot(xb, w_ref[:, z0:z0 + GLA_HV])
    sz_ref[...] = (_silu(z) * g_ref[...]).astype(BF16)
    vT_ref[...] = _dot_nt(wvT_ref[:, :D_MODEL], xb).astype(BF16)

    a = _dot_nt(xb, wa_ref[...]).astype(BF16)
    pre = _dot(a, wa2_ref[...]) + ba_ref[...]
    la = (jnp.minimum(pre, 0.0) - jnp.log(1.0 + jnp.exp(-jnp.abs(pre)))) * (1.0 / GLA_TAU)
    cum = _chunk_cumsum(la)
    chunks = bm // CHUNK
    tots = [cum[(c + 1) * CHUNK - 1:(c + 1) * CHUNK, :] for c in range(chunks)]
    dec_ref[0] = jnp.exp(jnp.concatenate(tots, axis=0))
    tot_rows = jnp.concatenate([jnp.broadcast_to(t, (CHUNK, GLA_HK)) for t in tots], axis=0)
    kd = (_dot(xb, w_ref[:, GLA_HK:2 * GLA_HK]) * jnp.exp(tot_rows - cum)).astype(BF16)

    zeros = jnp.zeros((CHUNK, GLA_DK), BF16)
    for c in range(chunks):
        rows = slice(c * CHUNK, (c + 1) * CHUNK)
        for h in range(GLA_HEADS):
            c0 = h * 2 * GLA_DK
            mine = slice(c0 + (c % 2) * GLA_DK, c0 + (c % 2 + 1) * GLA_DK)
            other = slice(c0 + (1 - c % 2) * GLA_DK, c0 + (2 - c % 2) * GLA_DK)
            kdm_ref[rows, mine] = kd[rows, h * GLA_DK:(h + 1) * GLA_DK]
            kdm_ref[rows, other] = zeros


def _gla_in_proj(x2, w_in, w_a2, b_a, g_out):
    n = x2.shape[0]
    bm = PROJ_BM
    br = PREP_BR
    wT = jnp.swapaxes(w_in, 0, 1)
    a0 = 2 * GLA_HK + 2 * GLA_HV
    wb = _prep_transposed(wT, a0 // br, br, 1, lambda j: j)
    wvT = _prep_rows(wT, 2 * GLA_HK // br, GLA_HV // br, br, WEIGHT_PAD)
    waT = jnp.pad(_prep_rows(wT, a0 // GLA_GATE_RANK, 1, GLA_GATE_RANK),
                  ((0, LANES - GLA_GATE_RANK), (0, 0)))
    wa2_p = jnp.pad(w_a2, ((0, LANES - GLA_GATE_RANK), (0, 0))).astype(BF16)
    row = lambda i: (i, 0)
    return pl.pallas_call(
        _gla_in_kernel,
        grid=(n // bm,),
        in_specs=[pl.BlockSpec((bm, D_MODEL), row),
                  _resident(wb.shape),
                  _resident(wvT.shape),
                  _resident((LANES, D_MODEL)),
                  _resident((LANES, GLA_HK)),
                  _resident((1, GLA_HK)),
                  _resident((1, GLA_HV))],
        out_specs=[pl.BlockSpec((bm, GLA_HK), row),
                   pl.BlockSpec((bm, 2 * GLA_HK), row),
                   pl.BlockSpec((bm, GLA_HV), row),
                   pl.BlockSpec((GLA_HV, bm), lambda i: (0, i)),
                   pl.BlockSpec((1, bm // CHUNK, GLA_HK), lambda i: (i, 0, 0))],
        out_shape=[jax.ShapeDtypeStruct((n, GLA_HK), BF16),
                   jax.ShapeDtypeStruct((n, 2 * GLA_HK), BF16),
                   jax.ShapeDtypeStruct((n, GLA_HV), BF16),
                   jax.ShapeDtypeStruct((GLA_HV, n), BF16),
                   jax.ShapeDtypeStruct((n // bm, bm // CHUNK, GLA_HK), F32)],
        compiler_params=_params("arbitrary"),
        name="gla_in_proj",
    )(x2, wb, wvT, waT, wa2_p, b_a.reshape(1, GLA_HK), g_out.reshape(1, GLA_HV))


def _gla_rec_kernel(q_ref, kdm_ref, vT_ref, sz_ref, dec_ref, o_ref, st_ref):
    @pl.when(pl.program_id(1) == 0)
    def _():
        st_ref[...] = jnp.zeros_like(st_ref)

    pair = 2 * CHUNK
    chunks_per_block = dec_ref.shape[1]
    eps = RMS_EPS * GLA_DK
    units = [(p, h) for p in range(GLA_TC // pair) for h in range(GLA_HEADS)]

    def pair_update(p, h):
        return _dot(vT_ref[h * GLA_DV:(h + 1) * GLA_DV, p * pair:(p + 1) * pair],
                    kdm_ref[p * pair:(p + 1) * pair, h * 2 * GLA_DK:(h + 1) * 2 * GLA_DK])

    kv_next = pair_update(*units[0])
    for n, (p, h) in enumerate(units):
        kv2 = kv_next
        if n + 1 < len(units):
            kv_next = pair_update(*units[n + 1])
        ks = slice(h * GLA_DK, (h + 1) * GLA_DK)
        vs = slice(h * GLA_DV, (h + 1) * GLA_DV)
        for e in range(2):
            c = 2 * p + e
            rows = slice(c * CHUNK, (c + 1) * CHUNK)
            dec = dec_ref[c // chunks_per_block, c % chunks_per_block:c % chunks_per_block + 1, ks]
            s_new = st_ref[h] * dec + kv2[:, e * GLA_DK:(e + 1) * GLA_DK]
            st_ref[h] = s_new
            o = _dot_nt(q_ref[rows, ks], s_new.astype(BF16))
            ms = jnp.mean(o * o, axis=-1, keepdims=True)
            o_ref[rows, vs] = (o * lax.rsqrt(ms + eps) * sz_ref[rows, vs]).astype(BF16)


def _gla_recurrence(q, kdm, vT, sz, dec, batch, seq):
    n = q.shape[0]
    tc = GLA_TC
    steps = seq // tc
    blocks = tc // PROJ_BM
    row = lambda b, t: (b * steps + t, 0)
    return pl.pallas_call(
        _gla_rec_kernel,
        grid=(batch, steps),
        in_specs=[pl.BlockSpec((tc, GLA_HK), row),
                  pl.BlockSpec((tc, 2 * GLA_HK), row),
                  pl.BlockSpec((GLA_HV, tc), lambda b, t: (0, b * steps + t)),
                  pl.BlockSpec((tc, GLA_HV), row),
                  pl.BlockSpec((blocks, dec.shape[1], GLA_HK), lambda b, t: (b * steps + t, 0, 0))],
        out_specs=pl.BlockSpec((tc, GLA_HV), row),
        out_shape=jax.ShapeDtypeStruct((n, GLA_HV), BF16),
        scratch_shapes=[pltpu.VMEM((GLA_HEADS, GLA_DV, GLA_DK), F32)],
        compiler_params=_params("arbitrary", "arbitrary"),
        name="gla_recurrence",
    )(q, kdm, vT, sz, dec)


def _out_ln_kernel(o_ref, x_ref, w_ref, g_ref, b_ref, y_ref):
    sub = OUT_SUB
    nsub = o_ref.shape[0] // sub

    def proj(r):
        return _dot(o_ref[r * sub:(r + 1) * sub, :], w_ref[...])

    y_next = proj(0)
    for r in range(nsub):
        rows = slice(r * sub, (r + 1) * sub)
        y = y_next
        if r + 1 < nsub:
            y_next = proj(r + 1)
        v = DEEPNORM_ALPHA * x_ref[rows, :] + y
        mu = jnp.mean(v, axis=-1, keepdims=True)
        vc = v - mu
        var = jnp.mean(vc * vc, axis=-1, keepdims=True)
        y_ref[rows, :] = vc * lax.rsqrt(var + LN_EPS) * g_ref[...] + b_ref[...]


def _out_proj_ln(o, x2, w_out, g, b):
    n = x2.shape[0]
    bm = OUT_BM
    return pl.pallas_call(
        _out_ln_kernel,
        grid=(n // bm,),
        in_specs=[pl.BlockSpec((bm, o.shape[1]), lambda i: (i, 0)),
                  pl.BlockSpec((bm, D_MODEL), lambda i: (i, 0)),
                  _resident((o.shape[1], D_MODEL)),
                  _resident((1, D_MODEL)),
                  _resident((1, D_MODEL))],
        out_specs=pl.BlockSpec((bm, D_MODEL), lambda i: (i, 0)),
        out_shape=jax.ShapeDtypeStruct((n, D_MODEL), F32),
        compiler_params=_params("arbitrary"),
        name="out_proj_ln",
    )(o, x2, w_out.astype(BF16), g.reshape(1, D_MODEL), b.reshape(1, D_MODEL))


def _mla_in_kernel(x_ref, wc_ref, wz_ref, wr_ref, gq_ref, gkv_ref, tab_ref,
                   cq_ref, ckv_ref, z_ref, kr_ref):
    xb = x_ref[...].astype(BF16)
    c = _dot(xb, wc_ref[:, :MLA_Q_RANK + MLA_KV_RANK])
    cq_ref[...] = _rms_rows(c[:, :MLA_Q_RANK], gq_ref[...]).astype(BF16)
    ckv_ref[...] = _rms_rows(c[:, MLA_Q_RANK:], gkv_ref[...]).astype(BF16)
    z_ref[...] = _silu(_dot(xb, wz_ref[:, :MLA_HV])).astype(BF16)
    a = _dot_nt(xb, wr_ref[...]) * tab_ref[...]
    kr_ref[...] = (a + pltpu.roll(a, LANES // 2, axis=1)).astype(BF16)


def _mla_in_proj(x2, w_in, g_q, g_kv, tab):
    n = x2.shape[0]
    bm = MLA_IN_BM
    nc = MLA_Q_RANK + MLA_KV_RANK
    half = MLA_ROPE // 2
    wT = jnp.swapaxes(w_in, 0, 1)
    wc = _prep_transposed(wT, nc // PREP_BR, PREP_BR, 1, lambda j: j)
    wz = _prep_transposed(wT, MLA_HV // (4 * MLA_ROPE), MLA_ROPE, 4,
                          lambda j: nc // MLA_ROPE + 1 + 4 * j)
    wr = _prep_rows(wT, nc // MLA_ROPE, 1, MLA_ROPE)
    t1, t2 = wr[:half], wr[half:]
    wr4T = jnp.concatenate([t1, t2, t2, t1], axis=0)
    row = lambda i: (i, 0)
    return pl.pallas_call(
        _mla_in_kernel,
        grid=(n // bm,),
        in_specs=[pl.BlockSpec((bm, D_MODEL), row),
                  _resident(wc.shape),
                  _resident(wz.shape),
                  _resident((LANES, D_MODEL)),
                  _resident((1, MLA_Q_RANK)),
                  _resident((1, MLA_KV_RANK)),
                  pl.BlockSpec((bm, LANES), row)],
        out_specs=[pl.BlockSpec((bm, MLA_Q_RANK), row),
                   pl.BlockSpec((bm, MLA_KV_RANK), row),
                   pl.BlockSpec((bm, MLA_HV), row),
                   pl.BlockSpec((bm, LANES), row)],
        out_shape=[jax.ShapeDtypeStruct((n, MLA_Q_RANK), BF16),
                   jax.ShapeDtypeStruct((n, MLA_KV_RANK), BF16),
                   jax.ShapeDtypeStruct((n, MLA_HV), BF16),
                   jax.ShapeDtypeStruct((n, LANES), BF16)],
        compiler_params=_params("arbitrary"),
        name="mla_in_proj",
    )(x2, wc, wz, wr4T, g_q.reshape(1, -1), g_kv.reshape(1, -1), tab)


def _mla_up_kernel(cq_ref, ckv_ref, wqT_ref, wk_ref, wvT_ref, tabT_ref,
                   qT_ref, k_ref, vT_ref):
    cq = cq_ref[...]
    ckv = ckv_ref[...]
    scale = MLA_QK ** -0.5 * math.log2(math.e)
    tab = tabT_ref[...] * scale
    group = 4 * QHEAD
    for g0 in range(0, MLA_HEADS * QHEAD, group):
        acc = _dot_nt(wqT_ref[g0:g0 + group, :], cq)
        for h0 in range(0, group, QHEAD):
            r0 = g0 + h0
            qT_ref[r0:r0 + LANES, :] = (acc[h0:h0 + LANES] * scale).astype(BF16)
            qT_ref[r0 + LANES:r0 + QHEAD, :] = (acc[h0 + LANES:h0 + QHEAD] * tab).astype(BF16)
    k_ref[...] = _dot(ckv, wk_ref[...]).astype(BF16)
    vT_ref[...] = _dot_nt(wvT_ref[...], ckv).astype(BF16)


def _mla_up_proj(cq, ckv, w_uq, w_ukv, tabT):
    n = cq.shape[0]
    bm = UP_BM
    half = MLA_ROPE // 2
    wq = w_uq.astype(BF16).reshape(MLA_Q_RANK, MLA_HEADS, MLA_QK)
    nope = wq[:, :, :MLA_NOPE]
    t1 = wq[:, :, MLA_NOPE:MLA_NOPE + half]
    t2 = wq[:, :, MLA_NOPE + half:]
    wq4 = jnp.concatenate([nope, t1, t2, t2, t1], axis=2)
    wqT = wq4.reshape(MLA_Q_RANK, MLA_HEADS * QHEAD).T
    wkv = w_ukv.astype(BF16).reshape(MLA_KV_RANK, MLA_HEADS, MLA_NOPE + MLA_V)
    wk = wkv[:, :, :MLA_NOPE].reshape(MLA_KV_RANK, MLA_HEADS * MLA_NOPE)
    wvT = wkv[:, :, MLA_NOPE:].reshape(MLA_KV_RANK, MLA_HV).T
    row = lambda i: (i, 0)
    col = lambda i: (0, i)
    return pl.pallas_call(
        _mla_up_kernel,
        grid=(n // bm,),
        in_specs=[pl.BlockSpec((bm, MLA_Q_RANK), row),
                  pl.BlockSpec((bm, MLA_KV_RANK), row),
                  _resident(wqT.shape),
                  _resident(wk.shape),
                  _resident(wvT.shape),
                  pl.BlockSpec((LANES, bm), col)],
        out_specs=[pl.BlockSpec((MLA_HEADS * QHEAD, bm), col),
                   pl.BlockSpec((bm, MLA_HEADS * MLA_NOPE), row),
                   pl.BlockSpec((MLA_HV, bm), col)],
        out_shape=[jax.ShapeDtypeStruct((MLA_HEADS * QHEAD, n), BF16),
                   jax.ShapeDtypeStruct((n, MLA_HEADS * MLA_NOPE), BF16),
                   jax.ShapeDtypeStruct((MLA_HV, n), BF16)],
        compiler_params=_params("arbitrary"),
        name="mla_up_proj",
    )(cq, ckv, wqT, wk, wvT, tabT)


def _mla_attn_kernel(qT_ref, k_ref, kr_ref, vT_ref, z_ref, qTn_ref, kn_ref, krn_ref,
                     o_ref, s_ref, m_ref):
    t = ATT_T
    nt = qT_ref.shape[1] // t
    kk = lax.broadcasted_iota(jnp.int32, (t, t), 0) // CHUNK
    qq = lax.broadcasted_iota(jnp.int32, (t, t), 1) // CHUNK
    visible = kk <= qq
    ones = jnp.ones((BF16_ROWS, t), BF16)
    blocks = [(i, j) for i in range(nt) for j in range(i + 1)]

    def score_steps(slot, q_tile, k_block, kr):
        m = [None] * nt

        def score(idx, i, j):
            ks = slice(j * t, (j + 1) * t)
            kcat = jnp.concatenate([k_block(ks), kr[ks, :]], axis=1)
            s = _dot(kcat, q_tile(i))
            if j == i:
                s = jnp.where(visible, s, -jnp.inf)
            m_blk = jnp.max(s, axis=0, keepdims=True)
            m[i] = m_blk if m[i] is None else jnp.maximum(m[i], m_blk)
            s_ref[slot, idx] = s
            if j == i:
                m_ref[slot, i] = m[i]

        return [functools.partial(score, idx, i, j) for idx, (i, j) in enumerate(blocks)]

    def local_score_steps(h):
        return score_steps(
            h % 2,
            lambda i: qT_ref[h * QHEAD:(h + 1) * QHEAD, i * t:(i + 1) * t],
            lambda ks: k_ref[ks, h * MLA_NOPE:(h + 1) * MLA_NOPE],
            kr_ref)

    def value_steps(h):
        slot = h % 2
        acc = [None] * nt

        def value(idx, i, j):
            p = jnp.exp2(s_ref[slot, idx] - m_ref[slot, i]).astype(BF16)
            v_aug = jnp.concatenate([vT_ref[h * MLA_V:(h + 1) * MLA_V, j * t:(j + 1) * t],
                                     ones], axis=0)
            pv = _dot(v_aug, p)
            acc[i] = pv if acc[i] is None else acc[i] + pv
            if j == i:
                a = acc[i]
                o = (a[:MLA_V] * (1.0 / a[MLA_V:MLA_V + 1])).T
                rows = slice(i * t, (i + 1) * t)
                cols = slice(h * MLA_V, (h + 1) * MLA_V)
                o_ref[rows, cols] = (o * z_ref[rows, cols]).astype(BF16)

        return [functools.partial(value, idx, i, j) for idx, (i, j) in enumerate(blocks)]

    @pl.when((pl.program_id(0) == 0) & (pl.program_id(1) == 0))
    def _():
        for f in local_score_steps(0):
            f()

    assert ATT_HB % 2 == 0
    for h in range(ATT_HB):
        if h + 1 < ATT_HB:
            nxt = local_score_steps(h + 1)
        else:
            nxt = score_steps(0, lambda i: qTn_ref[:, i * t:(i + 1) * t],
                              lambda ks: kn_ref[ks, :], krn_ref)
        for f_score, f_value in zip(nxt, value_steps(h)):
            f_score()
            f_value()


def _mla_attention(qT, k, kr, vT, z, batch, seq):
    n = k.shape[0]
    hb = ATT_HB
    groups = MLA_HEADS // hb
    nt = seq // ATT_T
    n_blocks = nt * (nt + 1) // 2

    def next_step(b, g):
        s = jnp.minimum(b * groups + g + 1, batch * groups - 1)
        return s // groups, s % groups

    return pl.pallas_call(
        _mla_attn_kernel,
        grid=(batch, groups),
        in_specs=[pl.BlockSpec((hb * QHEAD, seq), lambda b, g: (g, b)),
                  pl.BlockSpec((seq, hb * MLA_NOPE), lambda b, g: (b, g)),
                  pl.BlockSpec((seq, LANES), lambda b, g: (b, 0)),
                  pl.BlockSpec((hb * MLA_V, seq), lambda b, g: (g, b)),
                  pl.BlockSpec((seq, hb * MLA_V), lambda b, g: (b, g)),
                  pl.BlockSpec((QHEAD, seq), lambda b, g: (hb * next_step(b, g)[1], next_step(b, g)[0])),
                  pl.BlockSpec((seq, MLA_NOPE), lambda b, g: (next_step(b, g)[0], hb * next_step(b, g)[1])),
                  pl.BlockSpec((seq, LANES), lambda b, g: (next_step(b, g)[0], 0))],
        out_specs=pl.BlockSpec((seq, hb * MLA_V), lambda b, g: (b, g)),
        out_shape=jax.ShapeDtypeStruct((n, MLA_HV), BF16),
        scratch_shapes=[pltpu.VMEM((2, n_blocks, ATT_T, ATT_T), F32),
                        pltpu.VMEM((2, nt, 1, ATT_T), F32)],
        compiler_params=_params("arbitrary", "arbitrary"),
        name="mla_attention",
    )(qT, k, kr, vT, z, qT, k, kr)


def kernel(x, positions, gla_w_in, gla_w_a2, gla_b_a, gla_g_out, gla_w_out,
           mla_w_in, mla_g_q, mla_w_uq, mla_g_kv, mla_w_ukv, mla_w_out, ln_g, ln_b):
    batch, seq, d = x.shape
    assert d == D_MODEL and seq % GLA_TC == 0 and seq % ATT_T == 0
    x2 = x.reshape(batch * seq, d)

    q, kdm, sz, vT, dec = _gla_in_proj(x2, gla_w_in[0], gla_w_a2[0], gla_b_a[0], gla_g_out[0])
    o = _gla_recurrence(q, kdm, vT, sz, dec, batch, seq)
    x2 = _out_proj_ln(o, x2, gla_w_out[0], ln_g[0], ln_b[0])

    tabT, tab = _rope_tables(positions)
    cq, ckv, z, kr = _mla_in_proj(x2, mla_w_in[0], mla_g_q[0], mla_g_kv[0], tab)
    qT, kn, vT = _mla_up_proj(cq, ckv, mla_w_uq[0], mla_w_ukv[0], tabT)
    o = _mla_attention(qT, kn, kr, vT, z, batch, seq)
    x2 = _out_proj_ln(o, x2, mla_w_out[0], ln_g[1], ln_b[1])
    return x2.reshape(batch, seq, d)
```

```python
import functools
import math

import jax
import jax.numpy as jnp
from jax import lax
from jax.experimental import pallas as pl
from jax.experimental.pallas import tpu as pltpu

BF16 = jnp.bfloat16
F32 = jnp.float32

D_MODEL = 2048
DEPTH = 2
CHUNK = 64
DEEPNORM_ALPHA = float((2 * DEPTH) ** 0.25)
LN_EPS = 1e-5
RMS_EPS = 1e-6

GLA_HEADS = 4
GLA_DK = 256
GLA_DV = 512
GLA_HK = GLA_HEADS * GLA_DK
GLA_HV = GLA_HEADS * GLA_DV
GLA_GATE_RANK = 16
GLA_TAU = 16.0

MLA_HEADS = 16
MLA_Q_RANK = 512
MLA_KV_RANK = 512
MLA_NOPE = 128
MLA_ROPE = 64
MLA_V = 128
MLA_QK = MLA_NOPE + MLA_ROPE
MLA_HV = MLA_HEADS * MLA_V
ROPE_THETA = 10000.0

LANES = 128
SUBLANES = 8
BF16_ROWS = 16
VMEM_LIMIT = 56 * 1024 * 1024

PROJ_BM = 256
OUT_BM = 512
OUT_SUB = 128
UP_BM = 512
MLA_IN_BM = 512
GLA_TC = 512
ATT_T = 256
ATT_HB = 4
ROPE_TAB_R = 2048
PREP_BR = 256
WEIGHT_PAD = 2 * LANES
QHEAD = 2 * LANES


def _params(*sem):
    return pltpu.CompilerParams(dimension_semantics=sem, vmem_limit_bytes=VMEM_LIMIT)


def _resident(shape):
    zeros = (0,) * len(shape)
    return pl.BlockSpec(shape, lambda *_: zeros, pipeline_mode=pl.Buffered(1))


def _dot(a, b):
    return jnp.dot(a, b, preferred_element_type=F32)


def _dot_nt(a, b):
    return lax.dot_general(a, b, (((1,), (1,)), ((), ())), preferred_element_type=F32)


def _silu(z):
    return z * (1.0 / (1.0 + jnp.exp(-z)))


def _rms_rows(x, g):
    ms = jnp.mean(x * x, axis=-1, keepdims=True)
    return x * lax.rsqrt(ms + RMS_EPS) * g


def _prep_transpose_kernel(*refs, n_out, tail_rows):
    *w_refs, o_ref = refs
    rows = [r[...] for r in w_refs]
    wt = (rows[0] if len(rows) == 1 else jnp.concatenate(rows, axis=0)).T

    @pl.when(pl.program_id(0) < n_out)
    def _():
        o_ref[...] = wt.astype(BF16)

    @pl.when(pl.program_id(0) >= n_out)
    def _():
        col = lax.broadcasted_iota(jnp.int32, wt.shape, 1)
        o_ref[...] = jnp.where(col < tail_rows, wt, 0.0).astype(BF16)


def _prep_transposed(wT, n_out, br, parts, first_block, tail_rows=0):
    d = wT.shape[1]
    assert br * parts == WEIGHT_PAD and (tail_rows == 0 or parts == 1)
    last = n_out if tail_rows else n_out - 1
    src_block = lambda k, j: (first_block(jnp.minimum(j, last)) + k, 0)
    return pl.pallas_call(
        functools.partial(_prep_transpose_kernel, n_out=n_out, tail_rows=tail_rows),
        grid=(n_out + 1,),
        in_specs=[pl.BlockSpec((br, d), functools.partial(src_block, k)) for k in range(parts)],
        out_specs=pl.BlockSpec((d, br * parts), lambda j: (0, j)),
        out_shape=jax.ShapeDtypeStruct((d, (n_out + 1) * br * parts), BF16),
        compiler_params=_params("arbitrary"),
        name="prep_transposed",
    )(*([wT] * parts))


def _prep_rows_kernel(w_ref, o_ref):
    d = w_ref.shape[1]
    o_ref[:, :d] = w_ref[...].astype(BF16)
    if o_ref.shape[1] > d:
        o_ref[:, d:] = jnp.zeros((o_ref.shape[0], o_ref.shape[1] - d), BF16)


def _prep_rows(wT, block0, n_blocks, br, pad=0):
    d = wT.shape[1]
    return pl.pallas_call(
        _prep_rows_kernel,
        grid=(n_blocks,),
        in_specs=[pl.BlockSpec((br, d), lambda j: (block0 + j, 0))],
        out_specs=pl.BlockSpec((br, d + pad), lambda j: (j, 0)),
        out_shape=jax.ShapeDtypeStruct((n_blocks * br, d + pad), BF16),
        compiler_params=_params("arbitrary"),
        name="prep_rows",
    )(wT)


def _rope_tab_kernel(pos_ref, freq_ref, tabT_ref, tab_ref):
    pos = pos_ref[...].astype(F32)
    ang = freq_ref[...] * pos
    c = jnp.cos(ang)
    s = jnp.sin(ang)
    t = jnp.concatenate([c, c, -s, s], axis=0)
    tabT_ref[...] = t
    tab_ref[...] = t.T


def _rope_tables(positions):
    n = positions.size
    half = MLA_ROPE // 2
    inv_freq = ROPE_THETA ** (-jnp.arange(half, dtype=F32) / half)
    freq_col = inv_freq.reshape(half, 1)
    pos_row = positions.reshape(1, n)
    return pl.pallas_call(
        _rope_tab_kernel,
        grid=(n // ROPE_TAB_R,),
        in_specs=[pl.BlockSpec((1, ROPE_TAB_R), lambda i: (0, i)),
                  pl.BlockSpec((half, 1), lambda i: (0, 0))],
        out_specs=[pl.BlockSpec((LANES, ROPE_TAB_R), lambda i: (0, i)),
                   pl.BlockSpec((ROPE_TAB_R, LANES), lambda i: (i, 0))],
        out_shape=[jax.ShapeDtypeStruct((LANES, n), F32),
                   jax.ShapeDtypeStruct((n, LANES), F32)],
        compiler_params=_params("arbitrary"),
        name="rope_tables",
    )(pos_row, freq_col)


def _chunk_cumsum(la):
    t, n = la.shape
    x = la.reshape(t // SUBLANES, SUBLANES, n)
    row = lax.broadcasted_iota(jnp.int32, x.shape, 1)
    s = 1
    while s < SUBLANES:
        x = x + jnp.where(row >= s, pltpu.roll(x, s, axis=1), 0.0)
        s *= 2
    tiles_per_chunk = CHUNK // SUBLANES
    out = []
    carry = None
    for i in range(t // SUBLANES):
        tile = x[i]
        if i % tiles_per_chunk != 0:
            tile = tile + carry
        carry = tile[SUBLANES - 1:SUBLANES, :]
        out.append(tile)
    return jnp.concatenate(out, axis=0)


def _gla_in_kernel(x_ref, w_ref, wvT_ref, wa2_ref, ba_ref, g_ref,
                   q_ref, kdm_ref, sz_ref, vT_ref, dec_ref):
    bm = x_ref.shape[0]
    z0 = 2 * GLA_HK + GLA_HV
    xb = x_ref[...].astype(BF16)
    q_ref[...] = _dot(xb, w_ref[:, :GLA_HK]).astype(BF16)
    za = _dot(xb, w_ref[:, z0:])
    z = za[:, :GLA_HV]
    a = za[:, GLA_HV:GLA_HV + LANES].astype(BF16)
    sz_ref[...] = (_silu(z) * g_ref[...]).astype(BF16)
    vT_ref[...] = _dot_nt(wvT_ref[:, :D_MODEL], xb).astype(BF16)

    pre = _dot(a, wa2_ref[...]) + ba_ref[...]
    la = (jnp.minimum(pre, 0.0) - jnp.log(1.0 + jnp.exp(-jnp.abs(pre)))) * (1.0 / GLA_TAU)
    cum = _chunk_cumsum(la)
    chunks = bm // CHUNK
    tots = [cum[(c + 1) * CHUNK - 1:(c + 1) * CHUNK, :] for c in range(chunks)]
    dec_ref[0] = jnp.exp(jnp.concatenate(tots, axis=0))
    tot_rows = jnp.concatenate([jnp.broadcast_to(t, (CHUNK, GLA_HK)) for t in tots], axis=0)
    kd = (_dot(xb, w_ref[:, GLA_HK:2 * GLA_HK]) * jnp.exp(tot_rows - cum)).astype(BF16)

    zeros = jnp.zeros((CHUNK, GLA_DK), BF16)
    for c in range(chunks):
        rows = slice(c * CHUNK, (c + 1) * CHUNK)
        for h in range(GLA_HEADS):
            c0 = h * 2 * GLA_DK
            mine = slice(c0 + (c % 2) * GLA_DK, c0 + (c % 2 + 1) * GLA_DK)
            other = slice(c0 + (1 - c % 2) * GLA_DK, c0 + (2 - c % 2) * GLA_DK)
            kdm_ref[rows, mine] = kd[rows, h * GLA_DK:(h + 1) * GLA_DK]
            kdm_ref[rows, other] = zeros


def _gla_in_proj(x2, w_in, w_a2, b_a, g_out):
    n = x2.shape[0]
    bm = PROJ_BM
    br = PREP_BR
    wT = jnp.swapaxes(w_in, 0, 1)
    a0 = 2 * GLA_HK + 2 * GLA_HV
    wb = _prep_transposed(wT, a0 // br, br, 1, lambda j: j,
                          tail_rows=GLA_GATE_RANK)
    wvT = _prep_rows(wT, 2 * GLA_HK // br, GLA_HV // br, br, WEIGHT_PAD)
    wa2_p = jnp.pad(w_a2, ((0, LANES - GLA_GATE_RANK), (0, 0))).astype(BF16)
    row = lambda i: (i, 0)
    return pl.pallas_call(
        _gla_in_kernel,
        grid=(n // bm,),
        in_specs=[pl.BlockSpec((bm, D_MODEL), row),
                  _resident(wb.shape),
                  _resident(wvT.shape),
                  _resident((LANES, GLA_HK)),
                  _resident((1, GLA_HK)),
                  _resident((1, GLA_HV))],
        out_specs=[pl.BlockSpec((bm, GLA_HK), row),
                   pl.BlockSpec((bm, 2 * GLA_HK), row),
                   pl.BlockSpec((bm, GLA_HV), row),
                   pl.BlockSpec((GLA_HV, bm), lambda i: (0, i)),
                   pl.BlockSpec((1, bm // CHUNK, GLA_HK), lambda i: (i, 0, 0))],
        out_shape=[jax.ShapeDtypeStruct((n, GLA_HK), BF16),
                   jax.ShapeDtypeStruct((n, 2 * GLA_HK), BF16),
                   jax.ShapeDtypeStruct((n, GLA_HV), BF16),
                   jax.ShapeDtypeStruct((GLA_HV, n), BF16),
                   jax.ShapeDtypeStruct((n // bm, bm // CHUNK, GLA_HK), F32)],
        compiler_params=_params("arbitrary"),
        name="gla_in_proj",
    )(x2, wb, wvT, wa2_p, b_a.reshape(1, GLA_HK), g_out.reshape(1, GLA_HV))


def _gla_rec_kernel(q_ref, kdm_ref, vT_ref, sz_ref, dec_ref, o_ref, st_ref):
    @pl.when(pl.program_id(1) == 0)
    def _():
        st_ref[...] = jnp.zeros_like(st_ref)

    pair = 2 * CHUNK
    chunks_per_block = dec_ref.shape[1]
    eps = RMS_EPS * GLA_DK
    units = [(p, h) for p in range(GLA_TC // pair) for h in range(GLA_HEADS)]

    def pair_update(p, h):
        return _dot(vT_ref[h * GLA_DV:(h + 1) * GLA_DV, p * pair:(p + 1) * pair],
                    kdm_ref[p * pair:(p + 1) * pair, h * 2 * GLA_DK:(h + 1) * 2 * GLA_DK])

    kv_next = pair_update(*units[0])
    for n, (p, h) in enumerate(units):
        kv2 = kv_next
        if n + 1 < len(units):
            kv_next = pair_update(*units[n + 1])
        ks = slice(h * GLA_DK, (h + 1) * GLA_DK)
        vs = slice(h * GLA_DV, (h + 1) * GLA_DV)
        for e in range(2):
            c = 2 * p + e
            rows = slice(c * CHUNK, (c + 1) * CHUNK)
            dec = dec_ref[c // chunks_per_block, c % chunks_per_block:c % chunks_per_block + 1, ks]
            s_new = st_ref[h] * dec + kv2[:, e * GLA_DK:(e + 1) * GLA_DK]
            st_ref[h] = s_new
            o = _dot_nt(q_ref[rows, ks], s_new.astype(BF16))
            ms = jnp.mean(o * o, axis=-1, keepdims=True)
            o_ref[rows, vs] = (o * lax.rsqrt(ms + eps) * sz_ref[rows, vs]).astype(BF16)


def _gla_recurrence(q, kdm, vT, sz, dec, batch, seq):
    n = q.shape[0]
    tc = GLA_TC
    steps = seq // tc
    blocks = tc // PROJ_BM
    row = lambda b, t: (b * steps + t, 0)
    return pl.pallas_call(
        _gla_rec_kernel,
        grid=(batch, steps),
        in_specs=[pl.BlockSpec((tc, GLA_HK), row),
                  pl.BlockSpec((tc, 2 * GLA_HK), row),
                  pl.BlockSpec((GLA_HV, tc), lambda b, t: (0, b * steps + t)),
                  pl.BlockSpec((tc, GLA_HV), row),
                  pl.BlockSpec((blocks, dec.shape[1], GLA_HK), lambda b, t: (b * steps + t, 0, 0))],
        out_specs=pl.BlockSpec((tc, GLA_HV), row),
        out_shape=jax.ShapeDtypeStruct((n, GLA_HV), BF16),
        scratch_shapes=[pltpu.VMEM((GLA_HEADS, GLA_DV, GLA_DK), F32)],
        compiler_params=_params("arbitrary", "arbitrary"),
        name="gla_recurrence",
    )(q, kdm, vT, sz, dec)


def _out_ln_kernel(o_ref, x_ref, w_ref, g_ref, b_ref, y_ref):
    sub = OUT_SUB
    nsub = o_ref.shape[0] // sub

    def proj(r):
        return _dot(o_ref[r * sub:(r + 1) * sub, :], w_ref[...])

    y_next = proj(0)
    for r in range(nsub):
        rows = slice(r * sub, (r + 1) * sub)
        y = y_next
        if r + 1 < nsub:
            y_next = proj(r + 1)
        v = DEEPNORM_ALPHA * x_ref[rows, :] + y
        mu = jnp.mean(v, axis=-1, keepdims=True)
        vc = v - mu
        var = jnp.mean(vc * vc, axis=-1, keepdims=True)
        y_ref[rows, :] = vc * lax.rsqrt(var + LN_EPS) * g_ref[...] + b_ref[...]


def _out_proj_ln(o, x2, w_out, g, b):
    n = x2.shape[0]
    bm = OUT_BM
    return pl.pallas_call(
        _out_ln_kernel,
        grid=(n // bm,),
        in_specs=[pl.BlockSpec((bm, o.shape[1]), lambda i: (i, 0)),
                  pl.BlockSpec((bm, D_MODEL), lambda i: (i, 0)),
                  _resident((o.shape[1], D_MODEL)),
                  _resident((1, D_MODEL)),
                  _resident((1, D_MODEL))],
        out_specs=pl.BlockSpec((bm, D_MODEL), lambda i: (i, 0)),
        out_shape=jax.ShapeDtypeStruct((n, D_MODEL), F32),
        compiler_params=_params("arbitrary"),
        name="out_proj_ln",
    )(o, x2, w_out.astype(BF16), g.reshape(1, D_MODEL), b.reshape(1, D_MODEL))


def _mla_in_kernel(x_ref, wc_ref, wz_ref, wr_ref, gq_ref, gkv_ref, tab_ref,
                   cq_ref, ckv_ref, z_ref, kr_ref):
    xb = x_ref[...].astype(BF16)
    c = _dot(xb, wc_ref[:, :MLA_Q_RANK + MLA_KV_RANK])
    cq_ref[...] = _rms_rows(c[:, :MLA_Q_RANK], gq_ref[...]).astype(BF16)
    ckv_ref[...] = _rms_rows(c[:, MLA_Q_RANK:], gkv_ref[...]).astype(BF16)
    z_ref[...] = _silu(_dot(xb, wz_ref[:, :MLA_HV])).astype(BF16)
    a = _dot_nt(xb, wr_ref[...]) * tab_ref[...]
    kr_ref[...] = (a + pltpu.roll(a, LANES // 2, axis=1)).astype(BF16)


def _mla_in_proj(x2, w_in, g_q, g_kv, tab):
    n = x2.shape[0]
    bm = MLA_IN_BM
    nc = MLA_Q_RANK + MLA_KV_RANK
    half = MLA_ROPE // 2
    wT = jnp.swapaxes(w_in, 0, 1)
    wc = _prep_transposed(wT, nc // PREP_BR, PREP_BR, 1, lambda j: j)
    wz = _prep_transposed(wT, MLA_HV // (4 * MLA_ROPE), MLA_ROPE, 4,
                          lambda j: nc // MLA_ROPE + 1 + 4 * j)
    wr = _prep_rows(wT, nc // MLA_ROPE, 1, MLA_ROPE)
    t1, t2 = wr[:half], wr[half:]
    wr4T = jnp.concatenate([t1, t2, t2, t1], axis=0)
    row = lambda i: (i, 0)
    return pl.pallas_call(
        _mla_in_kernel,
        grid=(n // bm,),
        in_specs=[pl.BlockSpec((bm, D_MODEL), row),
                  _resident(wc.shape),
                  _resident(wz.shape),
                  _resident((LANES, D_MODEL)),
                  _resident((1, MLA_Q_RANK)),
                  _resident((1, MLA_KV_RANK)),
                  pl.BlockSpec((bm, LANES), row)],
        out_specs=[pl.BlockSpec((bm, MLA_Q_RANK), row),
                   pl.BlockSpec((bm, MLA_KV_RANK), row),
                   pl.BlockSpec((bm, MLA_HV), row),
                   pl.BlockSpec((bm, LANES), row)],
        out_shape=[jax.ShapeDtypeStruct((n, MLA_Q_RANK), BF16),
                   jax.ShapeDtypeStruct((n, MLA_KV_RANK), BF16),
                   jax.ShapeDtypeStruct((n, MLA_HV), BF16),
                   jax.ShapeDtypeStruct((n, LANES), BF16)],
        compiler_params=_params("arbitrary"),
        name="mla_in_proj",
    )(x2, wc, wz, wr4T, g_q.reshape(1, -1), g_kv.reshape(1, -1), tab)


def _mla_up_kernel(cq_ref, ckv_ref, wqT_ref, wk_ref, wvT_ref, tabT_ref,
                   qT_ref, k_ref, vT_ref):
    cq = cq_ref[...]
    ckv = ckv_ref[...]
    scale = MLA_QK ** -0.5 * math.log2(math.e)
    tab = tabT_ref[...] * scale
    heads = 4
    half = MLA_ROPE // 2
    for g in range(0, MLA_HEADS, heads):
        acc = _dot_nt(wqT_ref[g * MLA_QK:(g + heads) * MLA_QK, :], cq)
        for h in range(heads):
            a0 = h * MLA_QK
            r0 = (g + h) * QHEAD
            t1 = acc[a0 + MLA_NOPE:a0 + MLA_NOPE + half]
            t2 = acc[a0 + MLA_NOPE + half:a0 + MLA_QK]
            rot = jnp.concatenate([t1, t2, t2, t1], axis=0) * tab
            qT_ref[r0:r0 + LANES, :] = (acc[a0:a0 + MLA_NOPE] * scale).astype(BF16)
            qT_ref[r0 + LANES:r0 + QHEAD, :] = rot.astype(BF16)
    k_ref[...] = _dot(ckv, wk_ref[...]).astype(BF16)
    vT_ref[...] = _dot_nt(wvT_ref[...], ckv).astype(BF16)


def _mla_up_proj(cq, ckv, w_uq, w_ukv, tabT):
    n = cq.shape[0]
    bm = UP_BM
    wqT = w_uq.astype(BF16).T
    wkv = w_ukv.astype(BF16).reshape(MLA_KV_RANK, MLA_HEADS, MLA_NOPE + MLA_V)
    wk = wkv[:, :, :MLA_NOPE].reshape(MLA_KV_RANK, MLA_HEADS * MLA_NOPE)
    wvT = wkv[:, :, MLA_NOPE:].reshape(MLA_KV_RANK, MLA_HV).T
    row = lambda i: (i, 0)
    col = lambda i: (0, i)
    return pl.pallas_call(
        _mla_up_kernel,
        grid=(n // bm,),
        in_specs=[pl.BlockSpec((bm, MLA_Q_RANK), row),
                  pl.BlockSpec((bm, MLA_KV_RANK), row),
                  _resident(wqT.shape),
                  _resident(wk.shape),
                  _resident(wvT.shape),
                  pl.BlockSpec((LANES, bm), col)],
        out_specs=[pl.BlockSpec((MLA_HEADS * QHEAD, bm), col),
                   pl.BlockSpec((bm, MLA_HEADS * MLA_NOPE), row),
                   pl.BlockSpec((MLA_HV, bm), col)],
        out_shape=[jax.ShapeDtypeStruct((MLA_HEADS * QHEAD, n), BF16),
                   jax.ShapeDtypeStruct((n, MLA_HEADS * MLA_NOPE), BF16),
                   jax.ShapeDtypeStruct((MLA_HV, n), BF16)],
        compiler_params=_params("arbitrary"),
        name="mla_up_proj",
    )(cq, ckv, wqT, wk, wvT, tabT)


def _mla_attn_kernel(qT_ref, k_ref, kr_ref, vT_ref, z_ref, qTn_ref, kn_ref, krn_ref,
                     o_ref, s_ref, m_ref):
    t = ATT_T
    nt = qT_ref.shape[1] // t
    kk = lax.broadcasted_iota(jnp.int32, (t, t), 0) // CHUNK
    qq = lax.broadcasted_iota(jnp.int32, (t, t), 1) // CHUNK
    visible = kk <= qq
    ones = jnp.ones((BF16_ROWS, t), BF16)
    blocks = [(i, j) for i in range(nt) for j in range(i + 1)]

    def score_steps(slot, q_tile, k_block, kr):
        m = [None] * nt

        def score(idx, i, j):
            ks = slice(j * t, (j + 1) * t)
            kcat = jnp.concatenate([k_block(ks), kr[ks, :]], axis=1)
            s = _dot(kcat, q_tile(i))
            if j == i:
                s = jnp.where(visible, s, -jnp.inf)
            m_blk = jnp.max(s, axis=0, keepdims=True)
            m[i] = m_blk if m[i] is None else jnp.maximum(m[i], m_blk)
            s_ref[slot, idx] = s
            if j == i:
                m_ref[slot, i] = m[i]

        return [functools.partial(score, idx, i, j) for idx, (i, j) in enumerate(blocks)]

    def local_score_steps(h):
        return score_steps(
            h % 2,
            lambda i: qT_ref[h * QHEAD:(h + 1) * QHEAD, i * t:(i + 1) * t],
            lambda ks: k_ref[ks, h * MLA_NOPE:(h + 1) * MLA_NOPE],
            kr_ref)

    def value_steps(h):
        slot = h % 2
        acc = [None] * nt

        def value(idx, i, j):
            p = jnp.exp2(s_ref[slot, idx] - m_ref[slot, i]).astype(BF16)
            v_aug = jnp.concatenate([vT_ref[h * MLA_V:(h + 1) * MLA_V, j * t:(j + 1) * t],
                                     ones], axis=0)
            pv = _dot(v_aug, p)
            acc[i] = pv if acc[i] is None else acc[i] + pv
            if j == i:
                a = acc[i]
                o = (a[:MLA_V] * (1.0 / a[MLA_V:MLA_V + 1])).T
                rows = slice(i * t, (i + 1) * t)
                cols = slice(h * MLA_V, (h + 1) * MLA_V)
                o_ref[rows, cols] = (o * z_ref[rows, cols]).astype(BF16)

        return [functools.partial(value, idx, i, j) for idx, (i, j) in enumerate(blocks)]

    @pl.when((pl.program_id(0) == 0) & (pl.program_id(1) == 0))
    def _():
        for f in local_score_steps(0):
            f()

    assert ATT_HB % 2 == 0
    for h in range(ATT_HB):
        if h + 1 < ATT_HB:
            nxt = local_score_steps(h + 1)
        else:
            nxt = score_steps(0, lambda i: qTn_ref[:, i * t:(i + 1) * t],
                              lambda ks: kn_ref[ks, :], krn_ref)
        for f_score, f_value in zip(nxt, value_steps(h)):
            f_score()
            f_value()


def _mla_attention(qT, k, kr, vT, z, batch, seq):
    n = k.shape[0]
    hb = ATT_HB
    groups = MLA_HEADS // hb
    nt = seq // ATT_T
    n_blocks = nt * (nt + 1) // 2

    def next_step(b, g):
        s = jnp.minimum(b * groups + g + 1, batch * groups - 1)
        return s // groups, s % groups

    return pl.pallas_call(
        _mla_attn_kernel,
        grid=(batch, groups),
        in_specs=[pl.BlockSpec((hb * QHEAD, seq), lambda b, g: (g, b)),
                  pl.BlockSpec((seq, hb * MLA_NOPE), lambda b, g: (b, g)),
                  pl.BlockSpec((seq, LANES), lambda b, g: (b, 0)),
                  pl.BlockSpec((hb * MLA_V, seq), lambda b, g: (g, b)),
                  pl.BlockSpec((seq, hb * MLA_V), lambda b, g: (b, g)),
                  pl.BlockSpec((QHEAD, seq), lambda b, g: (hb * next_step(b, g)[1], next_step(b, g)[0])),
                  pl.BlockSpec((seq, MLA_NOPE), lambda b, g: (next_step(b, g)[0], hb * next_step(b, g)[1])),
                  pl.BlockSpec((seq, LANES), lambda b, g: (next_step(b, g)[0], 0))],
        out_specs=pl.BlockSpec((seq, hb * MLA_V), lambda b, g: (b, g)),
        out_shape=jax.ShapeDtypeStruct((n, MLA_HV), BF16),
        scratch_shapes=[pltpu.VMEM((2, n_blocks, ATT_T, ATT_T), F32),
                        pltpu.VMEM((2, nt, 1, ATT_T), F32)],
        compiler_params=_params("arbitrary", "arbitrary"),
        name="mla_attention",
    )(qT, k, kr, vT, z, qT, k, kr)


def kernel(x, positions, gla_w_in, gla_w_a2, gla_b_a, gla_g_out, gla_w_out,
           mla_w_in, mla_g_q, mla_w_uq, mla_g_kv, mla_w_ukv, mla_w_out, ln_g, ln_b):
    batch, seq, d = x.shape
    assert d == D_MODEL and seq % GLA_TC == 0 and seq % ATT_T == 0
    x2 = x.reshape(batch * seq, d)

    q, kdm, sz, vT, dec = _gla_in_proj(x2, gla_w_in[0], gla_w_a2[0], gla_b_a[0], gla_g_out[0])
    o = _gla_recurrence(q, kdm, vT, sz, dec, batch, seq)
    x2 = _out_proj_ln(o, x2, gla_w_out[0], ln_g[0], ln_b[0])

    tabT, tab = _rope_tables(positions)
    cq, ckv, z, kr = _mla_in_proj(x2, mla_w_in[0], mla_g_q[0], mla_g_kv[0], tab)
    qT, kn, vT = _mla_up_proj(cq, ckv, mla_w_uq[0], mla_w_ukv[0], tabT)
    o = _mla_attention(qT, kn, kr, vT, z, batch, seq)
    x2 = _out_proj_ln(o, x2, mla_w_out[0], ln_g[1], ln_b[1])
    return x2.reshape(batch, seq, d)
```

```python
import functools
import math

import jax
import jax.numpy as jnp
from jax import lax
from jax.experimental import pallas as pl
from jax.experimental.pallas import tpu as pltpu

BF16 = jnp.bfloat16
F32 = jnp.float32

D_MODEL = 2048
DEPTH = 2
CHUNK = 64
DEEPNORM_ALPHA = float((2 * DEPTH) ** 0.25)
LN_EPS = 1e-5
RMS_EPS = 1e-6

GLA_HEADS = 4
GLA_DK = 256
GLA_DV = 512
GLA_HK = GLA_HEADS * GLA_DK
GLA_HV = GLA_HEADS * GLA_DV
GLA_GATE_RANK = 16
GLA_TAU = 16.0

MLA_HEADS = 16
MLA_Q_RANK = 512
MLA_KV_RANK = 512
MLA_NOPE = 128
MLA_ROPE = 64
MLA_V = 128
MLA_QK = MLA_NOPE + MLA_ROPE
MLA_HV = MLA_HEADS * MLA_V
ROPE_THETA = 10000.0

LANES = 128
SUBLANES = 8
BF16_ROWS = 16
VMEM_LIMIT = 56 * 1024 * 1024

PROJ_BM = 256
OUT_BM = 512
OUT_SUB = 128
UP_BM = 512
MLA_IN_BM = 512
GLA_TC = 512
ATT_T = 256
ATT_HB = 4
PREP_BR = 256
WEIGHT_PAD = 2 * LANES
QHEAD = 2 * LANES


def _params(*sem):
    return pltpu.CompilerParams(dimension_semantics=sem, vmem_limit_bytes=VMEM_LIMIT)


def _resident(shape):
    zeros = (0,) * len(shape)
    return pl.BlockSpec(shape, lambda *_: zeros, pipeline_mode=pl.Buffered(1))


def _dot(a, b):
    return jnp.dot(a, b, preferred_element_type=F32)


def _dot_nt(a, b):
    return lax.dot_general(a, b, (((1,), (1,)), ((), ())), preferred_element_type=F32)


def _silu(z):
    return z * (1.0 / (1.0 + jnp.exp(-z)))


def _rms_rows(x, g):
    ms = jnp.mean(x * x, axis=-1, keepdims=True)
    return x * lax.rsqrt(ms + RMS_EPS) * g


def _prep_transpose_kernel(*refs, n_out, tail_rows, unused):
    *w_refs, o_ref = refs
    j = pl.program_id(0)
    is_unused = (j >= unused[0]) & (j < unused[1])

    def transposed():
        rows = [r[...] for r in w_refs]
        return (rows[0] if len(rows) == 1 else jnp.concatenate(rows, axis=0)).T

    @pl.when((j < n_out) & jnp.logical_not(is_unused))
    def _():
        o_ref[...] = transposed().astype(BF16)

    @pl.when(is_unused)
    def _():
        o_ref[...] = jnp.zeros(o_ref.shape, BF16)

    @pl.when(j >= n_out)
    def _():
        wt = transposed()
        col = lax.broadcasted_iota(jnp.int32, wt.shape, 1)
        o_ref[...] = jnp.where(col < tail_rows, wt, 0.0).astype(BF16)


def _prep_transposed(wT, n_out, br, parts, first_block, tail_rows=0, unused=(0, 0)):
    d = wT.shape[1]
    assert br * parts == WEIGHT_PAD and (tail_rows == 0 or parts == 1)
    last = n_out if tail_rows else n_out - 1

    def src_block(k, j):
        j = jnp.minimum(j, last)
        j = jnp.where((j >= unused[0]) & (j < unused[1]), max(unused[0] - 1, 0), j)
        return first_block(j) + k, 0

    return pl.pallas_call(
        functools.partial(_prep_transpose_kernel, n_out=n_out, tail_rows=tail_rows, unused=unused),
        grid=(n_out + 1,),
        in_specs=[pl.BlockSpec((br, d), functools.partial(src_block, k)) for k in range(parts)],
        out_specs=pl.BlockSpec((d, br * parts), lambda j: (0, j)),
        out_shape=jax.ShapeDtypeStruct((d, (n_out + 1) * br * parts), BF16),
        compiler_params=_params("arbitrary"),
        name="prep_transposed",
    )(*([wT] * parts))


def _prep_rows_kernel(w_ref, o_ref):
    d = w_ref.shape[1]
    o_ref[:, :d] = w_ref[...].astype(BF16)
    if o_ref.shape[1] > d:
        o_ref[:, d:] = jnp.zeros((o_ref.shape[0], o_ref.shape[1] - d), BF16)


def _prep_rows(wT, block0, n_blocks, br, pad=0):
    d = wT.shape[1]
    return pl.pallas_call(
        _prep_rows_kernel,
        grid=(n_blocks,),
        in_specs=[pl.BlockSpec((br, d), lambda j: (block0 + j, 0))],
        out_specs=pl.BlockSpec((br, d + pad), lambda j: (j, 0)),
        out_shape=jax.ShapeDtypeStruct((n_blocks * br, d + pad), BF16),
        compiler_params=_params("arbitrary"),
        name="prep_rows",
    )(wT)


def _rope_table_t(pos_ref, freq_ref):
    pos = pos_ref[...].astype(F32)
    ang = freq_ref[...] * pos
    c = jnp.cos(ang)
    s = jnp.sin(ang)
    return jnp.concatenate([c, c, -s, s], axis=0)


def _rope_freqs():
    half = MLA_ROPE // 2
    return (ROPE_THETA ** (-jnp.arange(half, dtype=F32) / half)).reshape(half, 1)


def _chunk_cumsum(la):
    t, n = la.shape
    x = la.reshape(t // SUBLANES, SUBLANES, n)
    row = lax.broadcasted_iota(jnp.int32, x.shape, 1)
    s = 1
    while s < SUBLANES:
        x = x + jnp.where(row >= s, pltpu.roll(x, s, axis=1), 0.0)
        s *= 2
    tiles_per_chunk = CHUNK // SUBLANES
    out = []
    carry = None
    for i in range(t // SUBLANES):
        tile = x[i]
        if i % tiles_per_chunk != 0:
            tile = tile + carry
        carry = tile[SUBLANES - 1:SUBLANES, :]
        out.append(tile)
    return jnp.concatenate(out, axis=0)


def _gla_in_kernel(x_ref, w_ref, wvT_ref, wa2_ref, ba_ref, g_ref,
                   q_ref, kdm_ref, sz_ref, vT_ref, dec_ref):
    bm = x_ref.shape[0]
    z0 = 2 * GLA_HK + GLA_HV
    xb = x_ref[...].astype(BF16)
    q_ref[...] = _dot(xb, w_ref[:, :GLA_HK]).astype(BF16)
    za = _dot(xb, w_ref[:, z0:])
    z = za[:, :GLA_HV]
    a = za[:, GLA_HV:GLA_HV + LANES].astype(BF16)
    sz_ref[...] = (_silu(z) * g_ref[...]).astype(BF16)
    vT_ref[...] = _dot_nt(wvT_ref[:, :D_MODEL], xb).astype(BF16)

    pre = _dot(a, wa2_ref[...]) + ba_ref[...]
    la = (jnp.minimum(pre, 0.0) - jnp.log(1.0 + jnp.exp(-jnp.abs(pre)))) * (1.0 / GLA_TAU)
    cum = _chunk_cumsum(la)
    chunks = bm // CHUNK
    tots = [cum[(c + 1) * CHUNK - 1:(c + 1) * CHUNK, :] for c in range(chunks)]
    dec_ref[0] = jnp.exp(jnp.concatenate(tots, axis=0))
    tot_rows = jnp.concatenate([jnp.broadcast_to(t, (CHUNK, GLA_HK)) for t in tots], axis=0)
    kd = (_dot(xb, w_ref[:, GLA_HK:2 * GLA_HK]) * jnp.exp(tot_rows - cum)).astype(BF16)

    zeros = jnp.zeros((CHUNK, GLA_DK), BF16)
    for c in range(chunks):
        rows = slice(c * CHUNK, (c + 1) * CHUNK)
        for h in range(GLA_HEADS):
            c0 = h * 2 * GLA_DK
            mine = slice(c0 + (c % 2) * GLA_DK, c0 + (c % 2 + 1) * GLA_DK)
            other = slice(c0 + (1 - c % 2) * GLA_DK, c0 + (2 - c % 2) * GLA_DK)
            kdm_ref[rows, mine] = kd[rows, h * GLA_DK:(h + 1) * GLA_DK]
            kdm_ref[rows, other] = zeros


def _gla_in_proj(x2, w_in, w_a2, b_a, g_out):
    n = x2.shape[0]
    bm = PROJ_BM
    br = PREP_BR
    wT = jnp.swapaxes(w_in, 0, 1)
    a0 = 2 * GLA_HK + 2 * GLA_HV
    v_blocks = (2 * GLA_HK // br, (2 * GLA_HK + GLA_HV) // br)
    wb = _prep_transposed(wT, a0 // br, br, 1, lambda j: j, tail_rows=GLA_GATE_RANK,
                          unused=v_blocks)
    wvT = _prep_rows(wT, 2 * GLA_HK // br, GLA_HV // br, br, WEIGHT_PAD)
    wa2_p = jnp.pad(w_a2, ((0, LANES - GLA_GATE_RANK), (0, 0))).astype(BF16)
    row = lambda i: (i, 0)
    return pl.pallas_call(
        _gla_in_kernel,
        grid=(n // bm,),
        in_specs=[pl.BlockSpec((bm, D_MODEL), row),
                  _resident(wb.shape),
                  _resident(wvT.shape),
                  _resident((LANES, GLA_HK)),
                  _resident((1, GLA_HK)),
                  _resident((1, GLA_HV))],
        out_specs=[pl.BlockSpec((bm, GLA_HK), row),
                   pl.BlockSpec((bm, 2 * GLA_HK), row),
                   pl.BlockSpec((bm, GLA_HV), row),
                   pl.BlockSpec((GLA_HV, bm), lambda i: (0, i)),
                   pl.BlockSpec((1, bm // CHUNK, GLA_HK), lambda i: (i, 0, 0))],
        out_shape=[jax.ShapeDtypeStruct((n, GLA_HK), BF16),
                   jax.ShapeDtypeStruct((n, 2 * GLA_HK), BF16),
                   jax.ShapeDtypeStruct((n, GLA_HV), BF16),
                   jax.ShapeDtypeStruct((GLA_HV, n), BF16),
                   jax.ShapeDtypeStruct((n // bm, bm // CHUNK, GLA_HK), F32)],
        compiler_params=_params("arbitrary"),
        name="gla_in_proj",
    )(x2, wb, wvT, wa2_p, b_a.reshape(1, GLA_HK), g_out.reshape(1, GLA_HV))


def _gla_rec_kernel(q_ref, kdm_ref, vT_ref, sz_ref, dec_ref, o_ref, st_ref):
    @pl.when(pl.program_id(1) == 0)
    def _():
        st_ref[...] = jnp.zeros_like(st_ref)

    pair = 2 * CHUNK
    chunks_per_block = dec_ref.shape[1]
    eps = RMS_EPS * GLA_DK
    units = [(p, h) for p in range(GLA_TC // pair) for h in range(GLA_HEADS)]

    def pair_update(p, h):
        return _dot(vT_ref[h * GLA_DV:(h + 1) * GLA_DV, p * pair:(p + 1) * pair],
                    kdm_ref[p * pair:(p + 1) * pair, h * 2 * GLA_DK:(h + 1) * 2 * GLA_DK])

    kv_next = pair_update(*units[0])
    for n, (p, h) in enumerate(units):
        kv2 = kv_next
        if n + 1 < len(units):
            kv_next = pair_update(*units[n + 1])
        ks = slice(h * GLA_DK, (h + 1) * GLA_DK)
        vs = slice(h * GLA_DV, (h + 1) * GLA_DV)
        for e in range(2):
            c = 2 * p + e
            rows = slice(c * CHUNK, (c + 1) * CHUNK)
            dec = dec_ref[c // chunks_per_block, c % chunks_per_block:c % chunks_per_block + 1, ks]
            s_new = st_ref[h] * dec + kv2[:, e * GLA_DK:(e + 1) * GLA_DK]
            st_ref[h] = s_new
            o = _dot_nt(q_ref[rows, ks], s_new.astype(BF16))
            ms = jnp.mean(o * o, axis=-1, keepdims=True)
            o_ref[rows, vs] = (o * lax.rsqrt(ms + eps) * sz_ref[rows, vs]).astype(BF16)


def _gla_recurrence(q, kdm, vT, sz, dec, batch, seq):
    n = q.shape[0]
    tc = GLA_TC
    steps = seq // tc
    blocks = tc // PROJ_BM
    row = lambda b, t: (b * steps + t, 0)
    return pl.pallas_call(
        _gla_rec_kernel,
        grid=(batch, steps),
        in_specs=[pl.BlockSpec((tc, GLA_HK), row),
                  pl.BlockSpec((tc, 2 * GLA_HK), row),
                  pl.BlockSpec((GLA_HV, tc), lambda b, t: (0, b * steps + t)),
                  pl.BlockSpec((tc, GLA_HV), row),
                  pl.BlockSpec((blocks, dec.shape[1], GLA_HK), lambda b, t: (b * steps + t, 0, 0))],
        out_specs=pl.BlockSpec((tc, GLA_HV), row),
        out_shape=jax.ShapeDtypeStruct((n, GLA_HV), BF16),
        scratch_shapes=[pltpu.VMEM((GLA_HEADS, GLA_DV, GLA_DK), F32)],
        compiler_params=_params("arbitrary", "arbitrary"),
        name="gla_recurrence",
    )(q, kdm, vT, sz, dec)


def _out_ln_kernel(o_ref, x_ref, w_ref, g_ref, b_ref, y_ref, wb_ref):
    sub = OUT_SUB
    nsub = o_ref.shape[0] // sub

    @pl.when(pl.program_id(0) == 0)
    def _():
        wb_ref[...] = w_ref[...].astype(BF16)

    def proj(r):
        return _dot(o_ref[r * sub:(r + 1) * sub, :], wb_ref[...])

    y_next = proj(0)
    for r in range(nsub):
        rows = slice(r * sub, (r + 1) * sub)
        y = y_next
        if r + 1 < nsub:
            y_next = proj(r + 1)
        v = DEEPNORM_ALPHA * x_ref[rows, :] + y
        mu = jnp.mean(v, axis=-1, keepdims=True)
        vc = v - mu
        var = jnp.mean(vc * vc, axis=-1, keepdims=True)
        y_ref[rows, :] = vc * lax.rsqrt(var + LN_EPS) * g_ref[...] + b_ref[...]


def _out_proj_ln(o, x2, w_out, g, b):
    n = x2.shape[0]
    bm = OUT_BM
    return pl.pallas_call(
        _out_ln_kernel,
        grid=(n // bm,),
        in_specs=[pl.BlockSpec((bm, o.shape[1]), lambda i: (i, 0)),
                  pl.BlockSpec((bm, D_MODEL), lambda i: (i, 0)),
                  _resident((o.shape[1], D_MODEL)),
                  _resident((1, D_MODEL)),
                  _resident((1, D_MODEL))],
        out_specs=pl.BlockSpec((bm, D_MODEL), lambda i: (i, 0)),
        out_shape=jax.ShapeDtypeStruct((n, D_MODEL), F32),
        scratch_shapes=[pltpu.VMEM((o.shape[1], D_MODEL), BF16)],
        compiler_params=_params("arbitrary"),
        name="out_proj_ln",
    )(o, x2, w_out, g.reshape(1, D_MODEL), b.reshape(1, D_MODEL))


def _mla_in_kernel(x_ref, wc_ref, wz_ref, wr_ref, gq_ref, gkv_ref, pos_ref, freq_ref,
                   cq_ref, ckv_ref, z_ref, kr_ref):
    xb = x_ref[...].astype(BF16)
    c = _dot(xb, wc_ref[:, :MLA_Q_RANK + MLA_KV_RANK])
    cq_ref[...] = _rms_rows(c[:, :MLA_Q_RANK], gq_ref[...]).astype(BF16)
    ckv_ref[...] = _rms_rows(c[:, MLA_Q_RANK:], gkv_ref[...]).astype(BF16)
    z_ref[...] = _silu(_dot(xb, wz_ref[:, :MLA_HV])).astype(BF16)
    a = _dot_nt(xb, wr_ref[...]) * _rope_table_t(pos_ref, freq_ref).T
    kr_ref[...] = (a + pltpu.roll(a, LANES // 2, axis=1)).astype(BF16)


def _mla_in_proj(x2, w_in, g_q, g_kv, pos_row):
    n = x2.shape[0]
    bm = MLA_IN_BM
    nc = MLA_Q_RANK + MLA_KV_RANK
    half = MLA_ROPE // 2
    wT = jnp.swapaxes(w_in, 0, 1)
    wc = _prep_transposed(wT, nc // PREP_BR, PREP_BR, 1, lambda j: j)
    wz = _prep_transposed(wT, MLA_HV // (4 * MLA_ROPE), MLA_ROPE, 4,
                          lambda j: nc // MLA_ROPE + 1 + 4 * j)
    wr = _prep_rows(wT, nc // MLA_ROPE, 1, MLA_ROPE)
    t1, t2 = wr[:half], wr[half:]
    wr4T = jnp.concatenate([t1, t2, t2, t1], axis=0)
    row = lambda i: (i, 0)
    return pl.pallas_call(
        _mla_in_kernel,
        grid=(n // bm,),
        in_specs=[pl.BlockSpec((bm, D_MODEL), row),
                  _resident(wc.shape),
                  _resident(wz.shape),
                  _resident((LANES, D_MODEL)),
                  _resident((1, MLA_Q_RANK)),
                  _resident((1, MLA_KV_RANK)),
                  pl.BlockSpec((1, bm), lambda i: (0, i)),
                  _resident((MLA_ROPE // 2, 1))],
        out_specs=[pl.BlockSpec((bm, MLA_Q_RANK), row),
                   pl.BlockSpec((bm, MLA_KV_RANK), row),
                   pl.BlockSpec((bm, MLA_HV), row),
                   pl.BlockSpec((bm, LANES), row)],
        out_shape=[jax.ShapeDtypeStruct((n, MLA_Q_RANK), BF16),
                   jax.ShapeDtypeStruct((n, MLA_KV_RANK), BF16),
                   jax.ShapeDtypeStruct((n, MLA_HV), BF16),
                   jax.ShapeDtypeStruct((n, LANES), BF16)],
        compiler_params=_params("arbitrary"),
        name="mla_in_proj",
    )(x2, wc, wz, wr4T, g_q.reshape(1, -1), g_kv.reshape(1, -1), pos_row, _rope_freqs())


def _mla_up_kernel(cq_ref, ckv_ref, wqT_ref, wk_ref, wvT_ref, pos_ref, freq_ref,
                   qT_ref, k_ref, vT_ref):
    cq = cq_ref[...]
    ckv = ckv_ref[...]
    scale = MLA_QK ** -0.5 * math.log2(math.e)
    tab = _rope_table_t(pos_ref, freq_ref) * scale
    heads = 4
    half = MLA_ROPE // 2
    for g in range(0, MLA_HEADS, heads):
        acc = _dot_nt(wqT_ref[g * MLA_QK:(g + heads) * MLA_QK, :], cq)
        for h in range(heads):
            a0 = h * MLA_QK
            r0 = (g + h) * QHEAD
            t1 = acc[a0 + MLA_NOPE:a0 + MLA_NOPE + half]
            t2 = acc[a0 + MLA_NOPE + half:a0 + MLA_QK]
            rot = jnp.concatenate([t1, t2, t2, t1], axis=0) * tab
            qT_ref[r0:r0 + LANES, :] = (acc[a0:a0 + MLA_NOPE] * scale).astype(BF16)
            qT_ref[r0 + LANES:r0 + QHEAD, :] = rot.astype(BF16)
    k_ref[...] = _dot(ckv, wk_ref[...]).astype(BF16)
    vT_ref[...] = _dot_nt(wvT_ref[...], ckv).astype(BF16)


def _mla_up_proj(cq, ckv, w_uq, w_ukv, pos_row):
    n = cq.shape[0]
    bm = UP_BM
    wqT = w_uq.astype(BF16).T
    wkv = w_ukv.astype(BF16).reshape(MLA_KV_RANK, MLA_HEADS, MLA_NOPE + MLA_V)
    wk = wkv[:, :, :MLA_NOPE].reshape(MLA_KV_RANK, MLA_HEADS * MLA_NOPE)
    wvT = wkv[:, :, MLA_NOPE:].reshape(MLA_KV_RANK, MLA_HV).T
    row = lambda i: (i, 0)
    col = lambda i: (0, i)
    return pl.pallas_call(
        _mla_up_kernel,
        grid=(n // bm,),
        in_specs=[pl.BlockSpec((bm, MLA_Q_RANK), row),
                  pl.BlockSpec((bm, MLA_KV_RANK), row),
                  _resident(wqT.shape),
                  _resident(wk.shape),
                  _resident(wvT.shape),
                  pl.BlockSpec((1, bm), col),
                  _resident((MLA_ROPE // 2, 1))],
        out_specs=[pl.BlockSpec((MLA_HEADS * QHEAD, bm), col),
                   pl.BlockSpec((bm, MLA_HEADS * MLA_NOPE), row),
                   pl.BlockSpec((MLA_HV, bm), col)],
        out_shape=[jax.ShapeDtypeStruct((MLA_HEADS * QHEAD, n), BF16),
                   jax.ShapeDtypeStruct((n, MLA_HEADS * MLA_NOPE), BF16),
                   jax.ShapeDtypeStruct((MLA_HV, n), BF16)],
        compiler_params=_params("arbitrary"),
        name="mla_up_proj",
    )(cq, ckv, wqT, wk, wvT, pos_row, _rope_freqs())


def _mla_attn_kernel(qT_ref, k_ref, kr_ref, vT_ref, z_ref, qTn_ref, kn_ref, krn_ref,
                     o_ref, s_ref, m_ref):
    t = ATT_T
    nt = qT_ref.shape[1] // t
    kk = lax.broadcasted_iota(jnp.int32, (t, t), 0) // CHUNK
    qq = lax.broadcasted_iota(jnp.int32, (t, t), 1) // CHUNK
    visible = kk <= qq
    ones = jnp.ones((BF16_ROWS, t), BF16)
    blocks = [(i, j) for i in range(nt) for j in range(i + 1)]

    def score_steps(slot, q_tile, k_block, kr):
        m = [None] * nt

        def score(idx, i, j):
            ks = slice(j * t, (j + 1) * t)
            kcat = jnp.concatenate([k_block(ks), kr[ks, :]], axis=1)
            s = _dot(kcat, q_tile(i))
            if j == i:
                s = jnp.where(visible, s, -jnp.inf)
            m_blk = jnp.max(s, axis=0, keepdims=True)
            m[i] = m_blk if m[i] is None else jnp.maximum(m[i], m_blk)
            s_ref[slot, idx] = s
            if j == i:
                m_ref[slot, i] = m[i]

        return [functools.partial(score, idx, i, j) for idx, (i, j) in enumerate(blocks)]

    def local_score_steps(h):
        return score_steps(
            h % 2,
            lambda i: qT_ref[h * QHEAD:(h + 1) * QHEAD, i * t:(i + 1) * t],
            lambda ks: k_ref[ks, h * MLA_NOPE:(h + 1) * MLA_NOPE],
            kr_ref)

    def value_steps(h):
        slot = h % 2
        acc = [None] * nt

        def value(idx, i, j):
            p = jnp.exp2(s_ref[slot, idx] - m_ref[slot, i]).astype(BF16)
            v_aug = jnp.concatenate([vT_ref[h * MLA_V:(h + 1) * MLA_V, j * t:(j + 1) * t],
                                     ones], axis=0)
            pv = _dot(v_aug, p)
            acc[i] = pv if acc[i] is None else acc[i] + pv
            if j == i:
                a = acc[i]
                o = (a[:MLA_V] * (1.0 / a[MLA_V:MLA_V + 1])).T
                rows = slice(i * t, (i + 1) * t)
                cols = slice(h * MLA_V, (h + 1) * MLA_V)
                o_ref[rows, cols] = (o * z_ref[rows, cols]).astype(BF16)

        return [functools.partial(value, idx, i, j) for idx, (i, j) in enumerate(blocks)]

    @pl.when((pl.program_id(0) == 0) & (pl.program_id(1) == 0))
    def _():
        for f in local_score_steps(0):
            f()

    assert ATT_HB % 2 == 0
    for h in range(ATT_HB):
        if h + 1 < ATT_HB:
            nxt = local_score_steps(h + 1)
        else:
            nxt = score_steps(0, lambda i: qTn_ref[:, i * t:(i + 1) * t],
                              lambda ks: kn_ref[ks, :], krn_ref)
        for f_score, f_value in zip(nxt, value_steps(h)):
            f_score()
            f_value()


def _mla_attention(qT, k, kr, vT, z, batch, seq):
    n = k.shape[0]
    hb = ATT_HB
    groups = MLA_HEADS // hb
    nt = seq // ATT_T
    n_blocks = nt * (nt + 1) // 2

    def next_step(b, g):
        s = jnp.minimum(b * groups + g + 1, batch * groups - 1)
        return s // groups, s % groups

    return pl.pallas_call(
        _mla_attn_kernel,
        grid=(batch, groups),
        in_specs=[pl.BlockSpec((hb * QHEAD, seq), lambda b, g: (g, b)),
                  pl.BlockSpec((seq, hb * MLA_NOPE), lambda b, g: (b, g)),
                  pl.BlockSpec((seq, LANES), lambda b, g: (b, 0)),
                  pl.BlockSpec((hb * MLA_V, seq), lambda b, g: (g, b)),
                  pl.BlockSpec((seq, hb * MLA_V), lambda b, g: (b, g)),
                  pl.BlockSpec((QHEAD, seq), lambda b, g: (hb * next_step(b, g)[1], next_step(b, g)[0])),
                  pl.BlockSpec((seq, MLA_NOPE), lambda b, g: (next_step(b, g)[0], hb * next_step(b, g)[1])),
                  pl.BlockSpec((seq, LANES), lambda b, g: (next_step(b, g)[0], 0))],
        out_specs=pl.BlockSpec((seq, hb * MLA_V), lambda b, g: (b, g)),
        out_shape=jax.ShapeDtypeStruct((n, MLA_HV), BF16),
        scratch_shapes=[pltpu.VMEM((2, n_blocks, ATT_T, ATT_T), F32),
                        pltpu.VMEM((2, nt, 1, ATT_T), F32)],
        compiler_params=_params("arbitrary", "arbitrary"),
        name="mla_attention",
    )(qT, k, kr, vT, z, qT, k, kr)


def kernel(x, positions, gla_w_in, gla_w_a2, gla_b_a, gla_g_out, gla_w_out,
           mla_w_in, mla_g_q, mla_w_uq, mla_g_kv, mla_w_ukv, mla_w_out, ln_g, ln_b):
    batch, seq, d = x.shape
    assert d == D_MODEL and seq % GLA_TC == 0 and seq % ATT_T == 0
    x2 = x.reshape(batch * seq, d)

    q, kdm, sz, vT, dec = _gla_in_proj(x2, gla_w_in[0], gla_w_a2[0], gla_b_a[0], gla_g_out[0])
    o = _gla_recurrence(q, kdm, vT, sz, dec, batch, seq)
    x2 = _out_proj_ln(o, x2, gla_w_out[0], ln_g[0], ln_b[0])

    pos_row = positions.reshape(1, batch * seq)
    cq, ckv, z, kr = _mla_in_proj(x2, mla_w_in[0], mla_g_q[0], mla_g_kv[0], pos_row)
    qT, kn, vT = _mla_up_proj(cq, ckv, mla_w_uq[0], mla_w_ukv[0], pos_row)
    o = _mla_attention(qT, kn, kr, vT, z, batch, seq)
    x2 = _out_proj_ln(o, x2, mla_w_out[0], ln_g[1], ln_b[1])
    return x2.reshape(batch, seq, d)
```

```python
import functools
import math

import jax
import jax.numpy as jnp
from jax import lax
from jax.experimental import pallas as pl
from jax.experimental.pallas import tpu as pltpu

BF16 = jnp.bfloat16
F32 = jnp.float32

D_MODEL = 2048
DEPTH = 2
CHUNK = 64
DEEPNORM_ALPHA = float((2 * DEPTH) ** 0.25)
LN_EPS = 1e-5
RMS_EPS = 1e-6

GLA_HEADS = 4
GLA_DK = 256
GLA_DV = 512
GLA_HK = GLA_HEADS * GLA_DK
GLA_HV = GLA_HEADS * GLA_DV
GLA_GATE_RANK = 16
GLA_TAU = 16.0

MLA_HEADS = 16
MLA_Q_RANK = 512
MLA_KV_RANK = 512
MLA_NOPE = 128
MLA_ROPE = 64
MLA_V = 128
MLA_QK = MLA_NOPE + MLA_ROPE
MLA_HV = MLA_HEADS * MLA_V
ROPE_THETA = 10000.0

LANES = 128
SUBLANES = 8
BF16_ROWS = 16
VMEM_LIMIT = 56 * 1024 * 1024

PROJ_BM = 256
OUT_BM = 512
OUT_SUB = 128
UP_BM = 1024
MLA_IN_BM = 512
GLA_TC = 1024
ATT_T = 256
ATT_HB = 4
PREP_BR = 256
WEIGHT_PAD = 2 * LANES
QHEAD = 2 * LANES


def _params(*sem):
    return pltpu.CompilerParams(dimension_semantics=sem, vmem_limit_bytes=VMEM_LIMIT)


def _resident(shape):
    zeros = (0,) * len(shape)
    return pl.BlockSpec(shape, lambda *_: zeros, pipeline_mode=pl.Buffered(1))


def _dot(a, b):
    return jnp.dot(a, b, preferred_element_type=F32)


def _dot_nt(a, b):
    return lax.dot_general(a, b, (((1,), (1,)), ((), ())), preferred_element_type=F32)


def _silu(z):
    return z * (1.0 / (1.0 + jnp.exp(-z)))


def _rms_rows(x, g):
    ms = jnp.mean(x * x, axis=-1, keepdims=True)
    return x * lax.rsqrt(ms + RMS_EPS) * g


def _prep_transpose_kernel(*refs, n_out, tail_rows, unused):
    *w_refs, o_ref = refs
    j = pl.program_id(0)
    is_unused = (j >= unused[0]) & (j < unused[1])

    def transposed():
        rows = [r[...] for r in w_refs]
        return (rows[0] if len(rows) == 1 else jnp.concatenate(rows, axis=0)).T

    @pl.when((j < n_out) & jnp.logical_not(is_unused))
    def _():
        o_ref[...] = transposed().astype(BF16)

    @pl.when(is_unused)
    def _():
        o_ref[...] = jnp.zeros(o_ref.shape, BF16)

    @pl.when(j >= n_out)
    def _():
        wt = transposed()
        col = lax.broadcasted_iota(jnp.int32, wt.shape, 1)
        o_ref[...] = jnp.where(col < tail_rows, wt, 0.0).astype(BF16)


def _prep_transposed(wT, n_out, br, parts, first_block, tail_rows=0, unused=(0, 0)):
    d = wT.shape[1]
    assert br * parts == WEIGHT_PAD and (tail_rows == 0 or parts == 1)
    last = n_out if tail_rows else n_out - 1

    def src_block(k, j):
        j = jnp.minimum(j, last)
        j = jnp.where((j >= unused[0]) & (j < unused[1]), max(unused[0] - 1, 0), j)
        return first_block(j) + k, 0

    return pl.pallas_call(
        functools.partial(_prep_transpose_kernel, n_out=n_out, tail_rows=tail_rows, unused=unused),
        grid=(n_out + 1,),
        in_specs=[pl.BlockSpec((br, d), functools.partial(src_block, k)) for k in range(parts)],
        out_specs=pl.BlockSpec((d, br * parts), lambda j: (0, j)),
        out_shape=jax.ShapeDtypeStruct((d, (n_out + 1) * br * parts), BF16),
        compiler_params=_params("arbitrary"),
        name="prep_transposed",
    )(*([wT] * parts))


def _prep_rows_kernel(w_ref, o_ref):
    d = w_ref.shape[1]
    o_ref[:, :d] = w_ref[...].astype(BF16)
    if o_ref.shape[1] > d:
        o_ref[:, d:] = jnp.zeros((o_ref.shape[0], o_ref.shape[1] - d), BF16)


def _prep_rows(wT, block0, n_blocks, br, pad=0):
    d = wT.shape[1]
    return pl.pallas_call(
        _prep_rows_kernel,
        grid=(n_blocks,),
        in_specs=[pl.BlockSpec((br, d), lambda j: (block0 + j, 0))],
        out_specs=pl.BlockSpec((br, d + pad), lambda j: (j, 0)),
        out_shape=jax.ShapeDtypeStruct((n_blocks * br, d + pad), BF16),
        compiler_params=_params("arbitrary"),
        name="prep_rows",
    )(wT)


def _rope_table_t(pos_ref, freq_ref):
    pos = pos_ref[...].astype(F32)
    ang = freq_ref[...] * pos
    c = jnp.cos(ang)
    s = jnp.sin(ang)
    return jnp.concatenate([c, c, -s, s], axis=0)


def _rope_freqs():
    half = MLA_ROPE // 2
    return (ROPE_THETA ** (-jnp.arange(half, dtype=F32) / half)).reshape(half, 1)


def _chunk_cumsum(la):
    t, n = la.shape
    x = la.reshape(t // SUBLANES, SUBLANES, n)
    row = lax.broadcasted_iota(jnp.int32, x.shape, 1)
    s = 1
    while s < SUBLANES:
        x = x + jnp.where(row >= s, pltpu.roll(x, s, axis=1), 0.0)
        s *= 2
    tiles_per_chunk = CHUNK // SUBLANES
    out = []
    carry = None
    for i in range(t // SUBLANES):
        tile = x[i]
        if i % tiles_per_chunk != 0:
            tile = tile + carry
        carry = tile[SUBLANES - 1:SUBLANES, :]
        out.append(tile)
    return jnp.concatenate(out, axis=0)


def _gla_in_kernel(x_ref, w_ref, wvT_ref, wa2_ref, ba_ref, g_ref,
                   q_ref, kdm_ref, sz_ref, vT_ref, dec_ref):
    bm = x_ref.shape[0]
    z0 = 2 * GLA_HK + GLA_HV
    xb = x_ref[...].astype(BF16)
    q_ref[...] = _dot(xb, w_ref[:, :GLA_HK]).astype(BF16)
    za = _dot(xb, w_ref[:, z0:])
    z = za[:, :GLA_HV]
    a = za[:, GLA_HV:GLA_HV + LANES].astype(BF16)
    sz_ref[...] = (_silu(z) * g_ref[...]).astype(BF16)
    vT_ref[...] = _dot_nt(wvT_ref[:, :D_MODEL], xb).astype(BF16)

    pre = _dot(a, wa2_ref[...]) + ba_ref[...]
    la = (jnp.minimum(pre, 0.0) - jnp.log(1.0 + jnp.exp(-jnp.abs(pre)))) * (1.0 / GLA_TAU)
    cum = _chunk_cumsum(la)
    chunks = bm // CHUNK
    tots = [cum[(c + 1) * CHUNK - 1:(c + 1) * CHUNK, :] for c in range(chunks)]
    dec_ref[0] = jnp.exp(jnp.concatenate(tots, axis=0))
    tot_rows = jnp.concatenate([jnp.broadcast_to(t, (CHUNK, GLA_HK)) for t in tots], axis=0)
    kd = (_dot(xb, w_ref[:, GLA_HK:2 * GLA_HK]) * jnp.exp(tot_rows - cum)).astype(BF16)

    zeros = jnp.zeros((CHUNK, GLA_DK), BF16)
    for c in range(chunks):
        rows = slice(c * CHUNK, (c + 1) * CHUNK)
        for h in range(GLA_HEADS):
            c0 = h * 2 * GLA_DK
            mine = slice(c0 + (c % 2) * GLA_DK, c0 + (c % 2 + 1) * GLA_DK)
            other = slice(c0 + (1 - c % 2) * GLA_DK, c0 + (2 - c % 2) * GLA_DK)
            kdm_ref[rows, mine] = kd[rows, h * GLA_DK:(h + 1) * GLA_DK]
            kdm_ref[rows, other] = zeros


def _gla_in_proj(x2, w_in, w_a2, b_a, g_out):
    n = x2.shape[0]
    bm = PROJ_BM
    br = PREP_BR
    wT = jnp.swapaxes(w_in, 0, 1)
    a0 = 2 * GLA_HK + 2 * GLA_HV
    v_blocks = (2 * GLA_HK // br, (2 * GLA_HK + GLA_HV) // br)
    wb = _prep_transposed(wT, a0 // br, br, 1, lambda j: j, tail_rows=GLA_GATE_RANK,
                          unused=v_blocks)
    wvT = _prep_rows(wT, 2 * GLA_HK // br, GLA_HV // br, br, WEIGHT_PAD)
    wa2_p = jnp.pad(w_a2, ((0, LANES - GLA_GATE_RANK), (0, 0))).astype(BF16)
    row = lambda i: (i, 0)
    return pl.pallas_call(
        _gla_in_kernel,
        grid=(n // bm,),
        in_specs=[pl.BlockSpec((bm, D_MODEL), row),
                  _resident(wb.shape),
                  _resident(wvT.shape),
                  _resident((LANES, GLA_HK)),
                  _resident((1, GLA_HK)),
                  _resident((1, GLA_HV))],
        out_specs=[pl.BlockSpec((bm, GLA_HK), row),
                   pl.BlockSpec((bm, 2 * GLA_HK), row),
                   pl.BlockSpec((bm, GLA_HV), row),
                   pl.BlockSpec((GLA_HV, bm), lambda i: (0, i)),
                   pl.BlockSpec((1, bm // CHUNK, GLA_HK), lambda i: (i, 0, 0))],
        out_shape=[jax.ShapeDtypeStruct((n, GLA_HK), BF16),
                   jax.ShapeDtypeStruct((n, 2 * GLA_HK), BF16),
                   jax.ShapeDtypeStruct((n, GLA_HV), BF16),
                   jax.ShapeDtypeStruct((GLA_HV, n), BF16),
                   jax.ShapeDtypeStruct((n // bm, bm // CHUNK, GLA_HK), F32)],
        compiler_params=_params("arbitrary"),
        name="gla_in_proj",
    )(x2, wb, wvT, wa2_p, b_a.reshape(1, GLA_HK), g_out.reshape(1, GLA_HV))


def _gla_rec_kernel(q_ref, kdm_ref, vT_ref, sz_ref, dec_ref, o_ref, st_ref):
    @pl.when(pl.program_id(1) == 0)
    def _():
        st_ref[...] = jnp.zeros_like(st_ref)

    pair = 2 * CHUNK
    chunks_per_block = dec_ref.shape[1]
    eps = RMS_EPS * GLA_DK
    units = [(p, h) for p in range(GLA_TC // pair) for h in range(GLA_HEADS)]

    def pair_update(p, h):
        return _dot(vT_ref[h * GLA_DV:(h + 1) * GLA_DV, p * pair:(p + 1) * pair],
                    kdm_ref[p * pair:(p + 1) * pair, h * 2 * GLA_DK:(h + 1) * 2 * GLA_DK])

    kv_next = pair_update(*units[0])
    for n, (p, h) in enumerate(units):
        kv2 = kv_next
        if n + 1 < len(units):
            kv_next = pair_update(*units[n + 1])
        ks = slice(h * GLA_DK, (h + 1) * GLA_DK)
        vs = slice(h * GLA_DV, (h + 1) * GLA_DV)
        for e in range(2):
            c = 2 * p + e
            rows = slice(c * CHUNK, (c + 1) * CHUNK)
            dec = dec_ref[c // chunks_per_block, c % chunks_per_block:c % chunks_per_block + 1, ks]
            s_new = st_ref[h] * dec + kv2[:, e * GLA_DK:(e + 1) * GLA_DK]
            st_ref[h] = s_new
            o = _dot_nt(q_ref[rows, ks], s_new.astype(BF16))
            ms = jnp.mean(o * o, axis=-1, keepdims=True)
            o_ref[rows, vs] = (o * lax.rsqrt(ms + eps) * sz_ref[rows, vs]).astype(BF16)


def _gla_recurrence(q, kdm, vT, sz, dec, batch, seq):
    n = q.shape[0]
    tc = GLA_TC
    steps = seq // tc
    blocks = tc // PROJ_BM
    row = lambda b, t: (b * steps + t, 0)
    return pl.pallas_call(
        _gla_rec_kernel,
        grid=(batch, steps),
        in_specs=[pl.BlockSpec((tc, GLA_HK), row),
                  pl.BlockSpec((tc, 2 * GLA_HK), row),
                  pl.BlockSpec((GLA_HV, tc), lambda b, t: (0, b * steps + t)),
                  pl.BlockSpec((tc, GLA_HV), row),
                  pl.BlockSpec((blocks, dec.shape[1], GLA_HK), lambda b, t: (b * steps + t, 0, 0))],
        out_specs=pl.BlockSpec((tc, GLA_HV), row),
        out_shape=jax.ShapeDtypeStruct((n, GLA_HV), BF16),
        scratch_shapes=[pltpu.VMEM((GLA_HEADS, GLA_DV, GLA_DK), F32)],
        compiler_params=_params("arbitrary", "arbitrary"),
        name="gla_recurrence",
    )(q, kdm, vT, sz, dec)


def _out_ln_kernel(o_ref, x_ref, w_ref, g_ref, b_ref, y_ref, wb_ref):
    sub = OUT_SUB
    nsub = o_ref.shape[0] // sub

    @pl.when(pl.program_id(0) == 0)
    def _():
        wb_ref[...] = w_ref[...].astype(BF16)

    def proj(r):
        return _dot(o_ref[r * sub:(r + 1) * sub, :], wb_ref[...])

    y_next = proj(0)
    for r in range(nsub):
        rows = slice(r * sub, (r + 1) * sub)
        y = y_next
        if r + 1 < nsub:
            y_next = proj(r + 1)
        v = DEEPNORM_ALPHA * x_ref[rows, :] + y
        mu = jnp.mean(v, axis=-1, keepdims=True)
        vc = v - mu
        var = jnp.mean(vc * vc, axis=-1, keepdims=True)
        y_ref[rows, :] = vc * lax.rsqrt(var + LN_EPS) * g_ref[...] + b_ref[...]


def _out_proj_ln(o, x2, w_out, g, b):
    n = x2.shape[0]
    bm = OUT_BM
    return pl.pallas_call(
        _out_ln_kernel,
        grid=(n // bm,),
        in_specs=[pl.BlockSpec((bm, o.shape[1]), lambda i: (i, 0)),
                  pl.BlockSpec((bm, D_MODEL), lambda i: (i, 0)),
                  _resident((o.shape[1], D_MODEL)),
                  _resident((1, D_MODEL)),
                  _resident((1, D_MODEL))],
        out_specs=pl.BlockSpec((bm, D_MODEL), lambda i: (i, 0)),
        out_shape=jax.ShapeDtypeStruct((n, D_MODEL), F32),
        scratch_shapes=[pltpu.VMEM((o.shape[1], D_MODEL), BF16)],
        compiler_params=_params("arbitrary"),
        name="out_proj_ln",
    )(o, x2, w_out, g.reshape(1, D_MODEL), b.reshape(1, D_MODEL))


def _mla_in_kernel(x_ref, wc_ref, wz_ref, wr_ref, gq_ref, gkv_ref, pos_ref, freq_ref,
                   cq_ref, ckv_ref, z_ref, kr_ref):
    xb = x_ref[...].astype(BF16)
    c = _dot(xb, wc_ref[:, :MLA_Q_RANK + MLA_KV_RANK])
    cq_ref[...] = _rms_rows(c[:, :MLA_Q_RANK], gq_ref[...]).astype(BF16)
    ckv_ref[...] = _rms_rows(c[:, MLA_Q_RANK:], gkv_ref[...]).astype(BF16)
    z_ref[...] = _silu(_dot(xb, wz_ref[:, :MLA_HV])).astype(BF16)
    a = _dot_nt(xb, wr_ref[...]) * _rope_table_t(pos_ref, freq_ref).T
    kr_ref[...] = (a + pltpu.roll(a, LANES // 2, axis=1)).astype(BF16)


def _mla_in_proj(x2, w_in, g_q, g_kv, pos_row):
    n = x2.shape[0]
    bm = MLA_IN_BM
    nc = MLA_Q_RANK + MLA_KV_RANK
    half = MLA_ROPE // 2
    wT = jnp.swapaxes(w_in, 0, 1)
    wc = _prep_transposed(wT, nc // PREP_BR, PREP_BR, 1, lambda j: j)
    wz = _prep_transposed(wT, MLA_HV // (4 * MLA_ROPE), MLA_ROPE, 4,
                          lambda j: nc // MLA_ROPE + 1 + 4 * j)
    wr = _prep_rows(wT, nc // MLA_ROPE, 1, MLA_ROPE)
    t1, t2 = wr[:half], wr[half:]
    wr4T = jnp.concatenate([t1, t2, t2, t1], axis=0)
    row = lambda i: (i, 0)
    return pl.pallas_call(
        _mla_in_kernel,
        grid=(n // bm,),
        in_specs=[pl.BlockSpec((bm, D_MODEL), row),
                  _resident(wc.shape),
                  _resident(wz.shape),
                  _resident((LANES, D_MODEL)),
                  _resident((1, MLA_Q_RANK)),
                  _resident((1, MLA_KV_RANK)),
                  pl.BlockSpec((1, bm), lambda i: (0, i)),
                  _resident((MLA_ROPE // 2, 1))],
        out_specs=[pl.BlockSpec((bm, MLA_Q_RANK), row),
                   pl.BlockSpec((bm, MLA_KV_RANK), row),
                   pl.BlockSpec((bm, MLA_HV), row),
                   pl.BlockSpec((bm, LANES), row)],
        out_shape=[jax.ShapeDtypeStruct((n, MLA_Q_RANK), BF16),
                   jax.ShapeDtypeStruct((n, MLA_KV_RANK), BF16),
                   jax.ShapeDtypeStruct((n, MLA_HV), BF16),
                   jax.ShapeDtypeStruct((n, LANES), BF16)],
        compiler_params=_params("arbitrary"),
        name="mla_in_proj",
    )(x2, wc, wz, wr4T, g_q.reshape(1, -1), g_kv.reshape(1, -1), pos_row, _rope_freqs())


def _mla_up_kernel(cq_ref, ckv_ref, wqT_ref, wk_ref, wvT_ref, pos_ref, freq_ref,
                   qT_ref, k_ref, vT_ref):
    cq = cq_ref[...]
    ckv = ckv_ref[...]
    scale = MLA_QK ** -0.5 * math.log2(math.e)
    tab = _rope_table_t(pos_ref, freq_ref) * scale
    heads = 4
    half = MLA_ROPE // 2
    for g in range(0, MLA_HEADS, heads):
        acc = _dot_nt(wqT_ref[g * MLA_QK:(g + heads) * MLA_QK, :], cq)
        for h in range(heads):
            a0 = h * MLA_QK
            r0 = (g + h) * QHEAD
            t1 = acc[a0 + MLA_NOPE:a0 + MLA_NOPE + half]
            t2 = acc[a0 + MLA_NOPE + half:a0 + MLA_QK]
            rot = jnp.concatenate([t1, t2, t2, t1], axis=0) * tab
            qT_ref[r0:r0 + LANES, :] = (acc[a0:a0 + MLA_NOPE] * scale).astype(BF16)
            qT_ref[r0 + LANES:r0 + QHEAD, :] = rot.astype(BF16)
    k_ref[...] = _dot(ckv, wk_ref[...]).astype(BF16)
    vT_ref[...] = _dot_nt(wvT_ref[...], ckv).astype(BF16)


def _mla_up_proj(cq, ckv, w_uq, w_ukv, pos_row):
    n = cq.shape[0]
    bm = UP_BM
    wqT = w_uq.astype(BF16).T
    wkv = w_ukv.astype(BF16).reshape(MLA_KV_RANK, MLA_HEADS, MLA_NOPE + MLA_V)
    wk = wkv[:, :, :MLA_NOPE].reshape(MLA_KV_RANK, MLA_HEADS * MLA_NOPE)
    wvT = wkv[:, :, MLA_NOPE:].reshape(MLA_KV_RANK, MLA_HV).T
    row = lambda i: (i, 0)
    col = lambda i: (0, i)
    return pl.pallas_call(
        _mla_up_kernel,
        grid=(n // bm,),
        in_specs=[pl.BlockSpec((bm, MLA_Q_RANK), row),
                  pl.BlockSpec((bm, MLA_KV_RANK), row),
                  _resident(wqT.shape),
                  _resident(wk.shape),
                  _resident(wvT.shape),
                  pl.BlockSpec((1, bm), col),
                  _resident((MLA_ROPE // 2, 1))],
        out_specs=[pl.BlockSpec((MLA_HEADS * QHEAD, bm), col),
                   pl.BlockSpec((bm, MLA_HEADS * MLA_NOPE), row),
                   pl.BlockSpec((MLA_HV, bm), col)],
        out_shape=[jax.ShapeDtypeStruct((MLA_HEADS * QHEAD, n), BF16),
                   jax.ShapeDtypeStruct((n, MLA_HEADS * MLA_NOPE), BF16),
                   jax.ShapeDtypeStruct((MLA_HV, n), BF16)],
        compiler_params=_params("arbitrary"),
        name="mla_up_proj",
    )(cq, ckv, wqT, wk, wvT, pos_row, _rope_freqs())


def _mla_attn_kernel(qT_ref, k_ref, kr_ref, vT_ref, z_ref, qTn_ref, kn_ref, krn_ref,
                     o_ref, s_ref, m_ref):
    t = ATT_T
    nt = qT_ref.shape[1] // t
    kk = lax.broadcasted_iota(jnp.int32, (t, t), 0) // CHUNK
    qq = lax.broadcasted_iota(jnp.int32, (t, t), 1) // CHUNK
    visible = kk <= qq
    ones = jnp.ones((BF16_ROWS, t), BF16)
    blocks = [(i, j) for i in range(nt) for j in range(i + 1)]

    def score_steps(slot, q_tile, k_block, kr):
        m = [None] * nt

        def score(idx, i, j):
            ks = slice(j * t, (j + 1) * t)
            kcat = jnp.concatenate([k_block(ks), kr[ks, :]], axis=1)
            s = _dot(kcat, q_tile(i))
            if j == i:
                s = jnp.where(visible, s, -jnp.inf)
            m_blk = jnp.max(s, axis=0, keepdims=True)
            m[i] = m_blk if m[i] is None else jnp.maximum(m[i], m_blk)
            s_ref[slot, idx] = s
            if j == i:
                m_ref[slot, i] = m[i]

        return [functools.partial(score, idx, i, j) for idx, (i, j) in enumerate(blocks)]

    def local_score_steps(h):
        return score_steps(
            h % 2,
            lambda i: qT_ref[h * QHEAD:(h + 1) * QHEAD, i * t:(i + 1) * t],
            lambda ks: k_ref[ks, h * MLA_NOPE:(h + 1) * MLA_NOPE],
            kr_ref)

    def value_steps(h):
        slot = h % 2
        acc = [None] * nt

        def value(idx, i, j):
            p = jnp.exp2(s_ref[slot, idx] - m_ref[slot, i]).astype(BF16)
            v_aug = jnp.concatenate([vT_ref[h * MLA_V:(h + 1) * MLA_V, j * t:(j + 1) * t],
                                     ones], axis=0)
            pv = _dot(v_aug, p)
            acc[i] = pv if acc[i] is None else acc[i] + pv
            if j == i:
                a = acc[i]
                o = (a[:MLA_V] * (1.0 / a[MLA_V:MLA_V + 1])).T
                rows = slice(i * t, (i + 1) * t)
                cols = slice(h * MLA_V, (h + 1) * MLA_V)
                o_ref[rows, cols] = (o * z_ref[rows, cols]).astype(BF16)

        return [functools.partial(value, idx, i, j) for idx, (i, j) in enumerate(blocks)]

    @pl.when((pl.program_id(0) == 0) & (pl.program_id(1) == 0))
    def _():
        for f in local_score_steps(0):
            f()

    assert ATT_HB % 2 == 0
    for h in range(ATT_HB):
        if h + 1 < ATT_HB:
            nxt = local_score_steps(h + 1)
        else:
            nxt = score_steps(0, lambda i: qTn_ref[:, i * t:(i + 1) * t],
                              lambda ks: kn_ref[ks, :], krn_ref)
        for f_score, f_value in zip(nxt, value_steps(h)):
            f_score()
            f_value()


def _mla_attention(qT, k, kr, vT, z, batch, seq):
    n = k.shape[0]
    hb = ATT_HB
    groups = MLA_HEADS // hb
    nt = seq // ATT_T
    n_blocks = nt * (nt + 1) // 2

    def next_step(b, g):
        s = jnp.minimum(b * groups + g + 1, batch * groups - 1)
        return s // groups, s % groups

    return pl.pallas_call(
        _mla_attn_kernel,
        grid=(batch, groups),
        in_specs=[pl.BlockSpec((hb * QHEAD, seq), lambda b, g: (g, b)),
                  pl.BlockSpec((seq, hb * MLA_NOPE), lambda b, g: (b, g)),
                  pl.BlockSpec((seq, LANES), lambda b, g: (b, 0)),
                  pl.BlockSpec((hb * MLA_V, seq), lambda b, g: (g, b)),
                  pl.BlockSpec((seq, hb * MLA_V), lambda b, g: (b, g)),
                  pl.BlockSpec((QHEAD, seq), lambda b, g: (hb * next_step(b, g)[1], next_step(b, g)[0])),
                  pl.BlockSpec((seq, MLA_NOPE), lambda b, g: (next_step(b, g)[0], hb * next_step(b, g)[1])),
                  pl.BlockSpec((seq, LANES), lambda b, g: (next_step(b, g)[0], 0))],
        out_specs=pl.BlockSpec((seq, hb * MLA_V), lambda b, g: (b, g)),
        out_shape=jax.ShapeDtypeStruct((n, MLA_HV), BF16),
        scratch_shapes=[pltpu.VMEM((2, n_blocks, ATT_T, ATT_T), F32),
                        pltpu.VMEM((2, nt, 1, ATT_T), F32)],
        compiler_params=_params("arbitrary", "arbitrary"),
        name="mla_attention",
    )(qT, k, kr, vT, z, qT, k, kr)


def kernel(x, positions, gla_w_in, gla_w_a2, gla_b_a, gla_g_out, gla_w_out,
           mla_w_in, mla_g_q, mla_w_uq, mla_g_kv, mla_w_ukv, mla_w_out, ln_g, ln_b):
    batch, seq, d = x.shape
    assert d == D_MODEL and seq % GLA_TC == 0 and seq % ATT_T == 0
    x2 = x.reshape(batch * seq, d)

    q, kdm, sz, vT, dec = _gla_in_proj(x2, gla_w_in[0], gla_w_a2[0], gla_b_a[0], gla_g_out[0])
    o = _gla_recurrence(q, kdm, vT, sz, dec, batch, seq)
    x2 = _out_proj_ln(o, x2, gla_w_out[0], ln_g[0], ln_b[0])

    pos_row = positions.reshape(1, batch * seq)
    cq, ckv, z, kr = _mla_in_proj(x2, mla_w_in[0], mla_g_q[0], mla_g_kv[0], pos_row)
    qT, kn, vT = _mla_up_proj(cq, ckv, mla_w_uq[0], mla_w_ukv[0], pos_row)
    o = _mla_attention(qT, kn, kr, vT, z, batch, seq)
    x2 = _out_proj_ln(o, x2, mla_w_out[0], ln_g[1], ln_b[1])
    return x2.reshape(batch, seq, d)
```

```python
import functools
import math

import jax
import jax.numpy as jnp
from jax import lax
from jax.experimental import pallas as pl
from jax.experimental.pallas import tpu as pltpu

BF16 = jnp.bfloat16
F32 = jnp.float32

D_MODEL = 2048
DEPTH = 2
CHUNK = 64
DEEPNORM_ALPHA = float((2 * DEPTH) ** 0.25)
LN_EPS = 1e-5
RMS_EPS = 1e-6

GLA_HEADS = 4
GLA_DK = 256
GLA_DV = 512
GLA_HK = GLA_HEADS * GLA_DK
GLA_HV = GLA_HEADS * GLA_DV
GLA_GATE_RANK = 16
GLA_TAU = 16.0

MLA_HEADS = 16
MLA_Q_RANK = 512
MLA_KV_RANK = 512
MLA_NOPE = 128
MLA_ROPE = 64
MLA_V = 128
MLA_QK = MLA_NOPE + MLA_ROPE
MLA_HV = MLA_HEADS * MLA_V
ROPE_THETA = 10000.0

LANES = 128
SUBLANES = 8
BF16_ROWS = 16
VMEM_LIMIT = 56 * 1024 * 1024

PROJ_BM = 256
OUT_BM = 512
OUT_SUB = 128
UP_BM = 1024
MLA_IN_BM = 512
GLA_TC = 1024
ATT_T = 256
ATT_HB = 4
PREP_BR = 256
WEIGHT_PAD = 2 * LANES
QHEAD = 2 * LANES


def _params(*sem):
    return pltpu.CompilerParams(dimension_semantics=sem, vmem_limit_bytes=VMEM_LIMIT)


def _resident(shape):
    zeros = (0,) * len(shape)
    return pl.BlockSpec(shape, lambda *_: zeros, pipeline_mode=pl.Buffered(1))


def _dot(a, b):
    return jnp.dot(a, b, preferred_element_type=F32)


def _dot_nt(a, b):
    return lax.dot_general(a, b, (((1,), (1,)), ((), ())), preferred_element_type=F32)


def _silu(z):
    return z * (1.0 / (1.0 + jnp.exp(-z)))


def _rms_rows(x, g):
    ms = jnp.mean(x * x, axis=-1, keepdims=True)
    return x * lax.rsqrt(ms + RMS_EPS) * g


def _prep_transpose_kernel(*refs, n_out, tail_rows, unused):
    *w_refs, o_ref = refs
    j = pl.program_id(0)
    is_unused = (j >= unused[0]) & (j < unused[1])

    def transposed():
        rows = [r[...] for r in w_refs]
        return (rows[0] if len(rows) == 1 else jnp.concatenate(rows, axis=0)).T

    @pl.when((j < n_out) & jnp.logical_not(is_unused))
    def _():
        o_ref[...] = transposed().astype(BF16)

    @pl.when(is_unused)
    def _():
        o_ref[...] = jnp.zeros(o_ref.shape, BF16)

    @pl.when(j >= n_out)
    def _():
        wt = transposed()
        col = lax.broadcasted_iota(jnp.int32, wt.shape, 1)
        o_ref[...] = jnp.where(col < tail_rows, wt, 0.0).astype(BF16)


def _prep_transposed(wT, n_out, br, parts, first_block, tail_rows=0, unused=(0, 0)):
    d = wT.shape[1]
    assert br * parts == WEIGHT_PAD and (tail_rows == 0 or parts == 1)
    last = n_out if tail_rows else n_out - 1

    def src_block(k, j):
        j = jnp.minimum(j, last)
        j = jnp.where((j >= unused[0]) & (j < unused[1]), max(unused[0] - 1, 0), j)
        return first_block(j) + k, 0

    return pl.pallas_call(
        functools.partial(_prep_transpose_kernel, n_out=n_out, tail_rows=tail_rows, unused=unused),
        grid=(n_out + 1,),
        in_specs=[pl.BlockSpec((br, d), functools.partial(src_block, k)) for k in range(parts)],
        out_specs=pl.BlockSpec((d, br * parts), lambda j: (0, j)),
        out_shape=jax.ShapeDtypeStruct((d, (n_out + 1) * br * parts), BF16),
        compiler_params=_params("arbitrary"),
        name="prep_transposed",
    )(*([wT] * parts))


def _prep_rows_kernel(w_ref, o_ref):
    d = w_ref.shape[1]
    o_ref[:, :d] = w_ref[...].astype(BF16)
    if o_ref.shape[1] > d:
        o_ref[:, d:] = jnp.zeros((o_ref.shape[0], o_ref.shape[1] - d), BF16)


def _prep_rows(wT, block0, n_blocks, br, pad=0):
    d = wT.shape[1]
    return pl.pallas_call(
        _prep_rows_kernel,
        grid=(n_blocks,),
        in_specs=[pl.BlockSpec((br, d), lambda j: (block0 + j, 0))],
        out_specs=pl.BlockSpec((br, d + pad), lambda j: (j, 0)),
        out_shape=jax.ShapeDtypeStruct((n_blocks * br, d + pad), BF16),
        compiler_params=_params("arbitrary"),
        name="prep_rows",
    )(wT)


def _rope_table_t(pos_ref, freq_ref):
    pos = pos_ref[...].astype(F32)
    ang = freq_ref[...] * pos
    c = jnp.cos(ang)
    s = jnp.sin(ang)
    return jnp.concatenate([c, c, -s, s], axis=0)


def _rope_freqs():
    half = MLA_ROPE // 2
    return (ROPE_THETA ** (-jnp.arange(half, dtype=F32) / half)).reshape(half, 1)


def _chunk_cumsum(la):
    t, n = la.shape
    x = la.reshape(t // SUBLANES, SUBLANES, n)
    row = lax.broadcasted_iota(jnp.int32, x.shape, 1)
    s = 1
    while s < SUBLANES:
        x = x + jnp.where(row >= s, pltpu.roll(x, s, axis=1), 0.0)
        s *= 2
    tiles_per_chunk = CHUNK // SUBLANES
    out = []
    carry = None
    for i in range(t // SUBLANES):
        tile = x[i]
        if i % tiles_per_chunk != 0:
            tile = tile + carry
        carry = tile[SUBLANES - 1:SUBLANES, :]
        out.append(tile)
    return jnp.concatenate(out, axis=0)


def _gla_in_kernel(x_ref, w_ref, wvT_ref, wa2_ref, ba_ref, g_ref,
                   q_ref, kd_ref, sz_ref, vT_ref, dec_ref):
    bm = x_ref.shape[0]
    z0 = 2 * GLA_HK + GLA_HV
    xb = x_ref[...].astype(BF16)
    q_ref[...] = _dot(xb, w_ref[:, :GLA_HK]).astype(BF16)
    za = _dot(xb, w_ref[:, z0:])
    z = za[:, :GLA_HV]
    a = za[:, GLA_HV:GLA_HV + LANES].astype(BF16)
    sz_ref[...] = (_silu(z) * g_ref[...]).astype(BF16)
    vT_ref[...] = _dot_nt(wvT_ref[:, :D_MODEL], xb).astype(BF16)

    pre = _dot(a, wa2_ref[...]) + ba_ref[...]
    la = (jnp.minimum(pre, 0.0) - jnp.log(1.0 + jnp.exp(-jnp.abs(pre)))) * (1.0 / GLA_TAU)
    cum = _chunk_cumsum(la)
    chunks = bm // CHUNK
    tots = [cum[(c + 1) * CHUNK - 1:(c + 1) * CHUNK, :] for c in range(chunks)]
    dec_ref[0] = jnp.exp(jnp.concatenate(tots, axis=0))
    tot_rows = jnp.concatenate([jnp.broadcast_to(t, (CHUNK, GLA_HK)) for t in tots], axis=0)
    kd_ref[...] = (_dot(xb, w_ref[:, GLA_HK:2 * GLA_HK]) * jnp.exp(tot_rows - cum)).astype(BF16)


def _gla_in_proj(x2, w_in, w_a2, b_a, g_out):
    n = x2.shape[0]
    bm = PROJ_BM
    br = PREP_BR
    wT = jnp.swapaxes(w_in, 0, 1)
    a0 = 2 * GLA_HK + 2 * GLA_HV
    v_blocks = (2 * GLA_HK // br, (2 * GLA_HK + GLA_HV) // br)
    wb = _prep_transposed(wT, a0 // br, br, 1, lambda j: j, tail_rows=GLA_GATE_RANK,
                          unused=v_blocks)
    wvT = _prep_rows(wT, 2 * GLA_HK // br, GLA_HV // br, br, WEIGHT_PAD)
    wa2_p = jnp.pad(w_a2, ((0, LANES - GLA_GATE_RANK), (0, 0))).astype(BF16)
    row = lambda i: (i, 0)
    return pl.pallas_call(
        _gla_in_kernel,
        grid=(n // bm,),
        in_specs=[pl.BlockSpec((bm, D_MODEL), row),
                  _resident(wb.shape),
                  _resident(wvT.shape),
                  _resident((LANES, GLA_HK)),
                  _resident((1, GLA_HK)),
                  _resident((1, GLA_HV))],
        out_specs=[pl.BlockSpec((bm, GLA_HK), row),
                   pl.BlockSpec((bm, GLA_HK), row),
                   pl.BlockSpec((bm, GLA_HV), row),
                   pl.BlockSpec((GLA_HV, bm), lambda i: (0, i)),
                   pl.BlockSpec((1, bm // CHUNK, GLA_HK), lambda i: (i, 0, 0))],
        out_shape=[jax.ShapeDtypeStruct((n, GLA_HK), BF16),
                   jax.ShapeDtypeStruct((n, GLA_HK), BF16),
                   jax.ShapeDtypeStruct((n, GLA_HV), BF16),
                   jax.ShapeDtypeStruct((GLA_HV, n), BF16),
                   jax.ShapeDtypeStruct((n // bm, bm // CHUNK, GLA_HK), F32)],
        compiler_params=_params("arbitrary"),
        name="gla_in_proj",
    )(x2, wb, wvT, wa2_p, b_a.reshape(1, GLA_HK), g_out.reshape(1, GLA_HV))


def _gla_rec_kernel(q_ref, kd_ref, vT_ref, sz_ref, dec_ref, o_ref, st_ref):
    @pl.when(pl.program_id(1) == 0)
    def _():
        st_ref[...] = jnp.zeros_like(st_ref)

    pair = 2 * CHUNK
    chunks_per_block = dec_ref.shape[1]
    eps = RMS_EPS * GLA_DK
    units = [(p, h) for p in range(GLA_TC // pair) for h in range(GLA_HEADS)]

    zeros = jnp.zeros((CHUNK, GLA_DK), BF16)

    def pair_update(p, h):
        r0 = p * pair
        ks = slice(h * GLA_DK, (h + 1) * GLA_DK)
        even = jnp.concatenate([kd_ref[r0:r0 + CHUNK, ks], zeros], axis=0)
        odd = jnp.concatenate([zeros, kd_ref[r0 + CHUNK:r0 + pair, ks]], axis=0)
        return _dot(vT_ref[h * GLA_DV:(h + 1) * GLA_DV, r0:r0 + pair],
                    jnp.concatenate([even, odd], axis=1))

    kv_next = pair_update(*units[0])
    for n, (p, h) in enumerate(units):
        kv2 = kv_next
        if n + 1 < len(units):
            kv_next = pair_update(*units[n + 1])
        ks = slice(h * GLA_DK, (h + 1) * GLA_DK)
        vs = slice(h * GLA_DV, (h + 1) * GLA_DV)
        for e in range(2):
            c = 2 * p + e
            rows = slice(c * CHUNK, (c + 1) * CHUNK)
            dec = dec_ref[c // chunks_per_block, c % chunks_per_block:c % chunks_per_block + 1, ks]
            s_new = st_ref[h] * dec + kv2[:, e * GLA_DK:(e + 1) * GLA_DK]
            st_ref[h] = s_new
            o = _dot_nt(q_ref[rows, ks], s_new.astype(BF16))
            ms = jnp.mean(o * o, axis=-1, keepdims=True)
            o_ref[rows, vs] = (o * lax.rsqrt(ms + eps) * sz_ref[rows, vs]).astype(BF16)


def _gla_recurrence(q, kd, vT, sz, dec, batch, seq):
    n = q.shape[0]
    tc = GLA_TC
    steps = seq // tc
    blocks = tc // PROJ_BM
    row = lambda b, t: (b * steps + t, 0)
    return pl.pallas_call(
        _gla_rec_kernel,
        grid=(batch, steps),
        in_specs=[pl.BlockSpec((tc, GLA_HK), row),
                  pl.BlockSpec((tc, GLA_HK), row),
                  pl.BlockSpec((GLA_HV, tc), lambda b, t: (0, b * steps + t)),
                  pl.BlockSpec((tc, GLA_HV), row),
                  pl.BlockSpec((blocks, dec.shape[1], GLA_HK), lambda b, t: (b * steps + t, 0, 0))],
        out_specs=pl.BlockSpec((tc, GLA_HV), row),
        out_shape=jax.ShapeDtypeStruct((n, GLA_HV), BF16),
        scratch_shapes=[pltpu.VMEM((GLA_HEADS, GLA_DV, GLA_DK), F32)],
        compiler_params=_params("arbitrary", "arbitrary"),
        name="gla_recurrence",
    )(q, kd, vT, sz, dec)


def _out_ln_kernel(o_ref, x_ref, w_ref, g_ref, b_ref, y_ref, wb_ref):
    sub = OUT_SUB
    nsub = o_ref.shape[0] // sub

    @pl.when(pl.program_id(0) == 0)
    def _():
        wb_ref[...] = w_ref[...].astype(BF16)

    def proj(r):
        return _dot(o_ref[r * sub:(r + 1) * sub, :], wb_ref[...])

    y_next = proj(0)
    for r in range(nsub):
        rows = slice(r * sub, (r + 1) * sub)
        y = y_next
        if r + 1 < nsub:
            y_next = proj(r + 1)
        v = DEEPNORM_ALPHA * x_ref[rows, :] + y
        mu = jnp.mean(v, axis=-1, keepdims=True)
        vc = v - mu
        var = jnp.mean(vc * vc, axis=-1, keepdims=True)
        y_ref[rows, :] = vc * lax.rsqrt(var + LN_EPS) * g_ref[...] + b_ref[...]


def _out_proj_ln(o, x2, w_out, g, b):
    n = x2.shape[0]
    bm = OUT_BM
    return pl.pallas_call(
        _out_ln_kernel,
        grid=(n // bm,),
        in_specs=[pl.BlockSpec((bm, o.shape[1]), lambda i: (i, 0)),
                  pl.BlockSpec((bm, D_MODEL), lambda i: (i, 0)),
                  _resident((o.shape[1], D_MODEL)),
                  _resident((1, D_MODEL)),
                  _resident((1, D_MODEL))],
        out_specs=pl.BlockSpec((bm, D_MODEL), lambda i: (i, 0)),
        out_shape=jax.ShapeDtypeStruct((n, D_MODEL), F32),
        scratch_shapes=[pltpu.VMEM((o.shape[1], D_MODEL), BF16)],
        compiler_params=_params("arbitrary"),
        name="out_proj_ln",
    )(o, x2, w_out, g.reshape(1, D_MODEL), b.reshape(1, D_MODEL))


def _mla_in_kernel(x_ref, wc_ref, wz_ref, wr_ref, gq_ref, gkv_ref, pos_ref, freq_ref,
                   cq_ref, ckv_ref, z_ref, kr_ref):
    xb = x_ref[...].astype(BF16)
    c = _dot(xb, wc_ref[:, :MLA_Q_RANK + MLA_KV_RANK])
    cq_ref[...] = _rms_rows(c[:, :MLA_Q_RANK], gq_ref[...]).astype(BF16)
    ckv_ref[...] = _rms_rows(c[:, MLA_Q_RANK:], gkv_ref[...]).astype(BF16)
    z_ref[...] = _silu(_dot(xb, wz_ref[:, :MLA_HV])).astype(BF16)
    a = _dot_nt(xb, wr_ref[...]) * _rope_table_t(pos_ref, freq_ref).T
    kr_ref[...] = (a + pltpu.roll(a, LANES // 2, axis=1)).astype(BF16)


def _mla_in_proj(x2, w_in, g_q, g_kv, pos_row):
    n = x2.shape[0]
    bm = MLA_IN_BM
    nc = MLA_Q_RANK + MLA_KV_RANK
    half = MLA_ROPE // 2
    wT = jnp.swapaxes(w_in, 0, 1)
    wc = _prep_transposed(wT, nc // PREP_BR, PREP_BR, 1, lambda j: j)
    wz = _prep_transposed(wT, MLA_HV // (4 * MLA_ROPE), MLA_ROPE, 4,
                          lambda j: nc // MLA_ROPE + 1 + 4 * j)
    wr = _prep_rows(wT, nc // MLA_ROPE, 1, MLA_ROPE)
    t1, t2 = wr[:half], wr[half:]
    wr4T = jnp.concatenate([t1, t2, t2, t1], axis=0)
    row = lambda i: (i, 0)
    return pl.pallas_call(
        _mla_in_kernel,
        grid=(n // bm,),
        in_specs=[pl.BlockSpec((bm, D_MODEL), row),
                  _resident(wc.shape),
                  _resident(wz.shape),
                  _resident((LANES, D_MODEL)),
                  _resident((1, MLA_Q_RANK)),
                  _resident((1, MLA_KV_RANK)),
                  pl.BlockSpec((1, bm), lambda i: (0, i)),
                  _resident((MLA_ROPE // 2, 1))],
        out_specs=[pl.BlockSpec((bm, MLA_Q_RANK), row),
                   pl.BlockSpec((bm, MLA_KV_RANK), row),
                   pl.BlockSpec((bm, MLA_HV), row),
                   pl.BlockSpec((bm, LANES), row)],
        out_shape=[jax.ShapeDtypeStruct((n, MLA_Q_RANK), BF16),
                   jax.ShapeDtypeStruct((n, MLA_KV_RANK), BF16),
                   jax.ShapeDtypeStruct((n, MLA_HV), BF16),
                   jax.ShapeDtypeStruct((n, LANES), BF16)],
        compiler_params=_params("arbitrary"),
        name="mla_in_proj",
    )(x2, wc, wz, wr4T, g_q.reshape(1, -1), g_kv.reshape(1, -1), pos_row, _rope_freqs())


def _mla_up_kernel(cq_ref, ckv_ref, wqT_ref, wk_ref, wvT_ref, pos_ref, freq_ref,
                   qT_ref, k_ref, vT_ref):
    cq = cq_ref[...]
    ckv = ckv_ref[...]
    scale = MLA_QK ** -0.5 * math.log2(math.e)
    tab = _rope_table_t(pos_ref, freq_ref) * scale
    heads = 4
    half = MLA_ROPE // 2
    for g in range(0, MLA_HEADS, heads):
        acc = _dot_nt(wqT_ref[g * MLA_QK:(g + heads) * MLA_QK, :], cq)
        for h in range(heads):
            a0 = h * MLA_QK
            r0 = (g + h) * QHEAD
            t1 = acc[a0 + MLA_NOPE:a0 + MLA_NOPE + half]
            t2 = acc[a0 + MLA_NOPE + half:a0 + MLA_QK]
            rot = jnp.concatenate([t1, t2, t2, t1], axis=0) * tab
            qT_ref[r0:r0 + LANES, :] = (acc[a0:a0 + MLA_NOPE] * scale).astype(BF16)
            qT_ref[r0 + LANES:r0 + QHEAD, :] = rot.astype(BF16)
    k_ref[...] = _dot(ckv, wk_ref[...]).astype(BF16)
    vT_ref[...] = _dot_nt(wvT_ref[...], ckv).astype(BF16)


def _mla_up_proj(cq, ckv, w_uq, w_ukv, pos_row):
    n = cq.shape[0]
    bm = UP_BM
    wqT = w_uq.astype(BF16).T
    wkv = w_ukv.astype(BF16).reshape(MLA_KV_RANK, MLA_HEADS, MLA_NOPE + MLA_V)
    wk = wkv[:, :, :MLA_NOPE].reshape(MLA_KV_RANK, MLA_HEADS * MLA_NOPE)
    wvT = wkv[:, :, MLA_NOPE:].reshape(MLA_KV_RANK, MLA_HV).T
    row = lambda i: (i, 0)
    col = lambda i: (0, i)
    return pl.pallas_call(
        _mla_up_kernel,
        grid=(n // bm,),
        in_specs=[pl.BlockSpec((bm, MLA_Q_RANK), row),
                  pl.BlockSpec((bm, MLA_KV_RANK), row),
                  _resident(wqT.shape),
                  _resident(wk.shape),
                  _resident(wvT.shape),
                  pl.BlockSpec((1, bm), col),
                  _resident((MLA_ROPE // 2, 1))],
        out_specs=[pl.BlockSpec((MLA_HEADS * QHEAD, bm), col),
                   pl.BlockSpec((bm, MLA_HEADS * MLA_NOPE), row),
                   pl.BlockSpec((MLA_HV, bm), col)],
        out_shape=[jax.ShapeDtypeStruct((MLA_HEADS * QHEAD, n), BF16),
                   jax.ShapeDtypeStruct((n, MLA_HEADS * MLA_NOPE), BF16),
                   jax.ShapeDtypeStruct((MLA_HV, n), BF16)],
        compiler_params=_params("arbitrary"),
        name="mla_up_proj",
    )(cq, ckv, wqT, wk, wvT, pos_row, _rope_freqs())


def _mla_attn_kernel(qT_ref, k_ref, kr_ref, vT_ref, z_ref, qTn_ref, kn_ref, krn_ref,
                     o_ref, s_ref, m_ref):
    t = ATT_T
    nt = qT_ref.shape[1] // t
    kk = lax.broadcasted_iota(jnp.int32, (t, t), 0) // CHUNK
    qq = lax.broadcasted_iota(jnp.int32, (t, t), 1) // CHUNK
    visible = kk <= qq
    ones = jnp.ones((BF16_ROWS, t), BF16)
    blocks = [(i, j) for i in range(nt) for j in range(i + 1)]

    def score_steps(slot, q_tile, k_block, kr):
        m = [None] * nt

        def score(idx, i, j):
            ks = slice(j * t, (j + 1) * t)
            kcat = jnp.concatenate([k_block(ks), kr[ks, :]], axis=1)
            s = _dot(kcat, q_tile(i))
            if j == i:
                s = jnp.where(visible, s, -jnp.inf)
            m_blk = jnp.max(s, axis=0, keepdims=True)
            m[i] = m_blk if m[i] is None else jnp.maximum(m[i], m_blk)
            s_ref[slot, idx] = s
            if j == i:
                m_ref[slot, i] = m[i]

        return [functools.partial(score, idx, i, j) for idx, (i, j) in enumerate(blocks)]

    def local_score_steps(h):
        return score_steps(
            h % 2,
            lambda i: qT_ref[h * QHEAD:(h + 1) * QHEAD, i * t:(i + 1) * t],
            lambda ks: k_ref[ks, h * MLA_NOPE:(h + 1) * MLA_NOPE],
            kr_ref)

    def value_steps(h):
        slot = h % 2
        acc = [None] * nt

        def value(idx, i, j):
            p = jnp.exp2(s_ref[slot, idx] - m_ref[slot, i]).astype(BF16)
            v_aug = jnp.concatenate([vT_ref[h * MLA_V:(h + 1) * MLA_V, j * t:(j + 1) * t],
                                     ones], axis=0)
            pv = _dot(v_aug, p)
            acc[i] = pv if acc[i] is None else acc[i] + pv
            if j == i:
                a = acc[i]
                o = (a[:MLA_V] * (1.0 / a[MLA_V:MLA_V + 1])).T
                rows = slice(i * t, (i + 1) * t)
                cols = slice(h * MLA_V, (h + 1) * MLA_V)
                o_ref[rows, cols] = (o * z_ref[rows, cols]).astype(BF16)

        return [functools.partial(value, idx, i, j) for idx, (i, j) in enumerate(blocks)]

    @pl.when((pl.program_id(0) == 0) & (pl.program_id(1) == 0))
    def _():
        for f in local_score_steps(0):
            f()

    assert ATT_HB % 2 == 0
    for h in range(ATT_HB):
        if h + 1 < ATT_HB:
            nxt = local_score_steps(h + 1)
        else:
            nxt = score_steps(0, lambda i: qTn_ref[:, i * t:(i + 1) * t],
                              lambda ks: kn_ref[ks, :], krn_ref)
        for f_score, f_value in zip(nxt, value_steps(h)):
            f_score()
            f_value()


def _mla_attention(qT, k, kr, vT, z, batch, seq):
    n = k.shape[0]
    hb = ATT_HB
    groups = MLA_HEADS // hb
    nt = seq // ATT_T
    n_blocks = nt * (nt + 1) // 2

    def next_step(b, g):
        s = jnp.minimum(b * groups + g + 1, batch * groups - 1)
        return s // groups, s % groups

    return pl.pallas_call(
        _mla_attn_kernel,
        grid=(batch, groups),
        in_specs=[pl.BlockSpec((hb * QHEAD, seq), lambda b, g: (g, b)),
                  pl.BlockSpec((seq, hb * MLA_NOPE), lambda b, g: (b, g)),
                  pl.BlockSpec((seq, LANES), lambda b, g: (b, 0)),
                  pl.BlockSpec((hb * MLA_V, seq), lambda b, g: (g, b)),
                  pl.BlockSpec((seq, hb * MLA_V), lambda b, g: (b, g)),
                  pl.BlockSpec((QHEAD, seq), lambda b, g: (hb * next_step(b, g)[1], next_step(b, g)[0])),
                  pl.BlockSpec((seq, MLA_NOPE), lambda b, g: (next_step(b, g)[0], hb * next_step(b, g)[1])),
                  pl.BlockSpec((seq, LANES), lambda b, g: (next_step(b, g)[0], 0))],
        out_specs=pl.BlockSpec((seq, hb * MLA_V), lambda b, g: (b, g)),
        out_shape=jax.ShapeDtypeStruct((n, MLA_HV), BF16),
        scratch_shapes=[pltpu.VMEM((2, n_blocks, ATT_T, ATT_T), F32),
                        pltpu.VMEM((2, nt, 1, ATT_T), F32)],
        compiler_params=_params("arbitrary", "arbitrary"),
        name="mla_attention",
    )(qT, k, kr, vT, z, qT, k, kr)


def kernel(x, positions, gla_w_in, gla_w_a2, gla_b_a, gla_g_out, gla_w_out,
           mla_w_in, mla_g_q, mla_w_uq, mla_g_kv, mla_w_ukv, mla_w_out, ln_g, ln_b):
    batch, seq, d = x.shape
    assert d == D_MODEL and seq % GLA_TC == 0 and seq % ATT_T == 0
    x2 = x.reshape(batch * seq, d)

    q, kd, sz, vT, dec = _gla_in_proj(x2, gla_w_in[0], gla_w_a2[0], gla_b_a[0], gla_g_out[0])
    o = _gla_recurrence(q, kd, vT, sz, dec, batch, seq)
    x2 = _out_proj_ln(o, x2, gla_w_out[0], ln_g[0], ln_b[0])

    pos_row = positions.reshape(1, batch * seq)
    cq, ckv, z, kr = _mla_in_proj(x2, mla_w_in[0], mla_g_q[0], mla_g_kv[0], pos_row)
    qT, kn, vT = _mla_up_proj(cq, ckv, mla_w_uq[0], mla_w_ukv[0], pos_row)
    o = _mla_attention(qT, kn, kr, vT, z, batch, seq)
    x2 = _out_proj_ln(o, x2, mla_w_out[0], ln_g[1], ln_b[1])
    return x2.reshape(batch, seq, d)
```

```python
import functools
import math

import jax
import jax.numpy as jnp
from jax import lax
from jax.experimental import pallas as pl
from jax.experimental.pallas import tpu as pltpu

BF16 = jnp.bfloat16
F32 = jnp.float32

D_MODEL = 2048
DEPTH = 2
CHUNK = 64
DEEPNORM_ALPHA = float((2 * DEPTH) ** 0.25)
LN_EPS = 1e-5
RMS_EPS = 1e-6

GLA_HEADS = 4
GLA_DK = 256
GLA_DV = 512
GLA_HK = GLA_HEADS * GLA_DK
GLA_HV = GLA_HEADS * GLA_DV
GLA_GATE_RANK = 16
GLA_TAU = 16.0

MLA_HEADS = 16
MLA_Q_RANK = 512
MLA_KV_RANK = 512
MLA_NOPE = 128
MLA_ROPE = 64
MLA_V = 128
MLA_QK = MLA_NOPE + MLA_ROPE
MLA_HV = MLA_HEADS * MLA_V
ROPE_THETA = 10000.0

LANES = 128
SUBLANES = 8
BF16_ROWS = 16
VMEM_LIMIT = 56 * 1024 * 1024

PROJ_BM = 256
OUT_BM = 512
OUT_SUB = 128
UP_BM = 1024
MLA_IN_BM = 512
GLA_TC = 1024
ATT_T = 256
ATT_HB = 4
PREP_BR = 256
UP_PREP_HEADS = 4
WEIGHT_PAD = 2 * LANES
QHEAD = 2 * LANES


def _params(*sem):
    return pltpu.CompilerParams(dimension_semantics=sem, vmem_limit_bytes=VMEM_LIMIT)


def _resident(shape):
    zeros = (0,) * len(shape)
    return pl.BlockSpec(shape, lambda *_: zeros, pipeline_mode=pl.Buffered(1))


def _dot(a, b):
    return jnp.dot(a, b, preferred_element_type=F32)


def _dot_nt(a, b):
    return lax.dot_general(a, b, (((1,), (1,)), ((), ())), preferred_element_type=F32)


def _silu(z):
    return z * (1.0 / (1.0 + jnp.exp(-z)))


def _rms_rows(x, g):
    ms = jnp.mean(x * x, axis=-1, keepdims=True)
    return x * lax.rsqrt(ms + RMS_EPS) * g


def _prep_transpose_kernel(*refs, n_out, tail_rows, unused):
    *w_refs, o_ref = refs
    j = pl.program_id(0)
    is_unused = (j >= unused[0]) & (j < unused[1])

    def transposed():
        rows = [r[...] for r in w_refs]
        return (rows[0] if len(rows) == 1 else jnp.concatenate(rows, axis=0)).T

    @pl.when((j < n_out) & jnp.logical_not(is_unused))
    def _():
        o_ref[...] = transposed().astype(BF16)

    @pl.when(is_unused)
    def _():
        o_ref[...] = jnp.zeros(o_ref.shape, BF16)

    @pl.when(j >= n_out)
    def _():
        wt = transposed()
        col = lax.broadcasted_iota(jnp.int32, wt.shape, 1)
        o_ref[...] = jnp.where(col < tail_rows, wt, 0.0).astype(BF16)


def _prep_transposed(wT, n_out, br, parts, first_block, tail_rows=0, unused=(0, 0)):
    d = wT.shape[1]
    assert br * parts == WEIGHT_PAD and (tail_rows == 0 or parts == 1)
    last = n_out if tail_rows else n_out - 1

    def src_block(k, j):
        j = jnp.minimum(j, last)
        j = jnp.where((j >= unused[0]) & (j < unused[1]), max(unused[0] - 1, 0), j)
        return first_block(j) + k, 0

    return pl.pallas_call(
        functools.partial(_prep_transpose_kernel, n_out=n_out, tail_rows=tail_rows, unused=unused),
        grid=(n_out + 1,),
        in_specs=[pl.BlockSpec((br, d), functools.partial(src_block, k)) for k in range(parts)],
        out_specs=pl.BlockSpec((d, br * parts), lambda j: (0, j)),
        out_shape=jax.ShapeDtypeStruct((d, (n_out + 1) * br * parts), BF16),
        compiler_params=_params("arbitrary"),
        name="prep_transposed",
    )(*([wT] * parts))


def _prep_rows_kernel(w_ref, o_ref):
    d = w_ref.shape[1]
    o_ref[:, :d] = w_ref[...].astype(BF16)
    if o_ref.shape[1] > d:
        o_ref[:, d:] = jnp.zeros((o_ref.shape[0], o_ref.shape[1] - d), BF16)


def _prep_rows(wT, block0, n_blocks, br, pad=0):
    d = wT.shape[1]
    return pl.pallas_call(
        _prep_rows_kernel,
        grid=(n_blocks,),
        in_specs=[pl.BlockSpec((br, d), lambda j: (block0 + j, 0))],
        out_specs=pl.BlockSpec((br, d + pad), lambda j: (j, 0)),
        out_shape=jax.ShapeDtypeStruct((n_blocks * br, d + pad), BF16),
        compiler_params=_params("arbitrary"),
        name="prep_rows",
    )(wT)


def _rope_table_t(pos_ref, freq_ref):
    pos = pos_ref[...].astype(F32)
    ang = freq_ref[...] * pos
    c = jnp.cos(ang)
    s = jnp.sin(ang)
    return jnp.concatenate([c, c, -s, s], axis=0)


def _rope_freqs():
    half = MLA_ROPE // 2
    return (ROPE_THETA ** (-jnp.arange(half, dtype=F32) / half)).reshape(half, 1)


def _chunk_cumsum(la):
    t, n = la.shape
    x = la.reshape(t // SUBLANES, SUBLANES, n)
    row = lax.broadcasted_iota(jnp.int32, x.shape, 1)
    s = 1
    while s < SUBLANES:
        x = x + jnp.where(row >= s, pltpu.roll(x, s, axis=1), 0.0)
        s *= 2
    tiles_per_chunk = CHUNK // SUBLANES
    out = []
    carry = None
    for i in range(t // SUBLANES):
        tile = x[i]
        if i % tiles_per_chunk != 0:
            tile = tile + carry
        carry = tile[SUBLANES - 1:SUBLANES, :]
        out.append(tile)
    return jnp.concatenate(out, axis=0)


def _gla_in_kernel(x_ref, w_ref, wvT_ref, wa2_ref, ba_ref, g_ref,
                   q_ref, kd_ref, sz_ref, vT_ref, dec_ref):
    bm = x_ref.shape[0]
    z0 = 2 * GLA_HK + GLA_HV
    xb = x_ref[...].astype(BF16)
    q_ref[...] = _dot(xb, w_ref[:, :GLA_HK]).astype(BF16)
    za = _dot(xb, w_ref[:, z0:])
    z = za[:, :GLA_HV]
    a = za[:, GLA_HV:GLA_HV + LANES].astype(BF16)
    sz_ref[...] = (_silu(z) * g_ref[...]).astype(BF16)
    vT_ref[...] = _dot_nt(wvT_ref[:, :D_MODEL], xb).astype(BF16)

    pre = _dot(a, wa2_ref[...]) + ba_ref[...]
    la = (jnp.minimum(pre, 0.0) - jnp.log(1.0 + jnp.exp(-jnp.abs(pre)))) * (1.0 / GLA_TAU)
    cum = _chunk_cumsum(la)
    chunks = bm // CHUNK
    tots = [cum[(c + 1) * CHUNK - 1:(c + 1) * CHUNK, :] for c in range(chunks)]
    dec_ref[0] = jnp.exp(jnp.concatenate(tots, axis=0))
    tot_rows = jnp.concatenate([jnp.broadcast_to(t, (CHUNK, GLA_HK)) for t in tots], axis=0)
    kd_ref[...] = (_dot(xb, w_ref[:, GLA_HK:2 * GLA_HK]) * jnp.exp(tot_rows - cum)).astype(BF16)


def _gla_in_proj(x2, w_in, w_a2, b_a, g_out):
    n = x2.shape[0]
    bm = PROJ_BM
    br = PREP_BR
    wT = jnp.swapaxes(w_in, 0, 1)
    a0 = 2 * GLA_HK + 2 * GLA_HV
    v_blocks = (2 * GLA_HK // br, (2 * GLA_HK + GLA_HV) // br)
    wb = _prep_transposed(wT, a0 // br, br, 1, lambda j: j, tail_rows=GLA_GATE_RANK,
                          unused=v_blocks)
    wvT = _prep_rows(wT, 2 * GLA_HK // br, GLA_HV // br, br, WEIGHT_PAD)
    wa2_p = jnp.pad(w_a2, ((0, LANES - GLA_GATE_RANK), (0, 0))).astype(BF16)
    row = lambda i: (i, 0)
    return pl.pallas_call(
        _gla_in_kernel,
        grid=(n // bm,),
        in_specs=[pl.BlockSpec((bm, D_MODEL), row),
                  _resident(wb.shape),
                  _resident(wvT.shape),
                  _resident((LANES, GLA_HK)),
                  _resident((1, GLA_HK)),
                  _resident((1, GLA_HV))],
        out_specs=[pl.BlockSpec((bm, GLA_HK), row),
                   pl.BlockSpec((bm, GLA_HK), row),
                   pl.BlockSpec((bm, GLA_HV), row),
                   pl.BlockSpec((GLA_HV, bm), lambda i: (0, i)),
                   pl.BlockSpec((1, bm // CHUNK, GLA_HK), lambda i: (i, 0, 0))],
        out_shape=[jax.ShapeDtypeStruct((n, GLA_HK), BF16),
                   jax.ShapeDtypeStruct((n, GLA_HK), BF16),
                   jax.ShapeDtypeStruct((n, GLA_HV), BF16),
                   jax.ShapeDtypeStruct((GLA_HV, n), BF16),
                   jax.ShapeDtypeStruct((n // bm, bm // CHUNK, GLA_HK), F32)],
        compiler_params=_params("arbitrary"),
        name="gla_in_proj",
    )(x2, wb, wvT, wa2_p, b_a.reshape(1, GLA_HK), g_out.reshape(1, GLA_HV))


def _gla_rec_kernel(q_ref, kd_ref, vT_ref, sz_ref, dec_ref, o_ref, st_ref):
    @pl.when(pl.program_id(1) == 0)
    def _():
        st_ref[...] = jnp.zeros_like(st_ref)

    pair = 2 * CHUNK
    chunks_per_block = dec_ref.shape[1]
    eps = RMS_EPS * GLA_DK
    units = [(p, h) for p in range(GLA_TC // pair) for h in range(GLA_HEADS)]

    zeros = jnp.zeros((CHUNK, GLA_DK), BF16)

    def pair_update(p, h):
        r0 = p * pair
        ks = slice(h * GLA_DK, (h + 1) * GLA_DK)
        even = jnp.concatenate([kd_ref[r0:r0 + CHUNK, ks], zeros], axis=0)
        odd = jnp.concatenate([zeros, kd_ref[r0 + CHUNK:r0 + pair, ks]], axis=0)
        return _dot(vT_ref[h * GLA_DV:(h + 1) * GLA_DV, r0:r0 + pair],
                    jnp.concatenate([even, odd], axis=1))

    kv_next = pair_update(*units[0])
    for n, (p, h) in enumerate(units):
        kv2 = kv_next
        if n + 1 < len(units):
            kv_next = pair_update(*units[n + 1])
        ks = slice(h * GLA_DK, (h + 1) * GLA_DK)
        vs = slice(h * GLA_DV, (h + 1) * GLA_DV)
        for e in range(2):
            c = 2 * p + e
            rows = slice(c * CHUNK, (c + 1) * CHUNK)
            dec = dec_ref[c // chunks_per_block, c % chunks_per_block:c % chunks_per_block + 1, ks]
            s_new = st_ref[h] * dec + kv2[:, e * GLA_DK:(e + 1) * GLA_DK]
            st_ref[h] = s_new
            o = _dot_nt(q_ref[rows, ks], s_new.astype(BF16))
            ms = jnp.mean(o * o, axis=-1, keepdims=True)
            o_ref[rows, vs] = (o * lax.rsqrt(ms + eps) * sz_ref[rows, vs]).astype(BF16)


def _gla_recurrence(q, kd, vT, sz, dec, batch, seq):
    n = q.shape[0]
    tc = GLA_TC
    steps = seq // tc
    blocks = tc // PROJ_BM
    row = lambda b, t: (b * steps + t, 0)
    return pl.pallas_call(
        _gla_rec_kernel,
        grid=(batch, steps),
        in_specs=[pl.BlockSpec((tc, GLA_HK), row),
                  pl.BlockSpec((tc, GLA_HK), row),
                  pl.BlockSpec((GLA_HV, tc), lambda b, t: (0, b * steps + t)),
                  pl.BlockSpec((tc, GLA_HV), row),
                  pl.BlockSpec((blocks, dec.shape[1], GLA_HK), lambda b, t: (b * steps + t, 0, 0))],
        out_specs=pl.BlockSpec((tc, GLA_HV), row),
        out_shape=jax.ShapeDtypeStruct((n, GLA_HV), BF16),
        scratch_shapes=[pltpu.VMEM((GLA_HEADS, GLA_DV, GLA_DK), F32)],
        compiler_params=_params("arbitrary", "arbitrary"),
        name="gla_recurrence",
    )(q, kd, vT, sz, dec)


def _out_ln_kernel(o_ref, x_ref, w_ref, g_ref, b_ref, y_ref, wb_ref):
    sub = OUT_SUB
    nsub = o_ref.shape[0] // sub

    @pl.when(pl.program_id(0) == 0)
    def _():
        wb_ref[...] = w_ref[...].astype(BF16)

    def proj(r):
        return _dot(o_ref[r * sub:(r + 1) * sub, :], wb_ref[...])

    y_next = proj(0)
    for r in range(nsub):
        rows = slice(r * sub, (r + 1) * sub)
        y = y_next
        if r + 1 < nsub:
            y_next = proj(r + 1)
        v = DEEPNORM_ALPHA * x_ref[rows, :] + y
        mu = jnp.mean(v, axis=-1, keepdims=True)
        vc = v - mu
        var = jnp.mean(vc * vc, axis=-1, keepdims=True)
        y_ref[rows, :] = vc * lax.rsqrt(var + LN_EPS) * g_ref[...] + b_ref[...]


def _out_proj_ln(o, x2, w_out, g, b):
    n = x2.shape[0]
    bm = OUT_BM
    return pl.pallas_call(
        _out_ln_kernel,
        grid=(n // bm,),
        in_specs=[pl.BlockSpec((bm, o.shape[1]), lambda i: (i, 0)),
                  pl.BlockSpec((bm, D_MODEL), lambda i: (i, 0)),
                  _resident((o.shape[1], D_MODEL)),
                  _resident((1, D_MODEL)),
                  _resident((1, D_MODEL))],
        out_specs=pl.BlockSpec((bm, D_MODEL), lambda i: (i, 0)),
        out_shape=jax.ShapeDtypeStruct((n, D_MODEL), F32),
        scratch_shapes=[pltpu.VMEM((o.shape[1], D_MODEL), BF16)],
        compiler_params=_params("arbitrary"),
        name="out_proj_ln",
    )(o, x2, w_out, g.reshape(1, D_MODEL), b.reshape(1, D_MODEL))


def _mla_in_kernel(x_ref, wc_ref, wz_ref, wr_ref, gq_ref, gkv_ref, pos_ref, freq_ref,
                   cq_ref, ckv_ref, z_ref, kr_ref):
    xb = x_ref[...].astype(BF16)
    c = _dot(xb, wc_ref[:, :MLA_Q_RANK + MLA_KV_RANK])
    cq_ref[...] = _rms_rows(c[:, :MLA_Q_RANK], gq_ref[...]).astype(BF16)
    ckv_ref[...] = _rms_rows(c[:, MLA_Q_RANK:], gkv_ref[...]).astype(BF16)
    z_ref[...] = _silu(_dot(xb, wz_ref[:, :MLA_HV])).astype(BF16)
    a = _dot_nt(xb, wr_ref[...]) * _rope_table_t(pos_ref, freq_ref).T
    kr_ref[...] = (a + pltpu.roll(a, LANES // 2, axis=1)).astype(BF16)


def _mla_in_proj(x2, w_in, g_q, g_kv, pos_row):
    n = x2.shape[0]
    bm = MLA_IN_BM
    nc = MLA_Q_RANK + MLA_KV_RANK
    half = MLA_ROPE // 2
    wT = jnp.swapaxes(w_in, 0, 1)
    wc = _prep_transposed(wT, nc // PREP_BR, PREP_BR, 1, lambda j: j)
    wz = _prep_transposed(wT, MLA_HV // (4 * MLA_ROPE), MLA_ROPE, 4,
                          lambda j: nc // MLA_ROPE + 1 + 4 * j)
    wr = _prep_rows(wT, nc // MLA_ROPE, 1, MLA_ROPE)
    t1, t2 = wr[:half], wr[half:]
    wr4T = jnp.concatenate([t1, t2, t2, t1], axis=0)
    row = lambda i: (i, 0)
    return pl.pallas_call(
        _mla_in_kernel,
        grid=(n // bm,),
        in_specs=[pl.BlockSpec((bm, D_MODEL), row),
                  _resident(wc.shape),
                  _resident(wz.shape),
                  _resident((LANES, D_MODEL)),
                  _resident((1, MLA_Q_RANK)),
                  _resident((1, MLA_KV_RANK)),
                  pl.BlockSpec((1, bm), lambda i: (0, i)),
                  _resident((MLA_ROPE // 2, 1))],
        out_specs=[pl.BlockSpec((bm, MLA_Q_RANK), row),
                   pl.BlockSpec((bm, MLA_KV_RANK), row),
                   pl.BlockSpec((bm, MLA_HV), row),
                   pl.BlockSpec((bm, LANES), row)],
        out_shape=[jax.ShapeDtypeStruct((n, MLA_Q_RANK), BF16),
                   jax.ShapeDtypeStruct((n, MLA_KV_RANK), BF16),
                   jax.ShapeDtypeStruct((n, MLA_HV), BF16),
                   jax.ShapeDtypeStruct((n, LANES), BF16)],
        compiler_params=_params("arbitrary"),
        name="mla_in_proj",
    )(x2, wc, wz, wr4T, g_q.reshape(1, -1), g_kv.reshape(1, -1), pos_row, _rope_freqs())


def _mla_up_kernel(cq_ref, ckv_ref, wqT_ref, wk_ref, wvT_ref, pos_ref, freq_ref,
                   qT_ref, k_ref, vT_ref):
    cq = cq_ref[...]
    ckv = ckv_ref[...]
    scale = MLA_QK ** -0.5 * math.log2(math.e)
    tab = _rope_table_t(pos_ref, freq_ref) * scale
    heads = 4
    half = MLA_ROPE // 2
    for g in range(0, MLA_HEADS, heads):
        acc = _dot_nt(wqT_ref[g * MLA_QK:(g + heads) * MLA_QK, :], cq)
        for h in range(heads):
            a0 = h * MLA_QK
            r0 = (g + h) * QHEAD
            t1 = acc[a0 + MLA_NOPE:a0 + MLA_NOPE + half]
            t2 = acc[a0 + MLA_NOPE + half:a0 + MLA_QK]
            rot = jnp.concatenate([t1, t2, t2, t1], axis=0) * tab
            qT_ref[r0:r0 + LANES, :] = (acc[a0:a0 + MLA_NOPE] * scale).astype(BF16)
            qT_ref[r0 + LANES:r0 + QHEAD, :] = rot.astype(BF16)
    k_ref[...] = _dot(ckv, wk_ref[...]).astype(BF16)
    vT_ref[...] = _dot_nt(wvT_ref[...], ckv).astype(BF16)


def _up_weights_kernel(wq_ref, wkv_ref, wqT_ref, wk_ref, wvT_ref):
    wqT_ref[...] = wq_ref[...].T.astype(BF16)
    per_head = MLA_NOPE + MLA_V
    for h in range(UP_PREP_HEADS):
        c0 = h * per_head
        wk_ref[:, h * MLA_NOPE:(h + 1) * MLA_NOPE] = wkv_ref[:, c0:c0 + MLA_NOPE].astype(BF16)
        wvT_ref[h * MLA_V:(h + 1) * MLA_V, :] = wkv_ref[:, c0 + MLA_NOPE:c0 + per_head].T.astype(BF16)


def _up_weights(w_uq, w_ukv):
    hb = UP_PREP_HEADS
    row = lambda j: (j, 0)
    col = lambda j: (0, j)
    return pl.pallas_call(
        _up_weights_kernel,
        grid=(MLA_HEADS // hb,),
        in_specs=[pl.BlockSpec((MLA_Q_RANK, hb * MLA_QK), col),
                  pl.BlockSpec((MLA_KV_RANK, hb * (MLA_NOPE + MLA_V)), col)],
        out_specs=[pl.BlockSpec((hb * MLA_QK, MLA_Q_RANK), row),
                   pl.BlockSpec((MLA_KV_RANK, hb * MLA_NOPE), col),
                   pl.BlockSpec((hb * MLA_V, MLA_KV_RANK), row)],
        out_shape=[jax.ShapeDtypeStruct((MLA_HEADS * MLA_QK, MLA_Q_RANK), BF16),
                   jax.ShapeDtypeStruct((MLA_KV_RANK, MLA_HEADS * MLA_NOPE), BF16),
                   jax.ShapeDtypeStruct((MLA_HV, MLA_KV_RANK), BF16)],
        compiler_params=_params("arbitrary"),
        name="prep_up_weights",
    )(w_uq, w_ukv)


def _mla_up_proj(cq, ckv, w_uq, w_ukv, pos_row):
    n = cq.shape[0]
    bm = UP_BM
    wqT, wk, wvT = _up_weights(w_uq, w_ukv)
    row = lambda i: (i, 0)
    col = lambda i: (0, i)
    return pl.pallas_call(
        _mla_up_kernel,
        grid=(n // bm,),
        in_specs=[pl.BlockSpec((bm, MLA_Q_RANK), row),
                  pl.BlockSpec((bm, MLA_KV_RANK), row),
                  _resident(wqT.shape),
                  _resident(wk.shape),
                  _resident(wvT.shape),
                  pl.BlockSpec((1, bm), col),
                  _resident((MLA_ROPE // 2, 1))],
        out_specs=[pl.BlockSpec((MLA_HEADS * QHEAD, bm), col),
                   pl.BlockSpec((bm, MLA_HEADS * MLA_NOPE), row),
                   pl.BlockSpec((MLA_HV, bm), col)],
        out_shape=[jax.ShapeDtypeStruct((MLA_HEADS * QHEAD, n), BF16),
                   jax.ShapeDtypeStruct((n, MLA_HEADS * MLA_NOPE), BF16),
                   jax.ShapeDtypeStruct((MLA_HV, n), BF16)],
        compiler_params=_params("arbitrary"),
        name="mla_up_proj",
    )(cq, ckv, wqT, wk, wvT, pos_row, _rope_freqs())


def _mla_attn_kernel(qT_ref, k_ref, kr_ref, vT_ref, z_ref, qTn_ref, kn_ref, krn_ref,
                     o_ref, s_ref, m_ref):
    t = ATT_T
    nt = qT_ref.shape[1] // t
    kk = lax.broadcasted_iota(jnp.int32, (t, t), 0) // CHUNK
    qq = lax.broadcasted_iota(jnp.int32, (t, t), 1) // CHUNK
    visible = kk <= qq
    ones = jnp.ones((BF16_ROWS, t), BF16)
    blocks = [(i, j) for i in range(nt) for j in range(i + 1)]

    def score_steps(slot, q_tile, k_block, kr):
        m = [None] * nt

        def score(idx, i, j):
            ks = slice(j * t, (j + 1) * t)
            kcat = jnp.concatenate([k_block(ks), kr[ks, :]], axis=1)
            s = _dot(kcat, q_tile(i))
            if j == i:
                s = jnp.where(visible, s, -jnp.inf)
            m_blk = jnp.max(s, axis=0, keepdims=True)
            m[i] = m_blk if m[i] is None else jnp.maximum(m[i], m_blk)
            s_ref[slot, idx] = s
            if j == i:
                m_ref[slot, i] = m[i]

        return [functools.partial(score, idx, i, j) for idx, (i, j) in enumerate(blocks)]

    def local_score_steps(h):
        return score_steps(
            h % 2,
            lambda i: qT_ref[h * QHEAD:(h + 1) * QHEAD, i * t:(i + 1) * t],
            lambda ks: k_ref[ks, h * MLA_NOPE:(h + 1) * MLA_NOPE],
            kr_ref)

    def value_steps(h):
        slot = h % 2
        acc = [None] * nt

        def value(idx, i, j):
            p = jnp.exp2(s_ref[slot, idx] - m_ref[slot, i]).astype(BF16)
            v_aug = jnp.concatenate([vT_ref[h * MLA_V:(h + 1) * MLA_V, j * t:(j + 1) * t],
                                     ones], axis=0)
            pv = _dot(v_aug, p)
            acc[i] = pv if acc[i] is None else acc[i] + pv
            if j == i:
                a = acc[i]
                o = (a[:MLA_V] * (1.0 / a[MLA_V:MLA_V + 1])).T
                rows = slice(i * t, (i + 1) * t)
                cols = slice(h * MLA_V, (h + 1) * MLA_V)
                o_ref[rows, cols] = (o * z_ref[rows, cols]).astype(BF16)

        return [functools.partial(value, idx, i, j) for idx, (i, j) in enumerate(blocks)]

    @pl.when((pl.program_id(0) == 0) & (pl.program_id(1) == 0))
    def _():
        for f in local_score_steps(0):
            f()

    assert ATT_HB % 2 == 0
    for h in range(ATT_HB):
        if h + 1 < ATT_HB:
            nxt = local_score_steps(h + 1)
        else:
            nxt = score_steps(0, lambda i: qTn_ref[:, i * t:(i + 1) * t],
                              lambda ks: kn_ref[ks, :], krn_ref)
        for f_score, f_value in zip(nxt, value_steps(h)):
            f_score()
            f_value()


def _mla_attention(qT, k, kr, vT, z, batch, seq):
    n = k.shape[0]
    hb = ATT_HB
    groups = MLA_HEADS // hb
    nt = seq // ATT_T
    n_blocks = nt * (nt + 1) // 2

    def next_step(b, g):
        s = jnp.minimum(b * groups + g + 1, batch * groups - 1)
        return s // groups, s % groups

    return pl.pallas_call(
        _mla_attn_kernel,
        grid=(batch, groups),
        in_specs=[pl.BlockSpec((hb * QHEAD, seq), lambda b, g: (g, b)),
                  pl.BlockSpec((seq, hb * MLA_NOPE), lambda b, g: (b, g)),
                  pl.BlockSpec((seq, LANES), lambda b, g: (b, 0)),
                  pl.BlockSpec((hb * MLA_V, seq), lambda b, g: (g, b)),
                  pl.BlockSpec((seq, hb * MLA_V), lambda b, g: (b, g)),
                  pl.BlockSpec((QHEAD, seq), lambda b, g: (hb * next_step(b, g)[1], next_step(b, g)[0])),
                  pl.BlockSpec((seq, MLA_NOPE), lambda b, g: (next_step(b, g)[0], hb * next_step(b, g)[1])),
                  pl.BlockSpec((seq, LANES), lambda b, g: (next_step(b, g)[0], 0))],
        out_specs=pl.BlockSpec((seq, hb * MLA_V), lambda b, g: (b, g)),
        out_shape=jax.ShapeDtypeStruct((n, MLA_HV), BF16),
        scratch_shapes=[pltpu.VMEM((2, n_blocks, ATT_T, ATT_T), F32),
                        pltpu.VMEM((2, nt, 1, ATT_T), F32)],
        compiler_params=_params("arbitrary", "arbitrary"),
        name="mla_attention",
    )(qT, k, kr, vT, z, qT, k, kr)


def kernel(x, positions, gla_w_in, gla_w_a2, gla_b_a, gla_g_out, gla_w_out,
           mla_w_in, mla_g_q, mla_w_uq, mla_g_kv, mla_w_ukv, mla_w_out, ln_g, ln_b):
    batch, seq, d = x.shape
    assert d == D_MODEL and seq % GLA_TC == 0 and seq % ATT_T == 0
    x2 = x.reshape(batch * seq, d)

    q, kd, sz, vT, dec = _gla_in_proj(x2, gla_w_in[0], gla_w_a2[0], gla_b_a[0], gla_g_out[0])
    o = _gla_recurrence(q, kd, vT, sz, dec, batch, seq)
    x2 = _out_proj_ln(o, x2, gla_w_out[0], ln_g[0], ln_b[0])

    pos_row = positions.reshape(1, batch * seq)
    cq, ckv, z, kr = _mla_in_proj(x2, mla_w_in[0], mla_g_q[0], mla_g_kv[0], pos_row)
    qT, kn, vT = _mla_up_proj(cq, ckv, mla_w_uq[0], mla_w_ukv[0], pos_row)
    o = _mla_attention(qT, kn, kr, vT, z, batch, seq)
    x2 = _out_proj_ln(o, x2, mla_w_out[0], ln_g[1], ln_b[1])
    return x2.reshape(batch, seq, d)
```

```python
import functools
import math

import jax
import jax.numpy as jnp
from jax import lax
from jax.experimental import pallas as pl
from jax.experimental.pallas import tpu as pltpu

BF16 = jnp.bfloat16
F32 = jnp.float32

D_MODEL = 2048
DEPTH = 2
CHUNK = 64
DEEPNORM_ALPHA = float((2 * DEPTH) ** 0.25)
LN_EPS = 1e-5
RMS_EPS = 1e-6

GLA_HEADS = 4
GLA_DK = 256
GLA_DV = 512
GLA_HK = GLA_HEADS * GLA_DK
GLA_HV = GLA_HEADS * GLA_DV
GLA_GATE_RANK = 16
GLA_TAU = 16.0

MLA_HEADS = 16
MLA_Q_RANK = 512
MLA_KV_RANK = 512
MLA_NOPE = 128
MLA_ROPE = 64
MLA_V = 128
MLA_QK = MLA_NOPE + MLA_ROPE
MLA_HV = MLA_HEADS * MLA_V
ROPE_THETA = 10000.0

LANES = 128
SUBLANES = 8
BF16_ROWS = 16
VMEM_LIMIT = 56 * 1024 * 1024

PROJ_BM = 256
OUT_BM = 512
OUT_SUB = 128
UP_BM = 1024
MLA_IN_BM = 512
GLA_TC = 1024
ATT_T = 256
ATT_HB = 4
PREP_BR = 256
UP_PREP_HEADS = 4
WEIGHT_PAD = 2 * LANES
QHEAD = 2 * LANES


def _params(*sem):
    return pltpu.CompilerParams(dimension_semantics=sem, vmem_limit_bytes=VMEM_LIMIT)


def _resident(shape):
    zeros = (0,) * len(shape)
    return pl.BlockSpec(shape, lambda *_: zeros, pipeline_mode=pl.Buffered(1))


def _dot(a, b):
    return jnp.dot(a, b, preferred_element_type=F32)


def _dot_nt(a, b):
    return lax.dot_general(a, b, (((1,), (1,)), ((), ())), preferred_element_type=F32)


def _silu(z):
    return z * (1.0 / (1.0 + jnp.exp(-z)))


def _rms_rows(x, g):
    ms = jnp.mean(x * x, axis=-1, keepdims=True)
    return x * lax.rsqrt(ms + RMS_EPS) * g


def _prep_transpose_kernel(*refs, n_out, tail_rows, unused):
    *w_refs, o_ref = refs
    j = pl.program_id(0)
    is_unused = (j >= unused[0]) & (j < unused[1])

    def transposed():
        rows = [r[...].astype(BF16) for r in w_refs]
        return (rows[0] if len(rows) == 1 else jnp.concatenate(rows, axis=0)).T

    @pl.when((j < n_out) & jnp.logical_not(is_unused))
    def _():
        o_ref[...] = transposed()

    @pl.when(is_unused)
    def _():
        o_ref[...] = jnp.zeros(o_ref.shape, BF16)

    @pl.when(j >= n_out)
    def _():
        wt = transposed()
        col = lax.broadcasted_iota(jnp.int32, wt.shape, 1)
        o_ref[...] = jnp.where(col < tail_rows, wt, jnp.zeros_like(wt))


def _prep_transposed(wT, n_out, br, parts, first_block, tail_rows=0, unused=(0, 0)):
    d = wT.shape[1]
    assert br * parts == WEIGHT_PAD and (tail_rows == 0 or parts == 1)
    last = n_out if tail_rows else n_out - 1

    def src_block(k, j):
        j = jnp.minimum(j, last)
        j = jnp.where((j >= unused[0]) & (j < unused[1]), max(unused[0] - 1, 0), j)
        return first_block(j) + k, 0

    return pl.pallas_call(
        functools.partial(_prep_transpose_kernel, n_out=n_out, tail_rows=tail_rows, unused=unused),
        grid=(n_out + 1,),
        in_specs=[pl.BlockSpec((br, d), functools.partial(src_block, k)) for k in range(parts)],
        out_specs=pl.BlockSpec((d, br * parts), lambda j: (0, j)),
        out_shape=jax.ShapeDtypeStruct((d, (n_out + 1) * br * parts), BF16),
        compiler_params=_params("arbitrary"),
        name="prep_transposed",
    )(*([wT] * parts))


def _prep_rows_kernel(w_ref, o_ref):
    d = w_ref.shape[1]
    o_ref[:, :d] = w_ref[...].astype(BF16)
    if o_ref.shape[1] > d:
        o_ref[:, d:] = jnp.zeros((o_ref.shape[0], o_ref.shape[1] - d), BF16)


def _prep_rows(wT, block0, n_blocks, br, pad=0):
    d = wT.shape[1]
    return pl.pallas_call(
        _prep_rows_kernel,
        grid=(n_blocks,),
        in_specs=[pl.BlockSpec((br, d), lambda j: (block0 + j, 0))],
        out_specs=pl.BlockSpec((br, d + pad), lambda j: (j, 0)),
        out_shape=jax.ShapeDtypeStruct((n_blocks * br, d + pad), BF16),
        compiler_params=_params("arbitrary"),
        name="prep_rows",
    )(wT)


def _rope_table_t(pos_ref, freq_ref):
    pos = pos_ref[...].astype(F32)
    ang = freq_ref[...] * pos
    c = jnp.cos(ang)
    s = jnp.sin(ang)
    return jnp.concatenate([c, c, -s, s], axis=0)


def _rope_freqs():
    half = MLA_ROPE // 2
    return (ROPE_THETA ** (-jnp.arange(half, dtype=F32) / half)).reshape(half, 1)


def _chunk_cumsum(la):
    t, n = la.shape
    x = la.reshape(t // SUBLANES, SUBLANES, n)
    row = lax.broadcasted_iota(jnp.int32, x.shape, 1)
    s = 1
    while s < SUBLANES:
        x = x + jnp.where(row >= s, pltpu.roll(x, s, axis=1), 0.0)
        s *= 2
    tiles_per_chunk = CHUNK // SUBLANES
    out = []
    carry = None
    for i in range(t // SUBLANES):
        tile = x[i]
        if i % tiles_per_chunk != 0:
            tile = tile + carry
        carry = tile[SUBLANES - 1:SUBLANES, :]
        out.append(tile)
    return jnp.concatenate(out, axis=0)


def _gla_in_kernel(x_ref, w_ref, wvT_ref, wa2_ref, ba_ref, g_ref,
                   q_ref, kd_ref, sz_ref, vT_ref, dec_ref):
    bm = x_ref.shape[0]
    z0 = 2 * GLA_HK + GLA_HV
    xb = x_ref[...].astype(BF16)
    q_ref[...] = _dot(xb, w_ref[:, :GLA_HK]).astype(BF16)
    za = _dot(xb, w_ref[:, z0:])
    z = za[:, :GLA_HV]
    a = za[:, GLA_HV:GLA_HV + LANES].astype(BF16)
    sz_ref[...] = (_silu(z) * g_ref[...]).astype(BF16)
    vT_ref[...] = _dot_nt(wvT_ref[:, :D_MODEL], xb).astype(BF16)

    pre = _dot(a, wa2_ref[...]) + ba_ref[...]
    la = (jnp.minimum(pre, 0.0) - jnp.log(1.0 + jnp.exp(-jnp.abs(pre)))) * (1.0 / GLA_TAU)
    cum = _chunk_cumsum(la)
    chunks = bm // CHUNK
    tots = [cum[(c + 1) * CHUNK - 1:(c + 1) * CHUNK, :] for c in range(chunks)]
    dec_ref[0] = jnp.exp(jnp.concatenate(tots, axis=0))
    tot_rows = jnp.concatenate([jnp.broadcast_to(t, (CHUNK, GLA_HK)) for t in tots], axis=0)
    kd_ref[...] = (_dot(xb, w_ref[:, GLA_HK:2 * GLA_HK]) * jnp.exp(tot_rows - cum)).astype(BF16)


def _gla_in_proj(x2, w_in, w_a2, b_a, g_out):
    n = x2.shape[0]
    bm = PROJ_BM
    br = PREP_BR
    wT = jnp.swapaxes(w_in, 0, 1)
    a0 = 2 * GLA_HK + 2 * GLA_HV
    v_blocks = (2 * GLA_HK // br, (2 * GLA_HK + GLA_HV) // br)
    wb = _prep_transposed(wT, a0 // br, br, 1, lambda j: j, tail_rows=GLA_GATE_RANK,
                          unused=v_blocks)
    wvT = _prep_rows(wT, 2 * GLA_HK // br, GLA_HV // br, br, WEIGHT_PAD)
    wa2_p = jnp.pad(w_a2, ((0, LANES - GLA_GATE_RANK), (0, 0))).astype(BF16)
    row = lambda i: (i, 0)
    return pl.pallas_call(
        _gla_in_kernel,
        grid=(n // bm,),
        in_specs=[pl.BlockSpec((bm, D_MODEL), row),
                  _resident(wb.shape),
                  _resident(wvT.shape),
                  _resident((LANES, GLA_HK)),
                  _resident((1, GLA_HK)),
                  _resident((1, GLA_HV))],
        out_specs=[pl.BlockSpec((bm, GLA_HK), row),
                   pl.BlockSpec((bm, GLA_HK), row),
                   pl.BlockSpec((bm, GLA_HV), row),
                   pl.BlockSpec((GLA_HV, bm), lambda i: (0, i)),
                   pl.BlockSpec((1, bm // CHUNK, GLA_HK), lambda i: (i, 0, 0))],
        out_shape=[jax.ShapeDtypeStruct((n, GLA_HK), BF16),
                   jax.ShapeDtypeStruct((n, GLA_HK), BF16),
                   jax.ShapeDtypeStruct((n, GLA_HV), BF16),
                   jax.ShapeDtypeStruct((GLA_HV, n), BF16),
                   jax.ShapeDtypeStruct((n // bm, bm // CHUNK, GLA_HK), F32)],
        compiler_params=_params("arbitrary"),
        name="gla_in_proj",
    )(x2, wb, wvT, wa2_p, b_a.reshape(1, GLA_HK), g_out.reshape(1, GLA_HV))


def _gla_rec_kernel(q_ref, kd_ref, vT_ref, sz_ref, dec_ref, o_ref, st_ref):
    @pl.when(pl.program_id(1) == 0)
    def _():
        st_ref[...] = jnp.zeros_like(st_ref)

    pair = 2 * CHUNK
    chunks_per_block = dec_ref.shape[1]
    eps = RMS_EPS * GLA_DK
    units = [(p, h) for p in range(GLA_TC // pair) for h in range(GLA_HEADS)]

    zeros = jnp.zeros((CHUNK, GLA_DK), BF16)

    def pair_update(p, h):
        r0 = p * pair
        ks = slice(h * GLA_DK, (h + 1) * GLA_DK)
        even = jnp.concatenate([kd_ref[r0:r0 + CHUNK, ks], zeros], axis=0)
        odd = jnp.concatenate([zeros, kd_ref[r0 + CHUNK:r0 + pair, ks]], axis=0)
        return _dot(vT_ref[h * GLA_DV:(h + 1) * GLA_DV, r0:r0 + pair],
                    jnp.concatenate([even, odd], axis=1))

    kv_next = pair_update(*units[0])
    for n, (p, h) in enumerate(units):
        kv2 = kv_next
        if n + 1 < len(units):
            kv_next = pair_update(*units[n + 1])
        ks = slice(h * GLA_DK, (h + 1) * GLA_DK)
        vs = slice(h * GLA_DV, (h + 1) * GLA_DV)
        for e in range(2):
            c = 2 * p + e
            rows = slice(c * CHUNK, (c + 1) * CHUNK)
            dec = dec_ref[c // chunks_per_block, c % chunks_per_block:c % chunks_per_block + 1, ks]
            s_new = st_ref[h] * dec + kv2[:, e * GLA_DK:(e + 1) * GLA_DK]
            st_ref[h] = s_new
            o = _dot_nt(q_ref[rows, ks], s_new.astype(BF16))
            ms = jnp.mean(o * o, axis=-1, keepdims=True)
            o_ref[rows, vs] = (o * lax.rsqrt(ms + eps) * sz_ref[rows, vs]).astype(BF16)


def _gla_recurrence(q, kd, vT, sz, dec, batch, seq):
    n = q.shape[0]
    tc = GLA_TC
    steps = seq // tc
    blocks = tc // PROJ_BM
    row = lambda b, t: (b * steps + t, 0)
    return pl.pallas_call(
        _gla_rec_kernel,
        grid=(batch, steps),
        in_specs=[pl.BlockSpec((tc, GLA_HK), row),
                  pl.BlockSpec((tc, GLA_HK), row),
                  pl.BlockSpec((GLA_HV, tc), lambda b, t: (0, b * steps + t)),
                  pl.BlockSpec((tc, GLA_HV), row),
                  pl.BlockSpec((blocks, dec.shape[1], GLA_HK), lambda b, t: (b * steps + t, 0, 0))],
        out_specs=pl.BlockSpec((tc, GLA_HV), row),
        out_shape=jax.ShapeDtypeStruct((n, GLA_HV), BF16),
        scratch_shapes=[pltpu.VMEM((GLA_HEADS, GLA_DV, GLA_DK), F32)],
        compiler_params=_params("arbitrary", "arbitrary"),
        name="gla_recurrence",
    )(q, kd, vT, sz, dec)


def _out_ln_kernel(o_ref, x_ref, w_ref, g_ref, b_ref, y_ref, wb_ref):
    sub = OUT_SUB
    nsub = o_ref.shape[0] // sub

    @pl.when(pl.program_id(0) == 0)
    def _():
        wb_ref[...] = w_ref[...].astype(BF16)

    def proj(r):
        return _dot(o_ref[r * sub:(r + 1) * sub, :], wb_ref[...])

    y_next = proj(0)
    for r in range(nsub):
        rows = slice(r * sub, (r + 1) * sub)
        y = y_next
        if r + 1 < nsub:
            y_next = proj(r + 1)
        v = DEEPNORM_ALPHA * x_ref[rows, :] + y
        mu = jnp.mean(v, axis=-1, keepdims=True)
        vc = v - mu
        var = jnp.mean(vc * vc, axis=-1, keepdims=True)
        y_ref[rows, :] = vc * lax.rsqrt(var + LN_EPS) * g_ref[...] + b_ref[...]


def _out_proj_ln(o, x2, w_out, g, b):
    n = x2.shape[0]
    bm = OUT_BM
    return pl.pallas_call(
        _out_ln_kernel,
        grid=(n // bm,),
        in_specs=[pl.BlockSpec((bm, o.shape[1]), lambda i: (i, 0)),
                  pl.BlockSpec((bm, D_MODEL), lambda i: (i, 0)),
                  _resident((o.shape[1], D_MODEL)),
                  _resident((1, D_MODEL)),
                  _resident((1, D_MODEL))],
        out_specs=pl.BlockSpec((bm, D_MODEL), lambda i: (i, 0)),
        out_shape=jax.ShapeDtypeStruct((n, D_MODEL), F32),
        scratch_shapes=[pltpu.VMEM((o.shape[1], D_MODEL), BF16)],
        compiler_params=_params("arbitrary"),
        name="out_proj_ln",
    )(o, x2, w_out, g.reshape(1, D_MODEL), b.reshape(1, D_MODEL))


def _mla_in_kernel(x_ref, wc_ref, wz_ref, wr_ref, gq_ref, gkv_ref, pos_ref, freq_ref,
                   cq_ref, ckv_ref, z_ref, kr_ref):
    xb = x_ref[...].astype(BF16)
    c = _dot(xb, wc_ref[:, :MLA_Q_RANK + MLA_KV_RANK])
    cq_ref[...] = _rms_rows(c[:, :MLA_Q_RANK], gq_ref[...]).astype(BF16)
    ckv_ref[...] = _rms_rows(c[:, MLA_Q_RANK:], gkv_ref[...]).astype(BF16)
    z_ref[...] = _silu(_dot(xb, wz_ref[:, :MLA_HV])).astype(BF16)
    a = _dot_nt(xb, wr_ref[...]) * _rope_table_t(pos_ref, freq_ref).T
    kr_ref[...] = (a + pltpu.roll(a, LANES // 2, axis=1)).astype(BF16)


def _mla_in_proj(x2, w_in, g_q, g_kv, pos_row):
    n = x2.shape[0]
    bm = MLA_IN_BM
    nc = MLA_Q_RANK + MLA_KV_RANK
    half = MLA_ROPE // 2
    wT = jnp.swapaxes(w_in, 0, 1)
    wc = _prep_transposed(wT, nc // PREP_BR, PREP_BR, 1, lambda j: j)
    wz = _prep_transposed(wT, MLA_HV // (4 * MLA_ROPE), MLA_ROPE, 4,
                          lambda j: nc // MLA_ROPE + 1 + 4 * j)
    wr = _prep_rows(wT, nc // MLA_ROPE, 1, MLA_ROPE)
    t1, t2 = wr[:half], wr[half:]
    wr4T = jnp.concatenate([t1, t2, t2, t1], axis=0)
    row = lambda i: (i, 0)
    return pl.pallas_call(
        _mla_in_kernel,
        grid=(n // bm,),
        in_specs=[pl.BlockSpec((bm, D_MODEL), row),
                  _resident(wc.shape),
                  _resident(wz.shape),
                  _resident((LANES, D_MODEL)),
                  _resident((1, MLA_Q_RANK)),
                  _resident((1, MLA_KV_RANK)),
                  pl.BlockSpec((1, bm), lambda i: (0, i)),
                  _resident((MLA_ROPE // 2, 1))],
        out_specs=[pl.BlockSpec((bm, MLA_Q_RANK), row),
                   pl.BlockSpec((bm, MLA_KV_RANK), row),
                   pl.BlockSpec((bm, MLA_HV), row),
                   pl.BlockSpec((bm, LANES), row)],
        out_shape=[jax.ShapeDtypeStruct((n, MLA_Q_RANK), BF16),
                   jax.ShapeDtypeStruct((n, MLA_KV_RANK), BF16),
                   jax.ShapeDtypeStruct((n, MLA_HV), BF16),
                   jax.ShapeDtypeStruct((n, LANES), BF16)],
        compiler_params=_params("arbitrary"),
        name="mla_in_proj",
    )(x2, wc, wz, wr4T, g_q.reshape(1, -1), g_kv.reshape(1, -1), pos_row, _rope_freqs())


def _mla_up_kernel(cq_ref, ckv_ref, wqT_ref, wk_ref, wvT_ref, pos_ref, freq_ref,
                   qT_ref, k_ref, vT_ref):
    cq = cq_ref[...]
    ckv = ckv_ref[...]
    scale = MLA_QK ** -0.5 * math.log2(math.e)
    tab = _rope_table_t(pos_ref, freq_ref) * scale
    heads = 4
    half = MLA_ROPE // 2
    for g in range(0, MLA_HEADS, heads):
        acc = _dot_nt(wqT_ref[g * MLA_QK:(g + heads) * MLA_QK, :], cq)
        for h in range(heads):
            a0 = h * MLA_QK
            r0 = (g + h) * QHEAD
            t1 = acc[a0 + MLA_NOPE:a0 + MLA_NOPE + half]
            t2 = acc[a0 + MLA_NOPE + half:a0 + MLA_QK]
            rot = jnp.concatenate([t1, t2, t2, t1], axis=0) * tab
            qT_ref[r0:r0 + LANES, :] = (acc[a0:a0 + MLA_NOPE] * scale).astype(BF16)
            qT_ref[r0 + LANES:r0 + QHEAD, :] = rot.astype(BF16)
    k_ref[...] = _dot(ckv, wk_ref[...]).astype(BF16)
    vT_ref[...] = _dot_nt(wvT_ref[...], ckv).astype(BF16)


def _up_weights_kernel(wq_ref, wkv_ref, wqT_ref, wk_ref, wvT_ref):
    wqT_ref[...] = wq_ref[...].astype(BF16).T
    per_head = MLA_NOPE + MLA_V
    for h in range(UP_PREP_HEADS):
        c0 = h * per_head
        wk_ref[:, h * MLA_NOPE:(h + 1) * MLA_NOPE] = wkv_ref[:, c0:c0 + MLA_NOPE].astype(BF16)
        wvT_ref[h * MLA_V:(h + 1) * MLA_V, :] = wkv_ref[:, c0 + MLA_NOPE:c0 + per_head].astype(BF16).T


def _up_weights(w_uq, w_ukv):
    hb = UP_PREP_HEADS
    row = lambda j: (j, 0)
    col = lambda j: (0, j)
    return pl.pallas_call(
        _up_weights_kernel,
        grid=(MLA_HEADS // hb,),
        in_specs=[pl.BlockSpec((MLA_Q_RANK, hb * MLA_QK), col),
                  pl.BlockSpec((MLA_KV_RANK, hb * (MLA_NOPE + MLA_V)), col)],
        out_specs=[pl.BlockSpec((hb * MLA_QK, MLA_Q_RANK), row),
                   pl.BlockSpec((MLA_KV_RANK, hb * MLA_NOPE), col),
                   pl.BlockSpec((hb * MLA_V, MLA_KV_RANK), row)],
        out_shape=[jax.ShapeDtypeStruct((MLA_HEADS * MLA_QK, MLA_Q_RANK), BF16),
                   jax.ShapeDtypeStruct((MLA_KV_RANK, MLA_HEADS * MLA_NOPE), BF16),
                   jax.ShapeDtypeStruct((MLA_HV, MLA_KV_RANK), BF16)],
        compiler_params=_params("arbitrary"),
        name="prep_up_weights",
    )(w_uq, w_ukv)


def _mla_up_proj(cq, ckv, w_uq, w_ukv, pos_row):
    n = cq.shape[0]
    bm = UP_BM
    wqT, wk, wvT = _up_weights(w_uq, w_ukv)
    row = lambda i: (i, 0)
    col = lambda i: (0, i)
    return pl.pallas_call(
        _mla_up_kernel,
        grid=(n // bm,),
        in_specs=[pl.BlockSpec((bm, MLA_Q_RANK), row),
                  pl.BlockSpec((bm, MLA_KV_RANK), row),
                  _resident(wqT.shape),
                  _resident(wk.shape),
                  _resident(wvT.shape),
                  pl.BlockSpec((1, bm), col),
                  _resident((MLA_ROPE // 2, 1))],
        out_specs=[pl.BlockSpec((MLA_HEADS * QHEAD, bm), col),
                   pl.BlockSpec((bm, MLA_HEADS * MLA_NOPE), row),
                   pl.BlockSpec((MLA_HV, bm), col)],
        out_shape=[jax.ShapeDtypeStruct((MLA_HEADS * QHEAD, n), BF16),
                   jax.ShapeDtypeStruct((n, MLA_HEADS * MLA_NOPE), BF16),
                   jax.ShapeDtypeStruct((MLA_HV, n), BF16)],
        compiler_params=_params("arbitrary"),
        name="mla_up_proj",
    )(cq, ckv, wqT, wk, wvT, pos_row, _rope_freqs())


def _mla_attn_kernel(qT_ref, k_ref, kr_ref, vT_ref, z_ref, qTn_ref, kn_ref, krn_ref,
                     o_ref, s_ref, m_ref):
    t = ATT_T
    nt = qT_ref.shape[1] // t
    kk = lax.broadcasted_iota(jnp.int32, (t, t), 0) // CHUNK
    qq = lax.broadcasted_iota(jnp.int32, (t, t), 1) // CHUNK
    visible = kk <= qq
    ones = jnp.ones((BF16_ROWS, t), BF16)
    blocks = [(i, j) for i in range(nt) for j in range(i + 1)]

    def score_steps(slot, q_tile, k_block, kr):
        m = [None] * nt

        def score(idx, i, j):
            ks = slice(j * t, (j + 1) * t)
            kcat = jnp.concatenate([k_block(ks), kr[ks, :]], axis=1)
            s = _dot(kcat, q_tile(i))
            if j == i:
                s = jnp.where(visible, s, -jnp.inf)
            m_blk = jnp.max(s, axis=0, keepdims=True)
            m[i] = m_blk if m[i] is None else jnp.maximum(m[i], m_blk)
            s_ref[slot, idx] = s
            if j == i:
                m_ref[slot, i] = m[i]

        return [functools.partial(score, idx, i, j) for idx, (i, j) in enumerate(blocks)]

    def local_score_steps(h):
        return score_steps(
            h % 2,
            lambda i: qT_ref[h * QHEAD:(h + 1) * QHEAD, i * t:(i + 1) * t],
            lambda ks: k_ref[ks, h * MLA_NOPE:(h + 1) * MLA_NOPE],
            kr_ref)

    def value_steps(h):
        slot = h % 2
        acc = [None] * nt

        def value(idx, i, j):
            p = jnp.exp2(s_ref[slot, idx] - m_ref[slot, i]).astype(BF16)
            v_aug = jnp.concatenate([vT_ref[h * MLA_V:(h + 1) * MLA_V, j * t:(j + 1) * t],
                                     ones], axis=0)
            pv = _dot(v_aug, p)
            acc[i] = pv if acc[i] is None else acc[i] + pv
            if j == i:
                a = acc[i]
                o = (a[:MLA_V] * (1.0 / a[MLA_V:MLA_V + 1])).T
                rows = slice(i * t, (i + 1) * t)
                cols = slice(h * MLA_V, (h + 1) * MLA_V)
                o_ref[rows, cols] = (o * z_ref[rows, cols]).astype(BF16)

        return [functools.partial(value, idx, i, j) for idx, (i, j) in enumerate(blocks)]

    @pl.when((pl.program_id(0) == 0) & (pl.program_id(1) == 0))
    def _():
        for f in local_score_steps(0):
            f()

    assert ATT_HB % 2 == 0
    for h in range(ATT_HB):
        if h + 1 < ATT_HB:
            nxt = local_score_steps(h + 1)
        else:
            nxt = score_steps(0, lambda i: qTn_ref[:, i * t:(i + 1) * t],
                              lambda ks: kn_ref[ks, :], krn_ref)
        for f_score, f_value in zip(nxt, value_steps(h)):
            f_score()
            f_value()


def _mla_attention(qT, k, kr, vT, z, batch, seq):
    n = k.shape[0]
    hb = ATT_HB
    groups = MLA_HEADS // hb
    nt = seq // ATT_T
    n_blocks = nt * (nt + 1) // 2

    def next_step(b, g):
        s = jnp.minimum(b * groups + g + 1, batch * groups - 1)
        return s // groups, s % groups

    return pl.pallas_call(
        _mla_attn_kernel,
        grid=(batch, groups),
        in_specs=[pl.BlockSpec((hb * QHEAD, seq), lambda b, g: (g, b)),
                  pl.BlockSpec((seq, hb * MLA_NOPE), lambda b, g: (b, g)),
                  pl.BlockSpec((seq, LANES), lambda b, g: (b, 0)),
                  pl.BlockSpec((hb * MLA_V, seq), lambda b, g: (g, b)),
                  pl.BlockSpec((seq, hb * MLA_V), lambda b, g: (b, g)),
                  pl.BlockSpec((QHEAD, seq), lambda b, g: (hb * next_step(b, g)[1], next_step(b, g)[0])),
                  pl.BlockSpec((seq, MLA_NOPE), lambda b, g: (next_step(b, g)[0], hb * next_step(b, g)[1])),
                  pl.BlockSpec((seq, LANES), lambda b, g: (next_step(b, g)[0], 0))],
        out_specs=pl.BlockSpec((seq, hb * MLA_V), lambda b, g: (b, g)),
        out_shape=jax.ShapeDtypeStruct((n, MLA_HV), BF16),
        scratch_shapes=[pltpu.VMEM((2, n_blocks, ATT_T, ATT_T), F32),
                        pltpu.VMEM((2, nt, 1, ATT_T), F32)],
        compiler_params=_params("arbitrary", "arbitrary"),
        name="mla_attention",
    )(qT, k, kr, vT, z, qT, k, kr)


def kernel(x, positions, gla_w_in, gla_w_a2, gla_b_a, gla_g_out, gla_w_out,
           mla_w_in, mla_g_q, mla_w_uq, mla_g_kv, mla_w_ukv, mla_w_out, ln_g, ln_b):
    batch, seq, d = x.shape
    assert d == D_MODEL and seq % GLA_TC == 0 and seq % ATT_T == 0
    x2 = x.reshape(batch * seq, d)

    q, kd, sz, vT, dec = _gla_in_proj(x2, gla_w_in[0], gla_w_a2[0], gla_b_a[0], gla_g_out[0])
    o = _gla_recurrence(q, kd, vT, sz, dec, batch, seq)
    x2 = _out_proj_ln(o, x2, gla_w_out[0], ln_g[0], ln_b[0])

    pos_row = positions.reshape(1, batch * seq)
    cq, ckv, z, kr = _mla_in_proj(x2, mla_w_in[0], mla_g_q[0], mla_g_kv[0], pos_row)
    qT, kn, vT = _mla_up_proj(cq, ckv, mla_w_uq[0], mla_w_ukv[0], pos_row)
    o = _mla_attention(qT, kn, kr, vT, z, batch, seq)
    x2 = _out_proj_ln(o, x2, mla_w_out[0], ln_g[1], ln_b[1])
    return x2.reshape(batch, seq, d)
```
